```python
import math
import jax, jax.numpy as jnp
from jax import lax
import numpy as np

D_MODEL = 2048
BATCH = 4
SEQ = 4096
DEPTH = 2

HEAD_DIM = 128
A_HEADS = 6
A_HALF = HEAD_DIM // 2
A_WIDTH = A_HEADS * HEAD_DIM
B_HEADS = 6
B_WIDTH = B_HEADS * HEAD_DIM
B_Q_RANK = 512
B_KV_RANK = 256
IDX_HEADS = 16
IDX_DIM = 64
TOPK_MAX = 256
C_GROUPS = 4
C_GROUP_DIM = 128
C_WIDTH = C_GROUPS * C_GROUP_DIM
CHUNK = 128
MIX_WIDTH = A_WIDTH + B_WIDTH + C_WIDTH
IN_SIZES = (A_WIDTH, A_WIDTH, A_WIDTH, B_Q_RANK, B_KV_RANK, IDX_DIM, IDX_HEADS, C_WIDTH, C_WIDTH)
IN_COLS = sum(IN_SIZES)
MEM_LEN = 256
X_HEADS = 4
X_DIM = 128
FFN_HIDDEN = ((-(-8 * D_MODEL // 3) + 255) // 256) * 256
REL_BUCKETS = 32
REL_MAX_DIST = 128
REL_HEADS = A_HEADS + B_HEADS
Q_BLOCK = 128
EPS = 1e-6

kernel_name = "hybrid_diffattn_dsa_gmlp_block"


def rms_norm(x, g):
    xf = x.astype(jnp.float32)
    y = xf * lax.rsqrt(jnp.mean(xf * xf, axis=-1, keepdims=True) + EPS)
    return (y * g.astype(jnp.float32)).astype(x.dtype)


def layer_norm(x, g, b):
    xf = x.astype(jnp.float32)
    mu = jnp.mean(xf, axis=-1, keepdims=True)
    xc = xf - mu
    var = jnp.mean(xc * xc, axis=-1, keepdims=True)
    return (xc * lax.rsqrt(var + EPS) * g.astype(jnp.float32) + b.astype(jnp.float32)).astype(x.dtype)


def rel_bucket(dist):
    n = jnp.maximum(dist, 0)
    max_exact = REL_BUCKETS // 2
    nf = jnp.maximum(n, 1).astype(jnp.float32)
    large = max_exact + (jnp.log(nf / max_exact) / math.log(REL_MAX_DIST / max_exact)
                         * (REL_BUCKETS - max_exact)).astype(jnp.int32)
    large = jnp.minimum(large, REL_BUCKETS - 1)
    return jnp.where(n < max_exact, n, large)


def to_blocks(t):
    b, s = t.shape[:2]
    return jnp.moveaxis(t.reshape((b, s // Q_BLOCK, Q_BLOCK) + t.shape[2:]), 1, 0)


def from_blocks(t):
    t = jnp.moveaxis(t, 0, 1)
    return t.reshape((t.shape[0], t.shape[1] * t.shape[2]) + t.shape[3:])


def diff_attention(q, k, v, lam, bias_tab, sub_g, lam_init):
    s_len = q.shape[1]
    kpos = jnp.arange(s_len)
    scale = A_HALF ** -0.5

    def block(args):
        qi, i = args
        qpos = i * Q_BLOCK + jnp.arange(Q_BLOCK)
        dist = qpos[:, None] - kpos[None, :]
        bias = jnp.transpose(bias_tab[rel_bucket(dist)], (2, 0, 1)).astype(jnp.float32)
        logits = jnp.einsum('bqhmd,bshmd->bhmqs', qi, k).astype(jnp.float32) * scale
        logits = logits + bias[None, :, None]
        logits = jnp.where((dist >= 0)[None, None, None], logits, -jnp.inf)
        p = jax.nn.softmax(logits, axis=-1)
        attn = p[:, :, 0] - lam * p[:, :, 1]
        return jnp.einsum('bhqs,bshd->bqhd', attn.astype(v.dtype), v)

    nblk = s_len // Q_BLOCK
    o = from_blocks(lax.map(block, (to_blocks(q), jnp.arange(nblk))))
    return rms_norm(o, sub_g) * (1.0 - lam_init)


def dsa_attention(c_q, c_kv, k_idx, w_idx, w_uq, w_qidx, w_uk, w_uv, bias_tab):
    bsz, s_len = c_q.shape[:2]
    topk = min(TOPK_MAX, s_len // 4)
    q = jnp.einsum('bsr,rhd->bshd', c_q, w_uq)
    q_lat = jnp.einsum('bshd,rhd->bshr', q, w_uk)
    q_idx = jnp.einsum('bsr,rgd->bsgd', c_q, w_qidx)
    w_idx = w_idx * (IDX_HEADS ** -0.5)
    kpos = jnp.arange(s_len)
    bidx = jnp.arange(bsz)[:, None, None]

    def block(args):
        qi, wi, ql, i = args
        qpos = i * Q_BLOCK + jnp.arange(Q_BLOCK)
        dots = jnp.einsum('bqgd,bsd->bqsg', qi, k_idx) * (IDX_DIM ** -0.5)
        score = jnp.einsum('bqsg,bqg->bqs', jax.nn.relu(dots), wi).astype(jnp.float32)
        causal = kpos[None, :] <= qpos[:, None]
        score = jnp.where(causal[None], score, -jnp.inf)
        top_score, idx = lax.top_k(score, topk)
        valid = jnp.isfinite(top_score)
        c_sel = c_kv[bidx, idx]
        dist = qpos[None, :, None] - idx
        bias = jnp.transpose(bias_tab[rel_bucket(dist)], (0, 3, 1, 2)).astype(jnp.float32)
        logits = jnp.einsum('bqhr,bqkr->bhqk', ql, c_sel).astype(jnp.float32) * (HEAD_DIM ** -0.5) + bias
        logits = jnp.where(valid[:, None], logits, -jnp.inf)
        p = jax.nn.softmax(logits, axis=-1)
        return jnp.einsum('bhqk,bqkr->bqhr', p.astype(c_sel.dtype), c_sel)

    nblk = s_len // Q_BLOCK
    o_lat = from_blocks(lax.map(block, (to_blocks(q_idx), to_blocks(w_idx), to_blocks(q_lat),
                                        jnp.arange(nblk))))
    return jnp.einsum('bshr,rhd->bshd', o_lat, w_uv)


def chunk_spatial_gating(u, v, ln_g, ln_b, w_s, b_s):
    v = layer_norm(v, ln_g, ln_b)
    bsz, s_len = v.shape[:2]
    vc = v.reshape(bsz, s_len // CHUNK, CHUNK, C_GROUPS, C_GROUP_DIM)
    mask = jnp.tril(jnp.ones((CHUNK, CHUNK), dtype=bool))
    w = jnp.where(mask[None], w_s, jnp.zeros_like(w_s))
    y = jnp.einsum('gts,bnsgc->bntgc', w, vc) + jnp.transpose(b_s)[None, None, :, :, None]
    return u * y.reshape(bsz, s_len, C_WIDTH)


def memory_attention(h, mem_n, wq, wkv, wo):
    q = jnp.einsum('bsd,dhe->bshe', h, wq)
    kv = jnp.einsum('bmd,dche->bmche', mem_n, wkv)
    k, v = kv[:, :, 0], kv[:, :, 1]
    logits = jnp.einsum('bshe,bmhe->bhsm', q, k).astype(jnp.float32) * (X_DIM ** -0.5)
    p = jax.nn.softmax(logits, axis=-1)
    o = jnp.einsum('bhsm,bmhe->bshe', p.astype(v.dtype), v)
    return jnp.einsum('bshe,hed->bsd', o, wo)


def swiglu(h, w_gu, w_down):
    gu = jnp.einsum('bsd,dcf->bscf', h, w_gu)
    return jnp.einsum('bsf,fd->bsd', jax.nn.silu(gu[:, :, 0]) * gu[:, :, 1], w_down)


def setup_inputs(seed: int = 0) -> dict:
    key = jax.random.key(seed)
    ks = iter(jax.random.split(key, 40))
    f32 = jnp.float32

    def nrm(shape, scale):
        return jax.random.normal(next(ks), shape, f32) * scale

    def gain(shape):
        return 1.0 + 0.05 * jax.random.normal(next(ks), shape, f32)

    L = DEPTH
    return {
        "x": nrm((BATCH, SEQ, D_MODEL), 1.0),
        "mem": nrm((BATCH, MEM_LEN, D_MODEL), 1.0),
        "rel_bias": nrm((REL_BUCKETS, REL_HEADS), 0.5),
        "mix_pre_g": gain((L, D_MODEL)),
        "mix_post_g": gain((L, D_MODEL)),
        "w_in": nrm((L, D_MODEL, IN_COLS), D_MODEL ** -0.5),
        "w_out": nrm((L, MIX_WIDTH, D_MODEL), MIX_WIDTH ** -0.5),
        "a_lambda": nrm((L, 4, A_HALF), 0.1),
        "a_sub_g": gain((L, HEAD_DIM)),
        "b_cq_g": gain((L, B_Q_RANK)),
        "b_ckv_g": gain((L, B_KV_RANK)),
        "b_w_uq": nrm((L, B_Q_RANK, B_HEADS, HEAD_DIM), B_Q_RANK ** -0.5),
        "b_w_qidx": nrm((L, B_Q_RANK, IDX_HEADS, IDX_DIM), B_Q_RANK ** -0.5),
        "b_w_uk": nrm((L, B_KV_RANK, B_HEADS, HEAD_DIM), B_KV_RANK ** -0.5),
        "b_w_uv": nrm((L, B_KV_RANK, B_HEADS, HEAD_DIM), B_KV_RANK ** -0.5),
        "c_ln_g": gain((L, C_WIDTH)),
        "c_ln_b": nrm((L, C_WIDTH), 0.02),
        "c_w_s": nrm((L, C_GROUPS, CHUNK, CHUNK), CHUNK ** -0.5),
        "c_b_s": 1.0 + nrm((L, C_GROUPS, CHUNK), 0.1),
        "x_pre_g": gain((L, D_MODEL)),
        "x_post_g": gain((L, D_MODEL)),
        "mem_g": gain((L, D_MODEL)),
        "x_wq": nrm((L, D_MODEL, X_HEADS, X_DIM), D_MODEL ** -0.5),
        "x_wkv": nrm((L, D_MODEL, 2, X_HEADS, X_DIM), D_MODEL ** -0.5),
        "x_wo": nrm((L, X_HEADS, X_DIM, D_MODEL), (X_HEADS * X_DIM) ** -0.5),
        "f_pre_g": gain((L, D_MODEL)),
        "f_post_g": gain((L, D_MODEL)),
        "f_w_gu": nrm((L, D_MODEL, 2, FFN_HIDDEN), D_MODEL ** -0.5),
        "f_w_down": nrm((L, FFN_HIDDEN, D_MODEL), FFN_HIDDEN ** -0.5),
    }


def reference(x, mem, rel_bias, mix_pre_g, mix_post_g, w_in, w_out, a_lambda, a_sub_g,
              b_cq_g, b_ckv_g, b_w_uq, b_w_qidx, b_w_uk, b_w_uv,
              c_ln_g, c_ln_b, c_w_s, c_b_s,
              x_pre_g, x_post_g, mem_g, x_wq, x_wkv, x_wo,
              f_pre_g, f_post_g, f_w_gu, f_w_down):
    bsz, s_len, _ = x.shape
    bias_a = rel_bias[:, :A_HEADS]
    bias_b = rel_bias[:, A_HEADS:]
    split_at = np.cumsum(IN_SIZES)[:-1].tolist()
    h = x
    for l in range(DEPTH):
        lam_init = 0.8 - 0.6 * math.exp(-0.3 * l)
        xn = rms_norm(h, mix_pre_g[l])
        proj = jnp.einsum('bsd,dc->bsc', xn, w_in[l])
        qa, ka, va, cq, ckv, kidx, widx, cu, cv = jnp.split(proj, split_at, axis=-1)
        lp = a_lambda[l].astype(jnp.float32)
        lam = jnp.exp(jnp.sum(lp[0] * lp[1])) - jnp.exp(jnp.sum(lp[2] * lp[3])) + lam_init
        oa = diff_attention(qa.reshape(bsz, s_len, A_HEADS, 2, A_HALF),
                            ka.reshape(bsz, s_len, A_HEADS, 2, A_HALF),
                            va.reshape(bsz, s_len, A_HEADS, HEAD_DIM),
                            lam, bias_a, a_sub_g[l], lam_init)
        ob = dsa_attention(rms_norm(cq, b_cq_g[l]), rms_norm(ckv, b_ckv_g[l]), kidx, widx,
                           b_w_uq[l], b_w_qidx[l], b_w_uk[l], b_w_uv[l], bias_b)
        oc = chunk_spatial_gating(jax.nn.gelu(cu), jax.nn.gelu(cv), c_ln_g[l], c_ln_b[l],
                                  c_w_s[l], c_b_s[l])
        mix = jnp.concatenate([oa.reshape(bsz, s_len, A_WIDTH),
                               ob.reshape(bsz, s_len, B_WIDTH), oc], axis=-1)
        h = h + rms_norm(jnp.einsum('bsc,cd->bsd', mix, w_out[l]), mix_post_g[l])
        xa = memory_attention(rms_norm(h, x_pre_g[l]), rms_norm(mem, mem_g[l]),
                              x_wq[l], x_wkv[l], x_wo[l])
        h = h + rms_norm(xa, x_post_g[l])
        ff = swiglu(rms_norm(h, f_pre_g[l]), f_w_gu[l], f_w_down[l])
        h = h + rms_norm(ff, f_post_g[l])
    return h
```

```python
import functools
import math

import numpy as np
import jax
import jax.numpy as jnp
from jax import lax
from jax.experimental import pallas as pl
from jax.experimental.pallas import tpu as pltpu

F32 = jnp.float32
BF16 = jnp.bfloat16
EPS = 1e-6
NEG = -1e30
INT_MIN = -(2 ** 31)

HEAD_DIM = 128
A_HEADS = 6
A_HALF = 64
B_HEADS = 6
B_Q_RANK = 512
B_KV_RANK = 256
IDX_HEADS = 16
IDX_DIM = 64
TOPK_MAX = 256
C_GROUPS = 4
C_WIDTH = 512
CHUNK = 128
X_HEADS = 4
X_DIM = 128
REL_BUCKETS = 32
REL_MAX_DIST = 128
A_WIDTH = A_HEADS * HEAD_DIM
B_WIDTH = B_HEADS * HEAD_DIM

ATTN_BLOCK = 256
MIB = 1024 * 1024


def _params(semantics, vmem_mib):
    return pltpu.CompilerParams(dimension_semantics=semantics,
                                vmem_limit_bytes=vmem_mib * MIB)


def _rms(x, g):
    return x * lax.rsqrt(jnp.mean(x * x, axis=-1, keepdims=True) + EPS) * g


def _dot(a, b):
    return jnp.dot(a, b, preferred_element_type=F32)


def _dot_nt(a, b):
    return lax.dot_general(a, b, (((1,), (1,)), ((), ())), preferred_element_type=F32)


def _norm_matmul_kernel(x_ref, g_ref, w_ref, o_ref, xn_ref):
    @pl.when(pl.program_id(1) == 0)
    def _():
        xn_ref[...] = _rms(x_ref[...], g_ref[...]).astype(BF16)

    o_ref[...] = _dot(xn_ref[...], w_ref[...]).astype(o_ref.dtype)


def _norm_matmul(x, g, w, out_dtype, tm, tn):
    m, d = x.shape
    n = w.shape[1]
    return pl.pallas_call(
        _norm_matmul_kernel,
        grid=(m // tm, n // tn),
        in_specs=[pl.BlockSpec((tm, d), lambda i, j: (i, 0)),
                  pl.BlockSpec((1, d), lambda i, j: (0, 0)),
                  pl.BlockSpec((d, tn), lambda i, j: (0, j))],
        out_specs=pl.BlockSpec((tm, tn), lambda i, j: (i, j)),
        out_shape=jax.ShapeDtypeStruct((m, n), out_dtype),
        scratch_shapes=[pltpu.VMEM((tm, d), BF16)],
        compiler_params=_params(("parallel", "arbitrary"), 48),
        name="norm_matmul",
    )(x, g.reshape(1, d), w)


def _softmax_step(s, vb, m_ref, l_ref, acc_ref):
    m_prev = m_ref[...]
    m_new = jnp.maximum(m_prev, jnp.max(s, axis=-1, keepdims=True))
    alpha = jnp.exp(m_prev - m_new)
    p = jnp.exp(s - m_new)
    l_ref[...] = alpha * l_ref[...] + jnp.sum(p, axis=-1, keepdims=True)
    acc_ref[...] = alpha * acc_ref[...] + _dot(p.astype(BF16), vb)
    m_ref[...] = m_new


def _bucket_np(dist):
    n = np.maximum(dist, 0)
    max_exact = REL_BUCKETS // 2
    nf = np.maximum(n, 1).astype(np.float64)
    large = max_exact + (np.log(nf / max_exact) / math.log(REL_MAX_DIST / max_exact)
                         * (REL_BUCKETS - max_exact)).astype(np.int32)
    large = np.minimum(large, REL_BUCKETS - 1)
    return np.where(n < max_exact, n, large)


def _bias_tiles(tab, t):
    assert t >= REL_MAX_DIST
    i = np.arange(t)[:, None]
    j = np.arange(t)[None, :]
    idx = np.stack([_bucket_np(i - j), _bucket_np(t + i - j)])
    tiles = jnp.take(tab, jnp.asarray(idx), axis=0) - tab[REL_BUCKETS - 1]
    tiles = jnp.transpose(tiles, (3, 0, 1, 2))
    causal = jnp.asarray(np.stack([i >= j, np.ones((t, t), bool)]))
    return jnp.where(causal[None], tiles, NEG).astype(F32)


def _diff_attn_kernel(lam_ref, q_ref, k_ref, v_ref, bias_ref, g_ref, o_ref,
                      m_ref, l_ref, acc_ref, *, t, out_scale):
    qi = pl.program_id(2)
    q = q_ref[0]
    lane = lax.broadcasted_iota(jnp.int32, q.shape, 1)
    zero = jnp.zeros_like(q)
    qs = jnp.concatenate([jnp.where(lane < A_HALF, q, zero),
                          jnp.where(lane >= A_HALF, q, zero)], axis=0)
    m_ref[...] = jnp.full(m_ref.shape, NEG, F32)
    l_ref[...] = jnp.zeros(l_ref.shape, F32)
    acc_ref[...] = jnp.zeros(acc_ref.shape, F32)

    def block(kb, bias):
        ks = pl.multiple_of(kb * t, t)
        s = _dot_nt(qs, k_ref[0, pl.ds(ks, t), :])
        if bias is not None:
            s = s + jnp.concatenate([bias, bias], axis=0)
        _softmax_step(s, v_ref[0, pl.ds(ks, t), :], m_ref, l_ref, acc_ref)

    def far(kb, c):
        block(kb, None)
        return c

    lax.fori_loop(0, jnp.maximum(qi - 1, 0), far, 0)

    @pl.when(qi >= 1)
    def _():
        block(qi - 1, bias_ref[0, 1])

    block(qi, bias_ref[0, 0])

    o = acc_ref[...] * (1.0 / l_ref[...])
    o = o[:t] - lam_ref[0] * o[t:]
    o_ref[0] = (_rms(o, g_ref[...]) * out_scale).astype(o_ref.dtype)


def _diff_attention(qkv, lam, bias, sub_g, out_scale):
    b, s, _ = qkv.shape
    t = min(ATTN_BLOCK, s)
    kern = functools.partial(_diff_attn_kernel, t=t, out_scale=out_scale)
    return pl.pallas_call(
        kern,
        grid=(b, A_HEADS, s // t),
        in_specs=[pl.BlockSpec(memory_space=pltpu.SMEM),
                  pl.BlockSpec((1, t, HEAD_DIM), lambda bi, h, qi: (bi, qi, h)),
                  pl.BlockSpec((1, s, HEAD_DIM), lambda bi, h, qi: (bi, 0, A_HEADS + h)),
                  pl.BlockSpec((1, s, HEAD_DIM), lambda bi, h, qi: (bi, 0, 2 * A_HEADS + h)),
                  pl.BlockSpec((1, 2, t, t), lambda bi, h, qi: (h, 0, 0, 0)),
                  pl.BlockSpec((1, HEAD_DIM), lambda bi, h, qi: (0, 0))],
        out_specs=pl.BlockSpec((1, t, HEAD_DIM), lambda bi, h, qi: (bi, qi, h)),
        out_shape=jax.ShapeDtypeStruct((b, s, A_WIDTH), BF16),
        scratch_shapes=[pltpu.VMEM((2 * t, 1), F32), pltpu.VMEM((2 * t, 1), F32),
                        pltpu.VMEM((2 * t, HEAD_DIM), F32)],
        compiler_params=_params(("parallel", "parallel", "arbitrary"), 32),
        name="diff_attention",
    )(lam, qkv, qkv, qkv, bias, sub_g.reshape(1, HEAD_DIM))


def _dsa_prep_kernel(cq_ref, ckv_ref, kk_ref, wi_ref, gq_ref, gkv_ref, wuq_ref, wuk_ref, wqi_ref,
                     qlat_ref, qidx_ref, ckvn_ref, kkb_ref, wis_ref):
    cqn = _rms(cq_ref[...], gq_ref[...]).astype(BF16)
    ckvn_ref[...] = _rms(ckv_ref[...], gkv_ref[...]).astype(BF16)
    q = _dot(cqn, wuq_ref[...])
    for h in range(B_HEADS):
        qh = q[:, h * HEAD_DIM:(h + 1) * HEAD_DIM].astype(BF16)
        qlat_ref[:, h * B_KV_RANK:(h + 1) * B_KV_RANK] = _dot(qh, wuk_ref[h]).astype(BF16)
    qidx_ref[...] = _dot(cqn, wqi_ref[...]).astype(BF16)
    kkb_ref[...] = kk_ref[...].astype(BF16)
    wis_ref[...] = wi_ref[:, :IDX_HEADS] * (IDX_HEADS ** -0.5 * IDX_DIM ** -0.5)


def _dsa_prep(rest, gq, gkv, wuq, wuk, wqi, tm):
    m = rest.shape[0]
    row = lambda c: (lambda i: (i, c))
    full2 = lambda i: (0, 0)
    return pl.pallas_call(
        _dsa_prep_kernel,
        grid=(m // tm,),
        in_specs=[pl.BlockSpec((tm, B_Q_RANK), row(2)),
                  pl.BlockSpec((tm, B_KV_RANK), row(6)),
                  pl.BlockSpec((tm, 128), row(14)),
                  pl.BlockSpec((tm, 128), row(15)),
                  pl.BlockSpec((1, B_Q_RANK), full2),
                  pl.BlockSpec((1, B_KV_RANK), full2),
                  pl.BlockSpec((B_Q_RANK, B_WIDTH), full2),
                  pl.BlockSpec((B_HEADS, HEAD_DIM, B_KV_RANK), lambda i: (0, 0, 0)),
                  pl.BlockSpec((B_Q_RANK, IDX_HEADS * IDX_DIM), full2)],
        out_specs=[pl.BlockSpec((tm, B_HEADS * B_KV_RANK), row(0)),
                   pl.BlockSpec((tm, IDX_HEADS * IDX_DIM), row(0)),
                   pl.BlockSpec((tm, B_KV_RANK), row(0)),
                   pl.BlockSpec((tm, 128), row(0)),
                   pl.BlockSpec((tm, IDX_HEADS), row(0))],
        out_shape=[jax.ShapeDtypeStruct((m, B_HEADS * B_KV_RANK), BF16),
                   jax.ShapeDtypeStruct((m, IDX_HEADS * IDX_DIM), BF16),
                   jax.ShapeDtypeStruct((m, B_KV_RANK), BF16),
                   jax.ShapeDtypeStruct((m, 128), BF16),
                   jax.ShapeDtypeStruct((m, IDX_HEADS), F32)],
        compiler_params=_params(("parallel",), 32),
        name="dsa_prep",
    )(rest, rest, rest, rest, gq.reshape(1, -1), gkv.reshape(1, -1), wuq, wuk, wqi)


def _dsa_kernel(qlat_ref, qidx_ref, wis_ref, kk_ref, ckvn_ref, bias_ref, wuv_ref, o_ref,
                key_ref, m_ref, l_ref, acc_ref, *, t, topk):
    qi = pl.program_id(1)
    nblk = qi + 1
    row = lax.broadcasted_iota(jnp.int32, (t, t), 0)
    col = lax.broadcasted_iota(jnp.int32, (t, t), 1)

    qidx = qidx_ref[0]
    wis = wis_ref[0]
    lane = lax.broadcasted_iota(jnp.int32, (t, 2 * IDX_DIM), 1)
    q_heads = []
    for g in range(IDX_HEADS):
        pair = qidx[:, (g // 2) * 2 * IDX_DIM:(g // 2 + 1) * 2 * IDX_DIM]
        keep = (lane < IDX_DIM) if g % 2 == 0 else (lane >= IDX_DIM)
        q_heads.append(jnp.where(keep, pair, jnp.zeros_like(pair)))

    def score_block(kb, c):
        ks = pl.multiple_of(kb * t, t)
        kblk = kk_ref[0, pl.ds(ks, t), :]
        score = jnp.zeros((t, t), F32)
        for g in range(IDX_HEADS):
            d = _dot_nt(q_heads[g], kblk)
            score = score + jnp.maximum(d, 0.0) * wis[:, g:g + 1]
        bits = pltpu.bitcast(score, jnp.int32)
        key = jnp.where(bits < 0, bits ^ jnp.int32(0x7FFFFFFF), bits)
        causal = (kb < qi) | (row >= col)
        key_ref[kb] = jnp.where(causal, key, jnp.int32(INT_MIN))
        return c

    lax.fori_loop(0, nblk, score_block, 0)

    def count_ge(cand):
        cand_b = jnp.broadcast_to(cand, (t, t))

        def body(kb, c):
            x = jnp.where(key_ref[kb] >= cand_b, 1.0, 0.0)
            for j in range(t // 128):
                c = c + x[:, j * 128:(j + 1) * 128]
            return c

        c = lax.fori_loop(0, nblk, body, jnp.zeros((t, 128), F32))
        return jnp.sum(c, axis=-1, keepdims=True)

    def bit_step(i, thr):
        cand = thr + jnp.left_shift(jnp.int32(1), 31 - i)
        return jnp.where(count_ge(cand) >= float(topk), cand, thr)

    thr = lax.fori_loop(0, 32, bit_step, jnp.full((t, 1), INT_MIN, jnp.int32))
    thr_b = jnp.broadcast_to(jnp.maximum(thr, jnp.int32(INT_MIN + 1)), (t, t))

    m_ref[...] = jnp.full(m_ref.shape, NEG, F32)
    l_ref[...] = jnp.zeros(l_ref.shape, F32)
    acc_ref[...] = jnp.zeros(acc_ref.shape, F32)
    qlat = qlat_ref[0]
    scale = HEAD_DIM ** -0.5

    def attend(kb, near):
        ks = pl.multiple_of(kb * t, t)
        cb = ckvn_ref[0, pl.ds(ks, t), :]
        sel = key_ref[kb] >= thr_b
        for h in range(B_HEADS):
            s = _dot_nt(qlat[:, h * B_KV_RANK:(h + 1) * B_KV_RANK], cb) * scale
            if near is not None:
                s = s + bias_ref[h, near]
            s = jnp.where(sel, s, NEG)
            _softmax_step(s, cb, m_ref.at[h], l_ref.at[h], acc_ref.at[h])

    def far(kb, c):
        attend(kb, None)
        return c

    lax.fori_loop(0, jnp.maximum(qi - 1, 0), far, 0)

    @pl.when(qi >= 1)
    def _():
        attend(qi - 1, 1)

    attend(qi, 0)

    for h in range(B_HEADS):
        o_lat = (acc_ref[h] * (1.0 / l_ref[h])).astype(BF16)
        o_ref[0, :, h * HEAD_DIM:(h + 1) * HEAD_DIM] = _dot(o_lat, wuv_ref[h]).astype(o_ref.dtype)


def _dsa_attention(qlat, qidx, wis, kk, ckvn, bias, wuv):
    b, s, _ = qlat.shape
    t = min(ATTN_BLOCK, s)
    topk = min(TOPK_MAX, s // 4)
    kern = functools.partial(_dsa_kernel, t=t, topk=topk)
    qrow = lambda bi, qi: (bi, qi, 0)
    allk = lambda bi, qi: (bi, 0, 0)
    return pl.pallas_call(
        kern,
        grid=(b, s // t),
        in_specs=[pl.BlockSpec((1, t, B_HEADS * B_KV_RANK), qrow),
                  pl.BlockSpec((1, t, IDX_HEADS * IDX_DIM), qrow),
                  pl.BlockSpec((1, t, IDX_HEADS), qrow),
                  pl.BlockSpec((1, s, 128), allk),
                  pl.BlockSpec((1, s, B_KV_RANK), allk),
                  pl.BlockSpec((B_HEADS, 2, t, t), lambda bi, qi: (0, 0, 0, 0)),
                  pl.BlockSpec((B_HEADS, B_KV_RANK, HEAD_DIM), lambda bi, qi: (0, 0, 0))],
        out_specs=pl.BlockSpec((1, t, B_WIDTH), qrow),
        out_shape=jax.ShapeDtypeStruct((b, s, B_WIDTH), BF16),
        scratch_shapes=[pltpu.VMEM((s // t, t, t), jnp.int32),
                        pltpu.VMEM((B_HEADS, t, 1), F32), pltpu.VMEM((B_HEADS, t, 1), F32),
                        pltpu.VMEM((B_HEADS, t, B_KV_RANK), F32)],
        compiler_params=_params(("parallel", "arbitrary"), 48),
        name="dsa_attention",
    )(qlat, qidx, wis, kk, ckvn, bias, wuv)


def _gelu(x):
    return x * (0.5 * (1.0 + jnp.tanh(math.sqrt(2.0 / math.pi) * (x + 0.044715 * (x * x * x)))))


def _gmlp_kernel(cu_ref, cv_ref, g_ref, b_ref, ws_ref, bs_ref, o_ref, *, ts):
    u = _gelu(cu_ref[...])
    v = _gelu(cv_ref[...])
    mu = jnp.mean(v, axis=-1, keepdims=True)
    vc = v - mu
    var = jnp.mean(vc * vc, axis=-1, keepdims=True)
    vn = (vc * lax.rsqrt(var + EPS) * g_ref[...] + b_ref[...]).astype(BF16)
    row = lax.broadcasted_iota(jnp.int32, (CHUNK, CHUNK), 0)
    col = lax.broadcasted_iota(jnp.int32, (CHUNK, CHUNK), 1)
    for g in range(C_GROUPS):
        w = jnp.where(row >= col, ws_ref[g], 0.0).astype(BF16)
        bcol = bs_ref[:, g:g + 1]
        cs = slice(g * 128, (g + 1) * 128)
        for c in range(ts // CHUNK):
            rs = slice(c * CHUNK, (c + 1) * CHUNK)
            y = _dot(w, vn[rs, cs]) + bcol
            o_ref[rs, cs] = (u[rs, cs] * y).astype(o_ref.dtype)


def _gmlp(rest, ln_g, ln_b, w_s, b_s, ts):
    m = rest.shape[0]
    kern = functools.partial(_gmlp_kernel, ts=ts)
    return pl.pallas_call(
        kern,
        grid=(m // ts,),
        in_specs=[pl.BlockSpec((ts, C_WIDTH), lambda i: (i, 0)),
                  pl.BlockSpec((ts, C_WIDTH), lambda i: (i, 1)),
                  pl.BlockSpec((1, C_WIDTH), lambda i: (0, 0)),
                  pl.BlockSpec((1, C_WIDTH), lambda i: (0, 0)),
                  pl.BlockSpec((C_GROUPS, CHUNK, CHUNK), lambda i: (0, 0, 0)),
                  pl.BlockSpec((CHUNK, C_GROUPS), lambda i: (0, 0))],
        out_specs=pl.BlockSpec((ts, C_WIDTH), lambda i: (i, 0)),
        out_shape=jax.ShapeDtypeStruct((m, C_WIDTH), BF16),
        compiler_params=_params(("parallel",), 32),
        name="gmlp",
    )(rest, rest, ln_g.reshape(1, -1), ln_b.reshape(1, -1), w_s, jnp.transpose(b_s))


def _out_proj_kernel(h_ref, oa_ref, ob_ref, oc_ref, wa_ref, wb_ref, wc_ref, g_ref, o_ref):
    y = _dot(oa_ref[...], wa_ref[...]) + _dot(ob_ref[...], wb_ref[...]) + _dot(oc_ref[...], wc_ref[...])
    o_ref[...] = h_ref[...] + _rms(y, g_ref[...])


def _out_proj(h, oa, ob, oc, wa, wb, wc, g, tm):
    m, d = h.shape
    rowblk = lambda i: (i, 0)
    full = lambda i: (0, 0)
    return pl.pallas_call(
        _out_proj_kernel,
        grid=(m // tm,),
        in_specs=[pl.BlockSpec((tm, d), rowblk),
                  pl.BlockSpec((tm, A_WIDTH), rowblk),
                  pl.BlockSpec((tm, B_WIDTH), rowblk),
                  pl.BlockSpec((tm, C_WIDTH), rowblk),
                  pl.BlockSpec((A_WIDTH, d), full),
                  pl.BlockSpec((B_WIDTH, d), full),
                  pl.BlockSpec((C_WIDTH, d), full),
                  pl.BlockSpec((1, d), full)],
        out_specs=pl.BlockSpec((tm, d), rowblk),
        out_shape=jax.ShapeDtypeStruct((m, d), F32),
        compiler_params=_params(("parallel",), 48),
        name="out_proj",
    )(h, oa, ob, oc, wa, wb, wc, g.reshape(1, d))


def _xattn_kernel(h_ref, gpre_ref, wq_ref, k_ref, v_ref, wo_ref, gpost_ref, o_ref):
    x = h_ref[0]
    xn = _rms(x, gpre_ref[...]).astype(BF16)
    q = _dot(xn, wq_ref[...])
    scale = X_DIM ** -0.5
    outs = []
    for hh in range(X_HEADS):
        cs = slice(hh * X_DIM, (hh + 1) * X_DIM)
        s = _dot_nt(q[:, cs].astype(BF16), k_ref[0, :, cs]) * scale
        p = jnp.exp(s - jnp.max(s, axis=-1, keepdims=True))
        p = p * (1.0 / jnp.sum(p, axis=-1, keepdims=True))
        outs.append(_dot(p.astype(BF16), v_ref[0, :, cs]).astype(BF16))
    o = jnp.concatenate(outs, axis=1)
    xa = _dot(o, wo_ref[...])
    o_ref[0] = x + _rms(xa, gpost_ref[...])


def _xattn(h, kv, gpre, wq, wo, gpost, tm):
    b, s, d = h.shape
    mlen = kv.shape[1]
    hw = X_HEADS * X_DIM
    return pl.pallas_call(
        _xattn_kernel,
        grid=(b, s // tm),
        in_specs=[pl.BlockSpec((1, tm, d), lambda bi, i: (bi, i, 0)),
                  pl.BlockSpec((1, d), lambda bi, i: (0, 0)),
                  pl.BlockSpec((d, hw), lambda bi, i: (0, 0)),
                  pl.BlockSpec((1, mlen, hw), lambda bi, i: (bi, 0, 0)),
                  pl.BlockSpec((1, mlen, hw), lambda bi, i: (bi, 0, 1)),
                  pl.BlockSpec((hw, d), lambda bi, i: (0, 0)),
                  pl.BlockSpec((1, d), lambda bi, i: (0, 0))],
        out_specs=pl.BlockSpec((1, tm, d), lambda bi, i: (bi, i, 0)),
        out_shape=jax.ShapeDtypeStruct((b, s, d), F32),
        compiler_params=_params(("parallel", "parallel"), 48),
        name="xattn",
    )(h, gpre.reshape(1, d), wq, kv, kv, wo, gpost.reshape(1, d))


def _ffn_kernel(x_ref, gpre_ref, wg_ref, wu_ref, wd_ref, gpost_ref, o_ref, xn_ref, acc_ref):
    j = pl.program_id(1)

    @pl.when(j == 0)
    def _():
        xn_ref[...] = _rms(x_ref[...], gpre_ref[...]).astype(BF16)
        acc_ref[...] = jnp.zeros(acc_ref.shape, F32)

    xn = xn_ref[...]
    gate = _dot(xn, wg_ref[...])
    up = _dot(xn, wu_ref[...])
    act = gate * (1.0 / (1.0 + jnp.exp(-gate))) * up
    acc_ref[...] += _dot(act.astype(BF16), wd_ref[...])

    @pl.when(j == pl.num_programs(1) - 1)
    def _():
        o_ref[...] = x_ref[...] + _rms(acc_ref[...], gpost_ref[...])


def _ffn(h, gpre, w_gu, w_down, gpost, tm, tf):
    m, d = h.shape
    f = w_down.shape[0]
    nf = f // tf
    return pl.pallas_call(
        _ffn_kernel,
        grid=(m // tm, nf),
        in_specs=[pl.BlockSpec((tm, d), lambda i, j: (i, 0)),
                  pl.BlockSpec((1, d), lambda i, j: (0, 0)),
                  pl.BlockSpec((d, tf), lambda i, j: (0, j)),
                  pl.BlockSpec((d, tf), lambda i, j: (0, nf + j)),
                  pl.BlockSpec((tf, d), lambda i, j: (j, 0)),
                  pl.BlockSpec((1, d), lambda i, j: (0, 0))],
        out_specs=pl.BlockSpec((tm, d), lambda i, j: (i, 0)),
        out_shape=jax.ShapeDtypeStruct((m, d), F32),
        scratch_shapes=[pltpu.VMEM((tm, d), BF16), pltpu.VMEM((tm, d), F32)],
        compiler_params=_params(("parallel", "arbitrary"), 56),
        name="ffn",
    )(h, gpre.reshape(1, d), w_gu, w_gu, w_down, gpost.reshape(1, d))


def _tile(n, pref):
    t = min(pref, n)
    assert n % t == 0
    return t


def kernel(x, mem, rel_bias, mix_pre_g, mix_post_g, w_in, w_out, a_lambda, a_sub_g, b_cq_g, b_ckv_g, b_w_uq, b_w_qidx, b_w_uk, b_w_uv, c_ln_g, c_ln_b, c_w_s, c_b_s, x_pre_g, x_post_g, mem_g, x_wq, x_wkv, x_wo, f_pre_g, f_post_g, f_w_gu, f_w_down):
    bsz, s_len, d = x.shape
    depth = w_in.shape[0]
    tokens = bsz * s_len
    mlen = mem.shape[1]
    t_attn = min(ATTN_BLOCK, s_len)
    bias_a = _bias_tiles(rel_bias[:, :A_HEADS], t_attn)
    bias_b = _bias_tiles(rel_bias[:, A_HEADS:], t_attn)
    sizes = (A_WIDTH, A_WIDTH, A_WIDTH, B_Q_RANK, B_KV_RANK, IDX_DIM, IDX_HEADS, C_WIDTH, C_WIDTH)
    offs = np.concatenate([[0], np.cumsum(sizes)])
    tm = _tile(tokens, 512)

    h = x.reshape(tokens, d)
    mem2 = mem.reshape(bsz * mlen, d)
    for l in range(depth):
        lam_init = 0.8 - 0.6 * math.exp(-0.3 * l)
        wl = w_in[l]
        cols = [wl[:, offs[i]:offs[i + 1]] for i in range(len(sizes))]
        wqa, wka, wva, wcq, wckv, wkidx, wwidx, wcu, wcv = cols
        w_qkv = jnp.concatenate([wqa * (A_HALF ** -0.5), wka, wva], axis=1).astype(BF16)
        w_rest = jnp.concatenate(
            [wcu, wcv, wcq, wckv, wkidx, wkidx, wwidx,
             jnp.zeros((d, 128 - IDX_HEADS), wl.dtype)], axis=1).astype(BF16)

        qkv = _norm_matmul(h, mix_pre_g[l], w_qkv, BF16, tm, 768)
        rest = _norm_matmul(h, mix_pre_g[l], w_rest, F32, tm, 1024)

        lp = a_lambda[l].astype(F32)
        lam = jnp.exp(jnp.sum(lp[0] * lp[1])) - jnp.exp(jnp.sum(lp[2] * lp[3])) + lam_init
        oa = _diff_attention(qkv.reshape(bsz, s_len, 3 * A_WIDTH), lam.reshape(1), bias_a,
                             a_sub_g[l], 1.0 - lam_init)

        wuq = b_w_uq[l].reshape(B_Q_RANK, B_WIDTH).astype(BF16)
        wuk = jnp.transpose(b_w_uk[l], (1, 2, 0)).astype(BF16)
        wqi = b_w_qidx[l].reshape(B_Q_RANK, IDX_HEADS * IDX_DIM).astype(BF16)
        wuv = jnp.transpose(b_w_uv[l], (1, 0, 2)).astype(BF16)
        qlat, qidx, ckvn, kk, wis = _dsa_prep(rest, b_cq_g[l], b_ckv_g[l], wuq, wuk, wqi, tm)
        r3 = lambda a: a.reshape(bsz, s_len, a.shape[-1])
        ob = _dsa_attention(r3(qlat), r3(qidx), r3(wis), r3(kk), r3(ckvn), bias_b, wuv)

        oc = _gmlp(rest, c_ln_g[l], c_ln_b[l], c_w_s[l], c_b_s[l], tm)

        wo_l = w_out[l].astype(BF16)
        h = _out_proj(h, oa.reshape(tokens, A_WIDTH), ob.reshape(tokens, B_WIDTH), oc,
                      wo_l[:A_WIDTH], wo_l[A_WIDTH:A_WIDTH + B_WIDTH], wo_l[A_WIDTH + B_WIDTH:],
                      mix_post_g[l], tm)

        hw = X_HEADS * X_DIM
        kv = _norm_matmul(mem2, mem_g[l], x_wkv[l].reshape(d, 2 * hw).astype(BF16), BF16,
                          _tile(bsz * mlen, 512), 512)
        h = _xattn(h.reshape(bsz, s_len, d), kv.reshape(bsz, mlen, 2 * hw), x_pre_g[l],
                   x_wq[l].reshape(d, hw).astype(BF16), x_wo[l].reshape(hw, d).astype(BF16),
                   x_post_g[l], _tile(s_len, 512)).reshape(tokens, d)

        fh = f_w_down.shape[1]
        h = _ffn(h, f_pre_g[l], f_w_gu[l].reshape(d, 2 * fh).astype(BF16),
                 f_w_down[l].astype(BF16), f_post_g[l], tm, 512)
    return h.reshape(bsz, s_len, d)
```

```python
import functools
import math

import numpy as np
import jax
import jax.numpy as jnp
from jax import lax
from jax.experimental import pallas as pl
from jax.experimental.pallas import tpu as pltpu

F32 = jnp.float32
BF16 = jnp.bfloat16
EPS = 1e-6
NEG = -1e30
INT_MIN = -(2 ** 31)

HEAD_DIM = 128
A_HEADS = 6
A_HALF = 64
B_HEADS = 6
B_Q_RANK = 512
B_KV_RANK = 256
IDX_HEADS = 16
IDX_DIM = 64
TOPK_MAX = 256
C_GROUPS = 4
C_WIDTH = 512
CHUNK = 128
X_HEADS = 4
X_DIM = 128
REL_BUCKETS = 32
REL_MAX_DIST = 128
A_WIDTH = A_HEADS * HEAD_DIM
B_WIDTH = B_HEADS * HEAD_DIM

ATTN_BLOCK = 256
MIB = 1024 * 1024


def _params(semantics, vmem_mib):
    return pltpu.CompilerParams(dimension_semantics=semantics,
                                vmem_limit_bytes=vmem_mib * MIB)


def _rms(x, g):
    return x * lax.rsqrt(jnp.mean(x * x, axis=-1, keepdims=True) + EPS) * g


def _dot(a, b):
    return jnp.dot(a, b, preferred_element_type=F32)


def _dot_nt(a, b):
    return lax.dot_general(a, b, (((1,), (1,)), ((), ())), preferred_element_type=F32)


def _norm_matmul_kernel(x_ref, g_ref, w_ref, o_ref, xn_ref):
    @pl.when(pl.program_id(1) == 0)
    def _():
        xn_ref[...] = _rms(x_ref[...], g_ref[...]).astype(BF16)

    o_ref[...] = _dot(xn_ref[...], w_ref[...]).astype(o_ref.dtype)


def _norm_matmul(x, g, w, out_dtype, tm, tn):
    m, d = x.shape
    n = w.shape[1]
    return pl.pallas_call(
        _norm_matmul_kernel,
        grid=(m // tm, n // tn),
        in_specs=[pl.BlockSpec((tm, d), lambda i, j: (i, 0)),
                  pl.BlockSpec((1, d), lambda i, j: (0, 0)),
                  pl.BlockSpec((d, tn), lambda i, j: (0, j))],
        out_specs=pl.BlockSpec((tm, tn), lambda i, j: (i, j)),
        out_shape=jax.ShapeDtypeStruct((m, n), out_dtype),
        scratch_shapes=[pltpu.VMEM((tm, d), BF16)],
        compiler_params=_params(("parallel", "arbitrary"), 48),
        name="norm_matmul",
    )(x, g.reshape(1, d), w)


def _softmax_step(s, vt, m, l, acc_ref):
    m_new = jnp.maximum(m, jnp.max(s, axis=0, keepdims=True))
    alpha = jnp.exp(m - m_new)
    p = jnp.exp(s - m_new)
    l_new = alpha * l + jnp.sum(p, axis=0, keepdims=True)
    acc_ref[...] = alpha * acc_ref[...] + _dot(vt, p.astype(BF16))
    return m_new, l_new


def _bucket_np(dist):
    n = np.maximum(dist, 0)
    max_exact = REL_BUCKETS // 2
    nf = np.maximum(n, 1).astype(np.float64)
    large = max_exact + (np.log(nf / max_exact) / math.log(REL_MAX_DIST / max_exact)
                         * (REL_BUCKETS - max_exact)).astype(np.int32)
    large = np.minimum(large, REL_BUCKETS - 1)
    return np.where(n < max_exact, n, large)


def _bias_tiles(tab, t):
    assert t >= REL_MAX_DIST
    length = 2 * t
    bucket = jnp.asarray(_bucket_np(np.arange(length)))
    by_dist = (jnp.take(tab, bucket, axis=0) - tab[REL_BUCKETS - 1]).T

    def skew(v):
        flat = jnp.tile(v, (1, t))[:, :t * (length - 1)]
        return flat.reshape(-1, t, length - 1)[:, :, :t]

    diag = skew(by_dist)
    prev = skew(jnp.roll(by_dist, -t, axis=1))
    key = np.arange(t)[:, None]
    query = np.arange(t)[None, :]
    diag = jnp.where(jnp.asarray(query >= key)[None], diag, NEG)
    return jnp.stack([diag, prev], axis=1).astype(F32)


def _diff_attn_kernel(lam_ref, qt_ref, k_ref, vt_ref, bias_ref, g_ref, o_ref, acc_ref,
                      *, t, out_scale):
    qi = pl.program_id(2)
    qt = qt_ref[0]
    sub = lax.broadcasted_iota(jnp.int32, qt.shape, 0)
    zero = jnp.zeros_like(qt)
    qs = jnp.concatenate([jnp.where(sub < A_HALF, qt, zero),
                          jnp.where(sub >= A_HALF, qt, zero)], axis=1)
    acc_ref[...] = jnp.zeros(acc_ref.shape, F32)

    def block(kb, m, l, bias):
        ks = pl.multiple_of(kb * t, t)
        s = _dot(k_ref[0, pl.ds(ks, t), :], qs)
        if bias is not None:
            s = s + jnp.concatenate([bias, bias], axis=1)
        return _softmax_step(s, vt_ref[0, 0, kb], m, l, acc_ref)

    m = jnp.full((1, 2 * t), NEG, F32)
    l = jnp.zeros((1, 2 * t), F32)
    m, l = lax.fori_loop(0, jnp.maximum(qi - 1, 0), lambda kb, c: block(kb, c[0], c[1], None), (m, l))
    m, l = lax.cond(qi >= 1, lambda: block(qi - 1, m, l, bias_ref[0, 1]), lambda: (m, l))
    m, l = block(qi, m, l, bias_ref[0, 0])

    o = acc_ref[...] * (1.0 / l)
    o = o[:, :t] - lam_ref[0] * o[:, t:]
    y = o * lax.rsqrt(jnp.mean(o * o, axis=0, keepdims=True) + EPS) * g_ref[...] * out_scale
    o_ref[0] = y.T.astype(o_ref.dtype)


def _diff_attention(qkv, lam, bias, sub_g, out_scale):
    b, s, _ = qkv.shape
    t = min(ATTN_BLOCK, s)
    nkb = s // t
    qt = jnp.swapaxes(qkv[:, :, :A_WIDTH], 1, 2)
    vt = jnp.transpose(qkv[:, :, 2 * A_WIDTH:].reshape(b, nkb, t, A_HEADS, HEAD_DIM),
                       (0, 3, 1, 4, 2))
    kern = functools.partial(_diff_attn_kernel, t=t, out_scale=out_scale)
    return pl.pallas_call(
        kern,
        grid=(b, A_HEADS, s // t),
        in_specs=[pl.BlockSpec(memory_space=pltpu.SMEM),
                  pl.BlockSpec((1, HEAD_DIM, t), lambda bi, h, qi: (bi, h, qi)),
                  pl.BlockSpec((1, s, HEAD_DIM), lambda bi, h, qi: (bi, 0, A_HEADS + h)),
                  pl.BlockSpec((1, 1, nkb, HEAD_DIM, t), lambda bi, h, qi: (bi, h, 0, 0, 0)),
                  pl.BlockSpec((1, 2, t, t), lambda bi, h, qi: (h, 0, 0, 0)),
                  pl.BlockSpec((HEAD_DIM, 1), lambda bi, h, qi: (0, 0))],
        out_specs=pl.BlockSpec((1, t, HEAD_DIM), lambda bi, h, qi: (bi, qi, h)),
        out_shape=jax.ShapeDtypeStruct((b, s, A_WIDTH), BF16),
        scratch_shapes=[pltpu.VMEM((HEAD_DIM, 2 * t), F32)],
        compiler_params=_params(("parallel", "parallel", "arbitrary"), 32),
        name="diff_attention",
    )(lam, qt, qkv, vt, bias, sub_g.reshape(HEAD_DIM, 1))


def _dsa_prep_kernel(cq_ref, ckv_ref, kk_ref, wi_ref, gq_ref, gkv_ref, wuq_ref, wuk_ref, wqi_ref,
                     qlat_ref, qidx_ref, ckvn_ref, kkb_ref, wis_ref):
    cqn = _rms(cq_ref[...], gq_ref[...]).astype(BF16)
    ckvn_ref[...] = _rms(ckv_ref[...], gkv_ref[...]).astype(BF16)
    q = _dot(cqn, wuq_ref[...])
    for h in range(B_HEADS):
        qh = q[:, h * HEAD_DIM:(h + 1) * HEAD_DIM].astype(BF16)
        qlat_ref[:, h * B_KV_RANK:(h + 1) * B_KV_RANK] = _dot(qh, wuk_ref[h]).astype(BF16)
    qidx_ref[...] = _dot(cqn, wqi_ref[...]).astype(BF16)
    kkb_ref[...] = kk_ref[...].astype(BF16)
    wis_ref[...] = wi_ref[:, :IDX_HEADS] * (IDX_HEADS ** -0.5 * IDX_DIM ** -0.5)


def _dsa_prep(rest, gq, gkv, wuq, wuk, wqi, tm):
    m = rest.shape[0]
    row = lambda c: (lambda i: (i, c))
    full2 = lambda i: (0, 0)
    return pl.pallas_call(
        _dsa_prep_kernel,
        grid=(m // tm,),
        in_specs=[pl.BlockSpec((tm, B_Q_RANK), row(2)),
                  pl.BlockSpec((tm, B_KV_RANK), row(6)),
                  pl.BlockSpec((tm, 128), row(14)),
                  pl.BlockSpec((tm, 128), row(15)),
                  pl.BlockSpec((1, B_Q_RANK), full2),
                  pl.BlockSpec((1, B_KV_RANK), full2),
                  pl.BlockSpec((B_Q_RANK, B_WIDTH), full2),
                  pl.BlockSpec((B_HEADS, HEAD_DIM, B_KV_RANK), lambda i: (0, 0, 0)),
                  pl.BlockSpec((B_Q_RANK, IDX_HEADS * IDX_DIM), full2)],
        out_specs=[pl.BlockSpec((tm, B_HEADS * B_KV_RANK), row(0)),
                   pl.BlockSpec((tm, IDX_HEADS * IDX_DIM), row(0)),
                   pl.BlockSpec((tm, B_KV_RANK), row(0)),
                   pl.BlockSpec((tm, 128), row(0)),
                   pl.BlockSpec((tm, IDX_HEADS), row(0))],
        out_shape=[jax.ShapeDtypeStruct((m, B_HEADS * B_KV_RANK), BF16),
                   jax.ShapeDtypeStruct((m, IDX_HEADS * IDX_DIM), BF16),
                   jax.ShapeDtypeStruct((m, B_KV_RANK), BF16),
                   jax.ShapeDtypeStruct((m, 128), BF16),
                   jax.ShapeDtypeStruct((m, IDX_HEADS), F32)],
        compiler_params=_params(("parallel",), 32),
        name="dsa_prep",
    )(rest, rest, rest, rest, gq.reshape(1, -1), gkv.reshape(1, -1), wuq, wuk, wqi)


def _dsa_kernel(qlat_ref, qidx_ref, wis_ref, kk_ref, ckvn_ref, ckvnt_ref, bias_ref, wuv_ref, o_ref,
                key_ref, m_ref, l_ref, acc_ref, *, t, topk):
    qi = pl.program_id(1)
    nblk = qi + 1
    key_pos = lax.broadcasted_iota(jnp.int32, (t, t), 0)
    query_pos = lax.broadcasted_iota(jnp.int32, (t, t), 1)

    qidx = qidx_ref[0]
    wis = wis_ref[0]
    sub = lax.broadcasted_iota(jnp.int32, (2 * IDX_DIM, t), 0)
    q_heads = []
    for g in range(IDX_HEADS):
        pair = qidx[(g // 2) * 2 * IDX_DIM:(g // 2 + 1) * 2 * IDX_DIM]
        keep = (sub < IDX_DIM) if g % 2 == 0 else (sub >= IDX_DIM)
        q_heads.append(jnp.where(keep, pair, jnp.zeros_like(pair)))

    def score_block(kb, c):
        ks = pl.multiple_of(kb * t, t)
        kblk = kk_ref[0, pl.ds(ks, t), :]
        score = jnp.zeros((t, t), F32)
        for g in range(IDX_HEADS):
            d = _dot(kblk, q_heads[g])
            score = score + jnp.maximum(d, 0.0) * wis[g:g + 1]
        bits = pltpu.bitcast(score, jnp.int32)
        key = jnp.where(bits < 0, bits ^ jnp.int32(0x7FFFFFFF), bits)
        causal = (kb < qi) | (query_pos >= key_pos)
        key_ref[kb] = jnp.where(causal, key, jnp.int32(INT_MIN))
        return c

    lax.fori_loop(0, nblk, score_block, 0)

    def count_ge(cand):
        cand_b = jnp.broadcast_to(cand, (t, t))

        def body(kb, c):
            x = jnp.where(key_ref[kb] >= cand_b, 1.0, 0.0)
            return c + jnp.sum(x.reshape(t // 8, 8, t), axis=0)

        c = lax.fori_loop(0, nblk, body, jnp.zeros((8, t), F32))
        return jnp.sum(c, axis=0, keepdims=True)

    def bit_step(i, thr):
        cand = thr + jnp.left_shift(jnp.int32(1), 31 - i)
        return jnp.where(count_ge(cand) >= float(topk), cand, thr)

    thr = lax.fori_loop(0, 32, bit_step, jnp.full((1, t), INT_MIN, jnp.int32))
    thr_b = jnp.broadcast_to(jnp.maximum(thr, jnp.int32(INT_MIN + 1)), (t, t))

    m_ref[...] = jnp.full(m_ref.shape, NEG, F32)
    l_ref[...] = jnp.zeros(l_ref.shape, F32)
    acc_ref[...] = jnp.zeros(acc_ref.shape, F32)
    qlat = qlat_ref[0]
    scale = HEAD_DIM ** -0.5

    def attend(kb, near):
        ks = pl.multiple_of(kb * t, t)
        cb = ckvn_ref[0, pl.ds(ks, t), :]
        cbt = ckvnt_ref[0, kb]
        sel = key_ref[kb] >= thr_b
        for h in range(B_HEADS):
            s = _dot(cb, qlat[h * B_KV_RANK:(h + 1) * B_KV_RANK]) * scale
            if near is not None:
                s = s + bias_ref[h, near]
            s = jnp.where(sel, s, NEG)
            m, l = _softmax_step(s, cbt, m_ref[h], l_ref[h], acc_ref.at[h])
            m_ref[h] = m
            l_ref[h] = l

    def far(kb, c):
        attend(kb, None)
        return c

    lax.fori_loop(0, jnp.maximum(qi - 1, 0), far, 0)

    @pl.when(qi >= 1)
    def _():
        attend(qi - 1, 1)

    attend(qi, 0)

    for h in range(B_HEADS):
        o_lat = (acc_ref[h] * (1.0 / l_ref[h])).astype(BF16)
        o = _dot(wuv_ref[h], o_lat)
        o_ref[0, :, h * HEAD_DIM:(h + 1) * HEAD_DIM] = o.T.astype(o_ref.dtype)


def _dsa_attention(qlat, qidx, wis, kk, ckvn, bias, wuv):
    b, s, _ = qlat.shape
    t = min(ATTN_BLOCK, s)
    nkb = s // t
    topk = min(TOPK_MAX, s // 4)
    qlat_t = jnp.swapaxes(qlat, 1, 2)
    qidx_t = jnp.swapaxes(qidx, 1, 2)
    wis_t = jnp.swapaxes(wis, 1, 2)
    ckvn_t = jnp.swapaxes(ckvn.reshape(b, nkb, t, B_KV_RANK), 2, 3)
    kern = functools.partial(_dsa_kernel, t=t, topk=topk)
    qcol = lambda bi, qi: (bi, 0, qi)
    allk = lambda bi, qi: (bi, 0, 0)
    return pl.pallas_call(
        kern,
        grid=(b, s // t),
        in_specs=[pl.BlockSpec((1, B_HEADS * B_KV_RANK, t), qcol),
                  pl.BlockSpec((1, IDX_HEADS * IDX_DIM, t), qcol),
                  pl.BlockSpec((1, IDX_HEADS, t), qcol),
                  pl.BlockSpec((1, s, 128), allk),
                  pl.BlockSpec((1, s, B_KV_RANK), allk),
                  pl.BlockSpec((1, nkb, B_KV_RANK, t), lambda bi, qi: (bi, 0, 0, 0)),
                  pl.BlockSpec((B_HEADS, 2, t, t), lambda bi, qi: (0, 0, 0, 0)),
                  pl.BlockSpec((B_HEADS, HEAD_DIM, B_KV_RANK), lambda bi, qi: (0, 0, 0))],
        out_specs=pl.BlockSpec((1, t, B_WIDTH), lambda bi, qi: (bi, qi, 0)),
        out_shape=jax.ShapeDtypeStruct((b, s, B_WIDTH), BF16),
        scratch_shapes=[pltpu.VMEM((nkb, t, t), jnp.int32),
                        pltpu.VMEM((B_HEADS, 1, t), F32), pltpu.VMEM((B_HEADS, 1, t), F32),
                        pltpu.VMEM((B_HEADS, B_KV_RANK, t), F32)],
        compiler_params=_params(("parallel", "arbitrary"), 48),
        name="dsa_attention",
    )(qlat_t, qidx_t, wis_t, kk, ckvn, ckvn_t, bias, wuv)


def _gelu(x):
    return x * (0.5 * (1.0 + jnp.tanh(math.sqrt(2.0 / math.pi) * (x + 0.044715 * (x * x * x)))))


def _gmlp_kernel(cu_ref, cv_ref, g_ref, b_ref, ws_ref, bs_ref, o_ref, *, ts):
    u = _gelu(cu_ref[...])
    v = _gelu(cv_ref[...])
    mu = jnp.mean(v, axis=-1, keepdims=True)
    vc = v - mu
    var = jnp.mean(vc * vc, axis=-1, keepdims=True)
    vn = (vc * lax.rsqrt(var + EPS) * g_ref[...] + b_ref[...]).astype(BF16)
    row = lax.broadcasted_iota(jnp.int32, (CHUNK, CHUNK), 0)
    col = lax.broadcasted_iota(jnp.int32, (CHUNK, CHUNK), 1)
    for g in range(C_GROUPS):
        w = jnp.where(row >= col, ws_ref[g], 0.0).astype(BF16)
        bcol = bs_ref[:, g:g + 1]
        cs = slice(g * 128, (g + 1) * 128)
        for c in range(ts // CHUNK):
            rs = slice(c * CHUNK, (c + 1) * CHUNK)
            y = _dot(w, vn[rs, cs]) + bcol
            o_ref[rs, cs] = (u[rs, cs] * y).astype(o_ref.dtype)


def _gmlp(rest, ln_g, ln_b, w_s, b_s, ts):
    m = rest.shape[0]
    kern = functools.partial(_gmlp_kernel, ts=ts)
    return pl.pallas_call(
        kern,
        grid=(m // ts,),
        in_specs=[pl.BlockSpec((ts, C_WIDTH), lambda i: (i, 0)),
                  pl.BlockSpec((ts, C_WIDTH), lambda i: (i, 1)),
                  pl.BlockSpec((1, C_WIDTH), lambda i: (0, 0)),
                  pl.BlockSpec((1, C_WIDTH), lambda i: (0, 0)),
                  pl.BlockSpec((C_GROUPS, CHUNK, CHUNK), lambda i: (0, 0, 0)),
                  pl.BlockSpec((CHUNK, C_GROUPS), lambda i: (0, 0))],
        out_specs=pl.BlockSpec((ts, C_WIDTH), lambda i: (i, 0)),
        out_shape=jax.ShapeDtypeStruct((m, C_WIDTH), BF16),
        compiler_params=_params(("parallel",), 32),
        name="gmlp",
    )(rest, rest, ln_g.reshape(1, -1), ln_b.reshape(1, -1), w_s, jnp.transpose(b_s))


def _out_proj_kernel(h_ref, oa_ref, ob_ref, oc_ref, wa_ref, wb_ref, wc_ref, g_ref, o_ref):
    y = _dot(oa_ref[...], wa_ref[...]) + _dot(ob_ref[...], wb_ref[...]) + _dot(oc_ref[...], wc_ref[...])
    o_ref[...] = h_ref[...] + _rms(y, g_ref[...])


def _out_proj(h, oa, ob, oc, wa, wb, wc, g, tm):
    m, d = h.shape
    rowblk = lambda i: (i, 0)
    full = lambda i: (0, 0)
    return pl.pallas_call(
        _out_proj_kernel,
        grid=(m // tm,),
        in_specs=[pl.BlockSpec((tm, d), rowblk),
                  pl.BlockSpec((tm, A_WIDTH), rowblk),
                  pl.BlockSpec((tm, B_WIDTH), rowblk),
                  pl.BlockSpec((tm, C_WIDTH), rowblk),
                  pl.BlockSpec((A_WIDTH, d), full),
                  pl.BlockSpec((B_WIDTH, d), full),
                  pl.BlockSpec((C_WIDTH, d), full),
                  pl.BlockSpec((1, d), full)],
        out_specs=pl.BlockSpec((tm, d), rowblk),
        out_shape=jax.ShapeDtypeStruct((m, d), F32),
        compiler_params=_params(("parallel",), 48),
        name="out_proj",
    )(h, oa, ob, oc, wa, wb, wc, g.reshape(1, d))


def _xattn_kernel(h_ref, gpre_ref, wq_ref, k_ref, v_ref, wo_ref, gpost_ref, o_ref):
    x = h_ref[0]
    xn = _rms(x, gpre_ref[...]).astype(BF16)
    q = _dot(xn, wq_ref[...])
    scale = X_DIM ** -0.5
    outs = []
    for hh in range(X_HEADS):
        cs = slice(hh * X_DIM, (hh + 1) * X_DIM)
        s = _dot_nt(q[:, cs].astype(BF16), k_ref[0, :, cs]) * scale
        p = jnp.exp(s - jnp.max(s, axis=-1, keepdims=True))
        p = p * (1.0 / jnp.sum(p, axis=-1, keepdims=True))
        outs.append(_dot(p.astype(BF16), v_ref[0, :, cs]).astype(BF16))
    o = jnp.concatenate(outs, axis=1)
    xa = _dot(o, wo_ref[...])
    o_ref[0] = x + _rms(xa, gpost_ref[...])


def _xattn(h, kv, gpre, wq, wo, gpost, tm):
    b, s, d = h.shape
    mlen = kv.shape[1]
    hw = X_HEADS * X_DIM
    return pl.pallas_call(
        _xattn_kernel,
        grid=(b, s // tm),
        in_specs=[pl.BlockSpec((1, tm, d), lambda bi, i: (bi, i, 0)),
                  pl.BlockSpec((1, d), lambda bi, i: (0, 0)),
                  pl.BlockSpec((d, hw), lambda bi, i: (0, 0)),
                  pl.BlockSpec((1, mlen, hw), lambda bi, i: (bi, 0, 0)),
                  pl.BlockSpec((1, mlen, hw), lambda bi, i: (bi, 0, 1)),
                  pl.BlockSpec((hw, d), lambda bi, i: (0, 0)),
                  pl.BlockSpec((1, d), lambda bi, i: (0, 0))],
        out_specs=pl.BlockSpec((1, tm, d), lambda bi, i: (bi, i, 0)),
        out_shape=jax.ShapeDtypeStruct((b, s, d), F32),
        compiler_params=_params(("parallel", "parallel"), 48),
        name="xattn",
    )(h, gpre.reshape(1, d), wq, kv, kv, wo, gpost.reshape(1, d))


def _ffn_kernel(x_ref, gpre_ref, wg_ref, wu_ref, wd_ref, gpost_ref, o_ref, xn_ref, acc_ref):
    j = pl.program_id(1)

    @pl.when(j == 0)
    def _():
        xn_ref[...] = _rms(x_ref[...], gpre_ref[...]).astype(BF16)
        acc_ref[...] = jnp.zeros(acc_ref.shape, F32)

    xn = xn_ref[...]
    gate = _dot(xn, wg_ref[...])
    up = _dot(xn, wu_ref[...])
    act = gate * (1.0 / (1.0 + jnp.exp(-gate))) * up
    acc_ref[...] += _dot(act.astype(BF16), wd_ref[...])

    @pl.when(j == pl.num_programs(1) - 1)
    def _():
        o_ref[...] = x_ref[...] + _rms(acc_ref[...], gpost_ref[...])


def _ffn(h, gpre, w_gu, w_down, gpost, tm, tf):
    m, d = h.shape
    f = w_down.shape[0]
    nf = f // tf
    return pl.pallas_call(
        _ffn_kernel,
        grid=(m // tm, nf),
        in_specs=[pl.BlockSpec((tm, d), lambda i, j: (i, 0)),
                  pl.BlockSpec((1, d), lambda i, j: (0, 0)),
                  pl.BlockSpec((d, tf), lambda i, j: (0, j)),
                  pl.BlockSpec((d, tf), lambda i, j: (0, nf + j)),
                  pl.BlockSpec((tf, d), lambda i, j: (j, 0)),
                  pl.BlockSpec((1, d), lambda i, j: (0, 0))],
        out_specs=pl.BlockSpec((tm, d), lambda i, j: (i, 0)),
        out_shape=jax.ShapeDtypeStruct((m, d), F32),
        scratch_shapes=[pltpu.VMEM((tm, d), BF16), pltpu.VMEM((tm, d), F32)],
        compiler_params=_params(("parallel", "arbitrary"), 56),
        name="ffn",
    )(h, gpre.reshape(1, d), w_gu, w_gu, w_down, gpost.reshape(1, d))


def _tile(n, pref):
    t = min(pref, n)
    assert n % t == 0
    return t


def kernel(x, mem, rel_bias, mix_pre_g, mix_post_g, w_in, w_out, a_lambda, a_sub_g, b_cq_g, b_ckv_g, b_w_uq, b_w_qidx, b_w_uk, b_w_uv, c_ln_g, c_ln_b, c_w_s, c_b_s, x_pre_g, x_post_g, mem_g, x_wq, x_wkv, x_wo, f_pre_g, f_post_g, f_w_gu, f_w_down):
    bsz, s_len, d = x.shape
    depth = w_in.shape[0]
    tokens = bsz * s_len
    mlen = mem.shape[1]
    t_attn = min(ATTN_BLOCK, s_len)
    bias_a = _bias_tiles(rel_bias[:, :A_HEADS], t_attn)
    bias_b = _bias_tiles(rel_bias[:, A_HEADS:], t_attn)
    sizes = (A_WIDTH, A_WIDTH, A_WIDTH, B_Q_RANK, B_KV_RANK, IDX_DIM, IDX_HEADS, C_WIDTH, C_WIDTH)
    offs = np.concatenate([[0], np.cumsum(sizes)])
    tm = _tile(tokens, 512)

    h = x.reshape(tokens, d)
    mem2 = mem.reshape(bsz * mlen, d)
    for l in range(depth):
        lam_init = 0.8 - 0.6 * math.exp(-0.3 * l)
        wl = w_in[l]
        cols = [wl[:, offs[i]:offs[i + 1]] for i in range(len(sizes))]
        wqa, wka, wva, wcq, wckv, wkidx, wwidx, wcu, wcv = cols
        w_qkv = jnp.concatenate([wqa * (A_HALF ** -0.5), wka, wva], axis=1).astype(BF16)
        w_rest = jnp.concatenate(
            [wcu, wcv, wcq, wckv, wkidx, wkidx, wwidx,
             jnp.zeros((d, 128 - IDX_HEADS), wl.dtype)], axis=1).astype(BF16)

        qkv = _norm_matmul(h, mix_pre_g[l], w_qkv, BF16, tm, 768)
        rest = _norm_matmul(h, mix_pre_g[l], w_rest, F32, tm, 1024)

        lp = a_lambda[l].astype(F32)
        lam = jnp.exp(jnp.sum(lp[0] * lp[1])) - jnp.exp(jnp.sum(lp[2] * lp[3])) + lam_init
        oa = _diff_attention(qkv.reshape(bsz, s_len, 3 * A_WIDTH), lam.reshape(1), bias_a,
                             a_sub_g[l], 1.0 - lam_init)

        wuq = b_w_uq[l].reshape(B_Q_RANK, B_WIDTH).astype(BF16)
        wuk = jnp.transpose(b_w_uk[l], (1, 2, 0)).astype(BF16)
        wqi = b_w_qidx[l].reshape(B_Q_RANK, IDX_HEADS * IDX_DIM).astype(BF16)
        wuv = jnp.transpose(b_w_uv[l], (1, 2, 0)).astype(BF16)
        qlat, qidx, ckvn, kk, wis = _dsa_prep(rest, b_cq_g[l], b_ckv_g[l], wuq, wuk, wqi, tm)
        r3 = lambda a: a.reshape(bsz, s_len, a.shape[-1])
        ob = _dsa_attention(r3(qlat), r3(qidx), r3(wis), r3(kk), r3(ckvn), bias_b, wuv)

        oc = _gmlp(rest, c_ln_g[l], c_ln_b[l], c_w_s[l], c_b_s[l], tm)

        wo_l = w_out[l].astype(BF16)
        h = _out_proj(h, oa.reshape(tokens, A_WIDTH), ob.reshape(tokens, B_WIDTH), oc,
                      wo_l[:A_WIDTH], wo_l[A_WIDTH:A_WIDTH + B_WIDTH], wo_l[A_WIDTH + B_WIDTH:],
                      mix_post_g[l], tm)

        hw = X_HEADS * X_DIM
        kv = _norm_matmul(mem2, mem_g[l], x_wkv[l].reshape(d, 2 * hw).astype(BF16), BF16,
                          _tile(bsz * mlen, 512), 512)
        h = _xattn(h.reshape(bsz, s_len, d), kv.reshape(bsz, mlen, 2 * hw), x_pre_g[l],
                   x_wq[l].reshape(d, hw).astype(BF16), x_wo[l].reshape(hw, d).astype(BF16),
                   x_post_g[l], _tile(s_len, 512)).reshape(tokens, d)

        fh = f_w_down.shape[1]
        h = _ffn(h, f_pre_g[l], f_w_gu[l].reshape(d, 2 * fh).astype(BF16),
                 f_w_down[l].astype(BF16), f_post_g[l], tm, 512)
    return h.reshape(bsz, s_len, d)
```

```python
import functools
import math

import numpy as np
import jax
import jax.numpy as jnp
from jax import lax
from jax.experimental import pallas as pl
from jax.experimental.pallas import tpu as pltpu

F32 = jnp.float32
BF16 = jnp.bfloat16
EPS = 1e-6
NEG = -1e30
INT_MIN = -(2 ** 31)

HEAD_DIM = 128
A_HEADS = 6
A_HALF = 64
B_HEADS = 6
B_Q_RANK = 512
B_KV_RANK = 256
IDX_HEADS = 16
IDX_DIM = 64
TOPK_MAX = 256
C_GROUPS = 4
C_WIDTH = 512
CHUNK = 128
X_HEADS = 4
X_DIM = 128
REL_BUCKETS = 32
REL_MAX_DIST = 128
A_WIDTH = A_HEADS * HEAD_DIM
B_WIDTH = B_HEADS * HEAD_DIM

ATTN_BLOCK = 256
MIB = 1024 * 1024


def _params(semantics, vmem_mib):
    return pltpu.CompilerParams(dimension_semantics=semantics,
                                vmem_limit_bytes=vmem_mib * MIB)


def _rms(x, g):
    return x * lax.rsqrt(jnp.mean(x * x, axis=-1, keepdims=True) + EPS) * g


def _dot(a, b):
    return jnp.dot(a, b, preferred_element_type=F32)


def _dot_nt(a, b):
    return lax.dot_general(a, b, (((1,), (1,)), ((), ())), preferred_element_type=F32)


def _norm_matmul_kernel(x_ref, g_ref, w_ref, o_ref, xn_ref):
    @pl.when(pl.program_id(1) == 0)
    def _():
        xn_ref[...] = _rms(x_ref[...], g_ref[...]).astype(BF16)

    o_ref[...] = _dot(xn_ref[...], w_ref[...]).astype(o_ref.dtype)


def _norm_matmul(x, g, w, out_dtype, tm, tn):
    m, d = x.shape
    n = w.shape[1]
    return pl.pallas_call(
        _norm_matmul_kernel,
        grid=(m // tm, n // tn),
        in_specs=[pl.BlockSpec((tm, d), lambda i, j: (i, 0)),
                  pl.BlockSpec((1, d), lambda i, j: (0, 0)),
                  pl.BlockSpec((d, tn), lambda i, j: (0, j))],
        out_specs=pl.BlockSpec((tm, tn), lambda i, j: (i, j)),
        out_shape=jax.ShapeDtypeStruct((m, n), out_dtype),
        scratch_shapes=[pltpu.VMEM((tm, d), BF16)],
        compiler_params=_params(("parallel", "arbitrary"), 48),
        name="norm_matmul",
    )(x, g.reshape(1, d), w)


SUM_ROWS = 16
LOG2E = math.log2(math.e)


def _softmax_probs(s, m):
    m_new = jnp.maximum(m, jnp.max(s, axis=0, keepdims=True))
    return m_new, jnp.exp2(m - m_new), jnp.exp2(s - m_new).astype(BF16)


def _accumulate(acc_ref, alpha, vt, p):
    acc_ref[...] = alpha * acc_ref[...] + _dot(vt, p)


def _with_sum_rows(vt):
    lead = vt.shape[:-2]
    tk = vt.shape[-1]
    extra = jnp.concatenate([jnp.ones(lead + (1, tk), vt.dtype),
                             jnp.zeros(lead + (SUM_ROWS - 1, tk), vt.dtype)], axis=-2)
    return jnp.concatenate([vt, extra], axis=-2)


def _bucket_np(dist):
    n = np.maximum(dist, 0)
    max_exact = REL_BUCKETS // 2
    nf = np.maximum(n, 1).astype(np.float64)
    large = max_exact + (np.log(nf / max_exact) / math.log(REL_MAX_DIST / max_exact)
                         * (REL_BUCKETS - max_exact)).astype(np.int32)
    large = np.minimum(large, REL_BUCKETS - 1)
    return np.where(n < max_exact, n, large)


NO_BLOCK, FAR_BLOCK, PREV_BLOCK, DIAG_BLOCK = 0, 1, 2, 3


def _bias_tiles(tab, t):
    assert t >= REL_MAX_DIST
    length = 2 * t
    bucket = jnp.asarray(_bucket_np(np.arange(length)))
    by_dist = ((jnp.take(tab, bucket, axis=0) - tab[REL_BUCKETS - 1]) * LOG2E).T

    def skew(v):
        flat = jnp.tile(v, (1, t))[:, :t * (length - 1)]
        return flat.reshape(-1, t, length - 1)[:, :, :t]

    diag = skew(by_dist)
    prev = skew(jnp.roll(by_dist, -t, axis=1))
    key = np.arange(t)[:, None]
    query = np.arange(t)[None, :]
    diag = jnp.where(jnp.asarray(query >= key)[None], diag, NEG)
    return jnp.stack([jnp.full_like(prev, NEG), jnp.zeros_like(prev), prev, diag], axis=1).astype(F32)


def _diff_attn_kernel(lam_ref, qt_ref, k_ref, vt_ref, bias_ref, g_ref, o_ref, acc_ref, s0_ref, s1_ref,
                      *, t, out_scale):
    qi = pl.program_id(2)
    qt = qt_ref[0]
    sub = lax.broadcasted_iota(jnp.int32, qt.shape, 0)
    zero = jnp.zeros_like(qt)
    qs = jnp.concatenate([jnp.where(sub < A_HALF, qt, zero),
                          jnp.where(sub >= A_HALF, qt, zero)], axis=1)
    acc_ref[...] = jnp.zeros(acc_ref.shape, F32)

    n_pairs = (qi + 2) // 2
    pad = 2 * n_pairs - (qi + 1)

    def block_of(j):
        return jnp.clip(j - pad, 0, qi)

    def kind_of(j):
        kb = j - pad
        return jnp.where(kb < 0, NO_BLOCK,
                         jnp.where(kb == qi, DIAG_BLOCK, jnp.where(kb == qi - 1, PREV_BLOCK, FAR_BLOCK)))

    def scores_into(ref, kb):
        ks = pl.multiple_of(kb * t, t)
        ref[...] = _dot(k_ref[0, pl.ds(ks, t), :], qs)

    def step(ref, j, m):
        bias = bias_ref[0, kind_of(j)]
        m, alpha, p = _softmax_probs(ref[...] + jnp.concatenate([bias, bias], axis=1), m)
        _accumulate(acc_ref, alpha, vt_ref[0, 0, block_of(j)], p)
        return m

    def pair(i, m):
        j = 2 * i
        scores_into(s1_ref, block_of(j + 1))
        m = step(s0_ref, j, m)
        scores_into(s0_ref, block_of(j + 2))
        return step(s1_ref, j + 1, m)

    scores_into(s0_ref, block_of(0))
    lax.fori_loop(0, n_pairs, pair, jnp.full((1, 2 * t), NEG, F32))

    acc = acc_ref[...]
    o = acc[:HEAD_DIM] * (1.0 / acc[HEAD_DIM:HEAD_DIM + 1])
    o = o[:, :t] - lam_ref[0] * o[:, t:]
    y = o * lax.rsqrt(jnp.mean(o * o, axis=0, keepdims=True) + EPS) * g_ref[...] * out_scale
    o_ref[0] = y.T.astype(o_ref.dtype)


def _diff_attention(qkv, lam, bias, sub_g, out_scale):
    b, s, _ = qkv.shape
    t = min(ATTN_BLOCK, s)
    nkb = s // t
    qt = jnp.swapaxes(qkv[:, :, :A_WIDTH], 1, 2)
    vt = jnp.transpose(qkv[:, :, 2 * A_WIDTH:].reshape(b, nkb, t, A_HEADS, HEAD_DIM),
                       (0, 3, 1, 4, 2))
    vt = _with_sum_rows(vt)
    kern = functools.partial(_diff_attn_kernel, t=t, out_scale=out_scale)
    return pl.pallas_call(
        kern,
        grid=(b, A_HEADS, s // t),
        in_specs=[pl.BlockSpec(memory_space=pltpu.SMEM),
                  pl.BlockSpec((1, HEAD_DIM, t), lambda bi, h, qi: (bi, h, qi)),
                  pl.BlockSpec((1, s, HEAD_DIM), lambda bi, h, qi: (bi, 0, A_HEADS + h)),
                  pl.BlockSpec((1, 1, nkb, HEAD_DIM + SUM_ROWS, t), lambda bi, h, qi: (bi, h, 0, 0, 0)),
                  pl.BlockSpec((1, 4, t, t), lambda bi, h, qi: (h, 0, 0, 0)),
                  pl.BlockSpec((HEAD_DIM, 1), lambda bi, h, qi: (0, 0))],
        out_specs=pl.BlockSpec((1, t, HEAD_DIM), lambda bi, h, qi: (bi, qi, h)),
        out_shape=jax.ShapeDtypeStruct((b, s, A_WIDTH), BF16),
        scratch_shapes=[pltpu.VMEM((HEAD_DIM + SUM_ROWS, 2 * t), F32),
                        pltpu.VMEM((t, 2 * t), F32), pltpu.VMEM((t, 2 * t), F32)],
        compiler_params=_params(("parallel", "parallel", "arbitrary"), 32),
        name="diff_attention",
    )(lam, qt, qkv, vt, bias, sub_g.reshape(HEAD_DIM, 1))


def _dsa_prep_kernel(cq_ref, ckv_ref, kk_ref, wi_ref, gq_ref, gkv_ref, wuq_ref, wuk_ref, wqi_ref,
                     qlat_ref, qidx_ref, ckvn_ref, kkb_ref, wis_ref):
    cqn = _rms(cq_ref[...], gq_ref[...]).astype(BF16)
    ckvn_ref[...] = _rms(ckv_ref[...], gkv_ref[...]).astype(BF16)
    q = _dot(cqn, wuq_ref[...])
    for h in range(B_HEADS):
        qh = q[:, h * HEAD_DIM:(h + 1) * HEAD_DIM].astype(BF16)
        qlat = _dot(qh, wuk_ref[h]) * (HEAD_DIM ** -0.5 * LOG2E)
        qlat_ref[:, h * B_KV_RANK:(h + 1) * B_KV_RANK] = qlat.astype(BF16)
    qidx_ref[...] = _dot(cqn, wqi_ref[...]).astype(BF16)
    kkb_ref[...] = kk_ref[...].astype(BF16)
    wis_ref[...] = wi_ref[:, :IDX_HEADS] * (IDX_HEADS ** -0.5 * IDX_DIM ** -0.5)


def _dsa_prep(rest, gq, gkv, wuq, wuk, wqi, tm):
    m = rest.shape[0]
    row = lambda c: (lambda i: (i, c))
    full2 = lambda i: (0, 0)
    return pl.pallas_call(
        _dsa_prep_kernel,
        grid=(m // tm,),
        in_specs=[pl.BlockSpec((tm, B_Q_RANK), row(2)),
                  pl.BlockSpec((tm, B_KV_RANK), row(6)),
                  pl.BlockSpec((tm, 128), row(14)),
                  pl.BlockSpec((tm, 128), row(15)),
                  pl.BlockSpec((1, B_Q_RANK), full2),
                  pl.BlockSpec((1, B_KV_RANK), full2),
                  pl.BlockSpec((B_Q_RANK, B_WIDTH), full2),
                  pl.BlockSpec((B_HEADS, HEAD_DIM, B_KV_RANK), lambda i: (0, 0, 0)),
                  pl.BlockSpec((B_Q_RANK, IDX_HEADS * IDX_DIM), full2)],
        out_specs=[pl.BlockSpec((tm, B_HEADS * B_KV_RANK), row(0)),
                   pl.BlockSpec((tm, IDX_HEADS * IDX_DIM), row(0)),
                   pl.BlockSpec((tm, B_KV_RANK), row(0)),
                   pl.BlockSpec((tm, 128), row(0)),
                   pl.BlockSpec((tm, IDX_HEADS), row(0))],
        out_shape=[jax.ShapeDtypeStruct((m, B_HEADS * B_KV_RANK), BF16),
                   jax.ShapeDtypeStruct((m, IDX_HEADS * IDX_DIM), BF16),
                   jax.ShapeDtypeStruct((m, B_KV_RANK), BF16),
                   jax.ShapeDtypeStruct((m, 128), BF16),
                   jax.ShapeDtypeStruct((m, IDX_HEADS), F32)],
        compiler_params=_params(("parallel",), 32),
        name="dsa_prep",
    )(rest, rest, rest, rest, gq.reshape(1, -1), gkv.reshape(1, -1), wuq, wuk, wqi)


def _dsa_kernel(qlat_ref, qidx_ref, wis_ref, kk_ref, ckvn_ref, ckvnt_ref, bias_ref, wuv_ref, o_ref,
                key_ref, m_ref, acc_ref, *, t, topk):
    qi = pl.program_id(1)
    nblk = qi + 1
    key_pos = lax.broadcasted_iota(jnp.int32, (t, t), 0)
    query_pos = lax.broadcasted_iota(jnp.int32, (t, t), 1)

    qidx = qidx_ref[0]
    wis = wis_ref[0]
    sub = lax.broadcasted_iota(jnp.int32, (2 * IDX_DIM, t), 0)
    q_heads = []
    for g in range(IDX_HEADS):
        pair = qidx[(g // 2) * 2 * IDX_DIM:(g // 2 + 1) * 2 * IDX_DIM]
        keep = (sub < IDX_DIM) if g % 2 == 0 else (sub >= IDX_DIM)
        q_heads.append(jnp.where(keep, pair, jnp.zeros_like(pair)))

    def score_block(kb, c):
        ks = pl.multiple_of(kb * t, t)
        kblk = kk_ref[0, pl.ds(ks, t), :]
        score = jnp.zeros((t, t), F32)
        for g in range(IDX_HEADS):
            d = _dot(kblk, q_heads[g])
            score = score + jnp.maximum(d, 0.0) * wis[g:g + 1]
        bits = pltpu.bitcast(score, jnp.int32)
        key = jnp.where(bits < 0, bits ^ jnp.int32(0x7FFFFFFF), bits)
        causal = (kb < qi) | (query_pos >= key_pos)
        key_ref[kb] = jnp.where(causal, key, jnp.int32(INT_MIN))
        return c

    lax.fori_loop(0, nblk, score_block, 0)

    def count_ge(cand):
        cand_b = jnp.broadcast_to(cand, (t, t))

        def body(kb, c):
            x = jnp.where(key_ref[kb] >= cand_b, 1.0, 0.0)
            return c + jnp.sum(x.reshape(t // 8, 8, t), axis=0)

        c = lax.fori_loop(0, nblk, body, jnp.zeros((8, t), F32))
        return jnp.sum(c, axis=0, keepdims=True)

    def bit_step(i, thr):
        cand = thr + jnp.left_shift(jnp.int32(1), 31 - i)
        return jnp.where(count_ge(cand) >= float(topk), cand, thr)

    thr = lax.fori_loop(0, 32, bit_step, jnp.full((1, t), INT_MIN, jnp.int32))
    thr_b = jnp.broadcast_to(jnp.maximum(thr, jnp.int32(INT_MIN + 1)), (t, t))

    m_ref[...] = jnp.full(m_ref.shape, NEG, F32)
    acc_ref[...] = jnp.zeros(acc_ref.shape, F32)
    qlat = qlat_ref[0]

    def attend(kb, near):
        ks = pl.multiple_of(kb * t, t)
        cb = ckvn_ref[0, pl.ds(ks, t), :]
        cbt = ckvnt_ref[0, kb]
        mask = jnp.where(key_ref[kb] >= thr_b, 0.0, NEG)
        for h in range(B_HEADS):
            s = _dot(cb, qlat[h * B_KV_RANK:(h + 1) * B_KV_RANK]) + mask
            if near is not None:
                s = s + bias_ref[h, near]
            m, alpha, p = _softmax_probs(s, m_ref[h])
            _accumulate(acc_ref.at[h], alpha, cbt, p)
            m_ref[h] = m

    def far(kb, c):
        attend(kb, None)
        return c

    lax.fori_loop(0, jnp.maximum(qi - 1, 0), far, 0)

    @pl.when(qi >= 1)
    def _():
        attend(qi - 1, PREV_BLOCK)

    attend(qi, DIAG_BLOCK)

    for h in range(B_HEADS):
        acc = acc_ref[h]
        o_lat = (acc[:B_KV_RANK] * (1.0 / acc[B_KV_RANK:B_KV_RANK + 1])).astype(BF16)
        o = _dot(wuv_ref[h], o_lat)
        o_ref[0, :, h * HEAD_DIM:(h + 1) * HEAD_DIM] = o.T.astype(o_ref.dtype)


def _dsa_attention(qlat, qidx, wis, kk, ckvn, bias, wuv):
    b, s, _ = qlat.shape
    t = min(ATTN_BLOCK, s)
    nkb = s // t
    topk = min(TOPK_MAX, s // 4)
    qlat_t = jnp.swapaxes(qlat, 1, 2)
    qidx_t = jnp.swapaxes(qidx, 1, 2)
    wis_t = jnp.swapaxes(wis, 1, 2)
    ckvn_t = _with_sum_rows(jnp.swapaxes(ckvn.reshape(b, nkb, t, B_KV_RANK), 2, 3))
    kern = functools.partial(_dsa_kernel, t=t, topk=topk)
    qcol = lambda bi, qi: (bi, 0, qi)
    allk = lambda bi, qi: (bi, 0, 0)
    return pl.pallas_call(
        kern,
        grid=(b, s // t),
        in_specs=[pl.BlockSpec((1, B_HEADS * B_KV_RANK, t), qcol),
                  pl.BlockSpec((1, IDX_HEADS * IDX_DIM, t), qcol),
                  pl.BlockSpec((1, IDX_HEADS, t), qcol),
                  pl.BlockSpec((1, s, 128), allk),
                  pl.BlockSpec((1, s, B_KV_RANK), allk),
                  pl.BlockSpec((1, nkb, B_KV_RANK + SUM_ROWS, t), lambda bi, qi: (bi, 0, 0, 0)),
                  pl.BlockSpec((B_HEADS, 4, t, t), lambda bi, qi: (0, 0, 0, 0)),
                  pl.BlockSpec((B_HEADS, HEAD_DIM, B_KV_RANK), lambda bi, qi: (0, 0, 0))],
        out_specs=pl.BlockSpec((1, t, B_WIDTH), lambda bi, qi: (bi, qi, 0)),
        out_shape=jax.ShapeDtypeStruct((b, s, B_WIDTH), BF16),
        scratch_shapes=[pltpu.VMEM((nkb, t, t), jnp.int32),
                        pltpu.VMEM((B_HEADS, 1, t), F32),
                        pltpu.VMEM((B_HEADS, B_KV_RANK + SUM_ROWS, t), F32)],
        compiler_params=_params(("parallel", "arbitrary"), 48),
        name="dsa_attention",
    )(qlat_t, qidx_t, wis_t, kk, ckvn, ckvn_t, bias, wuv)


def _gelu(x):
    return x * (0.5 * (1.0 + jnp.tanh(math.sqrt(2.0 / math.pi) * (x + 0.044715 * (x * x * x)))))


def _gmlp_kernel(cu_ref, cv_ref, g_ref, b_ref, ws_ref, bs_ref, o_ref, *, ts):
    u = _gelu(cu_ref[...])
    v = _gelu(cv_ref[...])
    mu = jnp.mean(v, axis=-1, keepdims=True)
    vc = v - mu
    var = jnp.mean(vc * vc, axis=-1, keepdims=True)
    vn = (vc * lax.rsqrt(var + EPS) * g_ref[...] + b_ref[...]).astype(BF16)
    row = lax.broadcasted_iota(jnp.int32, (CHUNK, CHUNK), 0)
    col = lax.broadcasted_iota(jnp.int32, (CHUNK, CHUNK), 1)
    for g in range(C_GROUPS):
        w = jnp.where(row >= col, ws_ref[g], 0.0).astype(BF16)
        bcol = bs_ref[:, g:g + 1]
        cs = slice(g * 128, (g + 1) * 128)
        for c in range(ts // CHUNK):
            rs = slice(c * CHUNK, (c + 1) * CHUNK)
            y = _dot(w, vn[rs, cs]) + bcol
            o_ref[rs, cs] = (u[rs, cs] * y).astype(o_ref.dtype)


def _gmlp(rest, ln_g, ln_b, w_s, b_s, ts):
    m = rest.shape[0]
    kern = functools.partial(_gmlp_kernel, ts=ts)
    return pl.pallas_call(
        kern,
        grid=(m // ts,),
        in_specs=[pl.BlockSpec((ts, C_WIDTH), lambda i: (i, 0)),
                  pl.BlockSpec((ts, C_WIDTH), lambda i: (i, 1)),
                  pl.BlockSpec((1, C_WIDTH), lambda i: (0, 0)),
                  pl.BlockSpec((1, C_WIDTH), lambda i: (0, 0)),
                  pl.BlockSpec((C_GROUPS, CHUNK, CHUNK), lambda i: (0, 0, 0)),
                  pl.BlockSpec((CHUNK, C_GROUPS), lambda i: (0, 0))],
        out_specs=pl.BlockSpec((ts, C_WIDTH), lambda i: (i, 0)),
        out_shape=jax.ShapeDtypeStruct((m, C_WIDTH), BF16),
        compiler_params=_params(("parallel",), 32),
        name="gmlp",
    )(rest, rest, ln_g.reshape(1, -1), ln_b.reshape(1, -1), w_s, jnp.transpose(b_s))


def _out_proj_kernel(h_ref, oa_ref, ob_ref, oc_ref, wa_ref, wb_ref, wc_ref, g_ref, o_ref):
    y = _dot(oa_ref[...], wa_ref[...]) + _dot(ob_ref[...], wb_ref[...]) + _dot(oc_ref[...], wc_ref[...])
    o_ref[...] = h_ref[...] + _rms(y, g_ref[...])


def _out_proj(h, oa, ob, oc, wa, wb, wc, g, tm):
    m, d = h.shape
    rowblk = lambda i: (i, 0)
    full = lambda i: (0, 0)
    return pl.pallas_call(
        _out_proj_kernel,
        grid=(m // tm,),
        in_specs=[pl.BlockSpec((tm, d), rowblk),
                  pl.BlockSpec((tm, A_WIDTH), rowblk),
                  pl.BlockSpec((tm, B_WIDTH), rowblk),
                  pl.BlockSpec((tm, C_WIDTH), rowblk),
                  pl.BlockSpec((A_WIDTH, d), full),
                  pl.BlockSpec((B_WIDTH, d), full),
                  pl.BlockSpec((C_WIDTH, d), full),
                  pl.BlockSpec((1, d), full)],
        out_specs=pl.BlockSpec((tm, d), rowblk),
        out_shape=jax.ShapeDtypeStruct((m, d), F32),
        compiler_params=_params(("parallel",), 48),
        name="out_proj",
    )(h, oa, ob, oc, wa, wb, wc, g.reshape(1, d))


def _xattn_kernel(h_ref, gpre_ref, wq_ref, k_ref, v_ref, wo_ref, gpost_ref, o_ref):
    x = h_ref[0]
    xn = _rms(x, gpre_ref[...]).astype(BF16)
    q = _dot(xn, wq_ref[...])
    scale = X_DIM ** -0.5
    outs = []
    for hh in range(X_HEADS):
        cs = slice(hh * X_DIM, (hh + 1) * X_DIM)
        s = _dot_nt(q[:, cs].astype(BF16), k_ref[0, :, cs]) * scale
        p = jnp.exp(s - jnp.max(s, axis=-1, keepdims=True))
        p = p * (1.0 / jnp.sum(p, axis=-1, keepdims=True))
        outs.append(_dot(p.astype(BF16), v_ref[0, :, cs]).astype(BF16))
    o = jnp.concatenate(outs, axis=1)
    xa = _dot(o, wo_ref[...])
    o_ref[0] = x + _rms(xa, gpost_ref[...])


def _xattn(h, kv, gpre, wq, wo, gpost, tm):
    b, s, d = h.shape
    mlen = kv.shape[1]
    hw = X_HEADS * X_DIM
    return pl.pallas_call(
        _xattn_kernel,
        grid=(b, s // tm),
        in_specs=[pl.BlockSpec((1, tm, d), lambda bi, i: (bi, i, 0)),
                  pl.BlockSpec((1, d), lambda bi, i: (0, 0)),
                  pl.BlockSpec((d, hw), lambda bi, i: (0, 0)),
                  pl.BlockSpec((1, mlen, hw), lambda bi, i: (bi, 0, 0)),
                  pl.BlockSpec((1, mlen, hw), lambda bi, i: (bi, 0, 1)),
                  pl.BlockSpec((hw, d), lambda bi, i: (0, 0)),
                  pl.BlockSpec((1, d), lambda bi, i: (0, 0))],
        out_specs=pl.BlockSpec((1, tm, d), lambda bi, i: (bi, i, 0)),
        out_shape=jax.ShapeDtypeStruct((b, s, d), F32),
        compiler_params=_params(("parallel", "parallel"), 48),
        name="xattn",
    )(h, gpre.reshape(1, d), wq, kv, kv, wo, gpost.reshape(1, d))


def _ffn_kernel(x_ref, gpre_ref, wg_ref, wu_ref, wd_ref, gpost_ref, o_ref, xn_ref, acc_ref):
    j = pl.program_id(1)

    @pl.when(j == 0)
    def _():
        xn_ref[...] = _rms(x_ref[...], gpre_ref[...]).astype(BF16)
        acc_ref[...] = jnp.zeros(acc_ref.shape, F32)

    xn = xn_ref[...]
    gate = _dot(xn, wg_ref[...])
    up = _dot(xn, wu_ref[...])
    act = gate * (1.0 / (1.0 + jnp.exp(-gate))) * up
    acc_ref[...] += _dot(act.astype(BF16), wd_ref[...])

    @pl.when(j == pl.num_programs(1) - 1)
    def _():
        o_ref[...] = x_ref[...] + _rms(acc_ref[...], gpost_ref[...])


def _ffn(h, gpre, w_gu, w_down, gpost, tm, tf):
    m, d = h.shape
    f = w_down.shape[0]
    nf = f // tf
    return pl.pallas_call(
        _ffn_kernel,
        grid=(m // tm, nf),
        in_specs=[pl.BlockSpec((tm, d), lambda i, j: (i, 0)),
                  pl.BlockSpec((1, d), lambda i, j: (0, 0)),
                  pl.BlockSpec((d, tf), lambda i, j: (0, j)),
                  pl.BlockSpec((d, tf), lambda i, j: (0, nf + j)),
                  pl.BlockSpec((tf, d), lambda i, j: (j, 0)),
                  pl.BlockSpec((1, d), lambda i, j: (0, 0))],
        out_specs=pl.BlockSpec((tm, d), lambda i, j: (i, 0)),
        out_shape=jax.ShapeDtypeStruct((m, d), F32),
        scratch_shapes=[pltpu.VMEM((tm, d), BF16), pltpu.VMEM((tm, d), F32)],
        compiler_params=_params(("parallel", "arbitrary"), 56),
        name="ffn",
    )(h, gpre.reshape(1, d), w_gu, w_gu, w_down, gpost.reshape(1, d))


def _tile(n, pref):
    t = min(pref, n)
    assert n % t == 0
    return t


def kernel(x, mem, rel_bias, mix_pre_g, mix_post_g, w_in, w_out, a_lambda, a_sub_g, b_cq_g, b_ckv_g, b_w_uq, b_w_qidx, b_w_uk, b_w_uv, c_ln_g, c_ln_b, c_w_s, c_b_s, x_pre_g, x_post_g, mem_g, x_wq, x_wkv, x_wo, f_pre_g, f_post_g, f_w_gu, f_w_down):
    bsz, s_len, d = x.shape
    depth = w_in.shape[0]
    tokens = bsz * s_len
    mlen = mem.shape[1]
    t_attn = min(ATTN_BLOCK, s_len)
    bias_a = _bias_tiles(rel_bias[:, :A_HEADS], t_attn)
    bias_b = _bias_tiles(rel_bias[:, A_HEADS:], t_attn)
    sizes = (A_WIDTH, A_WIDTH, A_WIDTH, B_Q_RANK, B_KV_RANK, IDX_DIM, IDX_HEADS, C_WIDTH, C_WIDTH)
    offs = np.concatenate([[0], np.cumsum(sizes)])
    tm = _tile(tokens, 512)

    h = x.reshape(tokens, d)
    mem2 = mem.reshape(bsz * mlen, d)
    for l in range(depth):
        lam_init = 0.8 - 0.6 * math.exp(-0.3 * l)
        wl = w_in[l]
        cols = [wl[:, offs[i]:offs[i + 1]] for i in range(len(sizes))]
        wqa, wka, wva, wcq, wckv, wkidx, wwidx, wcu, wcv = cols
        w_qkv = jnp.concatenate([wqa * (A_HALF ** -0.5 * LOG2E), wka, wva], axis=1).astype(BF16)
        w_rest = jnp.concatenate(
            [wcu, wcv, wcq, wckv, wkidx, wkidx, wwidx,
             jnp.zeros((d, 128 - IDX_HEADS), wl.dtype)], axis=1).astype(BF16)

        qkv = _norm_matmul(h, mix_pre_g[l], w_qkv, BF16, tm, 768)
        rest = _norm_matmul(h, mix_pre_g[l], w_rest, F32, tm, 1024)

        lp = a_lambda[l].astype(F32)
        lam = jnp.exp(jnp.sum(lp[0] * lp[1])) - jnp.exp(jnp.sum(lp[2] * lp[3])) + lam_init
        oa = _diff_attention(qkv.reshape(bsz, s_len, 3 * A_WIDTH), lam.reshape(1), bias_a,
                             a_sub_g[l], 1.0 - lam_init)

        wuq = b_w_uq[l].reshape(B_Q_RANK, B_WIDTH).astype(BF16)
        wuk = jnp.transpose(b_w_uk[l], (1, 2, 0)).astype(BF16)
        wqi = b_w_qidx[l].reshape(B_Q_RANK, IDX_HEADS * IDX_DIM).astype(BF16)
        wuv = jnp.transpose(b_w_uv[l], (1, 2, 0)).astype(BF16)
        qlat, qidx, ckvn, kk, wis = _dsa_prep(rest, b_cq_g[l], b_ckv_g[l], wuq, wuk, wqi, tm)
        r3 = lambda a: a.reshape(bsz, s_len, a.shape[-1])
        ob = _dsa_attention(r3(qlat), r3(qidx), r3(wis), r3(kk), r3(ckvn), bias_b, wuv)

        oc = _gmlp(rest, c_ln_g[l], c_ln_b[l], c_w_s[l], c_b_s[l], tm)

        wo_l = w_out[l].astype(BF16)
        h = _out_proj(h, oa.reshape(tokens, A_WIDTH), ob.reshape(tokens, B_WIDTH), oc,
                      wo_l[:A_WIDTH], wo_l[A_WIDTH:A_WIDTH + B_WIDTH], wo_l[A_WIDTH + B_WIDTH:],
                      mix_post_g[l], tm)

        hw = X_HEADS * X_DIM
        kv = _norm_matmul(mem2, mem_g[l], x_wkv[l].reshape(d, 2 * hw).astype(BF16), BF16,
                          _tile(bsz * mlen, 512), 512)
        h = _xattn(h.reshape(bsz, s_len, d), kv.reshape(bsz, mlen, 2 * hw), x_pre_g[l],
                   x_wq[l].reshape(d, hw).astype(BF16), x_wo[l].reshape(hw, d).astype(BF16),
                   x_post_g[l], _tile(s_len, 512)).reshape(tokens, d)

        fh = f_w_down.shape[1]
        h = _ffn(h, f_pre_g[l], f_w_gu[l].reshape(d, 2 * fh).astype(BF16),
                 f_w_down[l].astype(BF16), f_post_g[l], tm, 512)
    return h.reshape(bsz, s_len, d)
```

```python
import functools
import math

import numpy as np
import jax
import jax.numpy as jnp
from jax import lax
from jax.experimental import pallas as pl
from jax.experimental.pallas import tpu as pltpu

F32 = jnp.float32
BF16 = jnp.bfloat16
EPS = 1e-6
NEG = -1e30
INT_MIN = -(2 ** 31)
I16_MIN = -(2 ** 15)

HEAD_DIM = 128
A_HEADS = 6
A_HALF = 64
B_HEADS = 6
B_Q_RANK = 512
B_KV_RANK = 256
IDX_HEADS = 16
IDX_DIM = 64
TOPK_MAX = 256
C_GROUPS = 4
C_WIDTH = 512
CHUNK = 128
X_HEADS = 4
X_DIM = 128
REL_BUCKETS = 32
REL_MAX_DIST = 128
A_WIDTH = A_HEADS * HEAD_DIM
B_WIDTH = B_HEADS * HEAD_DIM

ATTN_BLOCK = 256
MIB = 1024 * 1024


def _params(semantics, vmem_mib):
    return pltpu.CompilerParams(dimension_semantics=semantics,
                                vmem_limit_bytes=vmem_mib * MIB)


def _rms(x, g):
    return x * lax.rsqrt(jnp.mean(x * x, axis=-1, keepdims=True) + EPS) * g


def _dot(a, b):
    return jnp.dot(a, b, preferred_element_type=F32)


def _dot_nt(a, b):
    return lax.dot_general(a, b, (((1,), (1,)), ((), ())), preferred_element_type=F32)


def _norm_matmul_kernel(x_ref, g_ref, w_ref, o_ref, xn_ref):
    @pl.when(pl.program_id(1) == 0)
    def _():
        xn_ref[...] = _rms(x_ref[...], g_ref[...]).astype(BF16)

    o_ref[...] = _dot(xn_ref[...], w_ref[...]).astype(o_ref.dtype)


def _norm_matmul(x, g, w, out_dtype, tm, tn):
    m, d = x.shape
    n = w.shape[1]
    return pl.pallas_call(
        _norm_matmul_kernel,
        grid=(m // tm, n // tn),
        in_specs=[pl.BlockSpec((tm, d), lambda i, j: (i, 0)),
                  pl.BlockSpec((1, d), lambda i, j: (0, 0)),
                  pl.BlockSpec((d, tn), lambda i, j: (0, j))],
        out_specs=pl.BlockSpec((tm, tn), lambda i, j: (i, j)),
        out_shape=jax.ShapeDtypeStruct((m, n), out_dtype),
        scratch_shapes=[pltpu.VMEM((tm, d), BF16)],
        compiler_params=_params(("parallel", "arbitrary"), 48),
        name="norm_matmul",
    )(x, g.reshape(1, d), w)


SUM_ROWS = 16
LOG2E = math.log2(math.e)


def _accumulate(acc_ref, alpha, vt, p):
    acc_ref[...] = alpha * acc_ref[...] + _dot(vt, p)


def _attend_causal_blocks(qi, t, n_col, zero_block, lhs_block, rhs, values_at,
                          far_addend, prev_addend, diag_addend,
                          acc_ref, s0_ref, s1_ref, pa_ref, pb_ref):
    nq = n_col * t
    n_far_pairs = qi // 2
    pad = 2 * n_far_pairs - (qi - 1)

    def block_of(j):
        return jnp.clip(j - pad, 0, qi)

    def scores_into(ref, j):
        ref[...] = _dot(lhs_block(block_of(j)), rhs)

    def values(j):
        kb = j - pad
        return values_at(jnp.where(kb < 0, zero_block, kb))

    def probs_into(p_ref, s_ref, m, addend):
        m_out, alpha_out = [], []
        for c in range(n_col):
            cols = slice(c * t, (c + 1) * t)
            s = s_ref[:, cols]
            if addend is not None:
                s = s + addend(c)
            m_new = jnp.maximum(m[:, cols], jnp.max(s, axis=0, keepdims=True))
            p_ref[:, cols] = jnp.exp2(s - m_new).astype(BF16)
            alpha_out.append(jnp.exp2(m[:, cols] - m_new))
            m_out.append(m_new)
        return jnp.concatenate(m_out, axis=1), jnp.concatenate(alpha_out, axis=1)

    def step(j, carry, addend_a, addend_b):
        m, alpha_late = carry
        _accumulate(acc_ref, alpha_late, values(j - 1), pb_ref[...])
        scores_into(s1_ref, j + 1)
        m, alpha = probs_into(pa_ref, s0_ref, m, addend_a(block_of(j)))
        _accumulate(acc_ref, alpha, values(j), pa_ref[...])
        scores_into(s0_ref, j + 2)
        m, alpha_late = probs_into(pb_ref, s1_ref, m, addend_b(block_of(j + 1)))
        return m, alpha_late

    acc_ref[...] = jnp.zeros(acc_ref.shape, F32)
    pb_ref[...] = jnp.zeros(pb_ref.shape, BF16)
    scores_into(s0_ref, 0)
    carry = (jnp.full((1, nq), NEG, F32), jnp.ones((1, nq), F32))
    carry = lax.fori_loop(0, n_far_pairs, lambda i, c: step(2 * i, c, far_addend, far_addend), carry)
    j_near = 2 * n_far_pairs
    _, alpha_late = step(j_near, carry, prev_addend, diag_addend)
    _accumulate(acc_ref, alpha_late, values(j_near + 1), pb_ref[...])


def _with_sum_rows(vt):
    lead = vt.shape[:-2]
    tk = vt.shape[-1]
    extra = jnp.concatenate([jnp.ones(lead + (1, tk), vt.dtype),
                             jnp.zeros(lead + (SUM_ROWS - 1, tk), vt.dtype)], axis=-2)
    return jnp.concatenate([vt, extra], axis=-2)


def _bucket_np(dist):
    n = np.maximum(dist, 0)
    max_exact = REL_BUCKETS // 2
    nf = np.maximum(n, 1).astype(np.float64)
    large = max_exact + (np.log(nf / max_exact) / math.log(REL_MAX_DIST / max_exact)
                         * (REL_BUCKETS - max_exact)).astype(np.int32)
    large = np.minimum(large, REL_BUCKETS - 1)
    return np.where(n < max_exact, n, large)


NO_BLOCK, PREV_BLOCK, DIAG_BLOCK = 0, 1, 2


def _bias_tiles(tab, t):
    assert t >= REL_MAX_DIST
    length = 2 * t
    bucket = jnp.asarray(_bucket_np(np.arange(length)))
    by_dist = ((jnp.take(tab, bucket, axis=0) - tab[REL_BUCKETS - 1]) * LOG2E).T

    def skew(v):
        flat = jnp.tile(v, (1, t))[:, :t * (length - 1)]
        return flat.reshape(-1, t, length - 1)[:, :, :t]

    diag = skew(by_dist)
    prev = skew(jnp.roll(by_dist, -t, axis=1))
    key = np.arange(t)[:, None]
    query = np.arange(t)[None, :]
    diag = jnp.where(jnp.asarray(query >= key)[None], diag, NEG)
    return jnp.stack([jnp.full_like(prev, NEG), prev, diag], axis=1).astype(F32)


def _diff_attn_kernel(lam_ref, qt_ref, k_ref, vt_ref, bias_ref, g_ref, o_ref, acc_ref, s0_ref, s1_ref, pa_ref, pb_ref,
                      *, t, out_scale):
    qi = pl.program_id(2)
    qt = qt_ref[0]
    sub = lax.broadcasted_iota(jnp.int32, qt.shape, 0)
    zero = jnp.zeros_like(qt)
    qs = jnp.concatenate([jnp.where(sub < A_HALF, qt, zero),
                          jnp.where(sub >= A_HALF, qt, zero)], axis=1)
    def lhs_block(kb):
        return k_ref[0, pl.ds(pl.multiple_of(kb * t, t), t), :]

    prev_kind = jnp.where(qi >= 1, PREV_BLOCK, NO_BLOCK)
    _attend_causal_blocks(
        qi, t, 2, vt_ref.shape[2] - 1, lhs_block, qs, lambda kb: vt_ref[0, 0, kb],
        lambda kb: None,
        lambda kb: (lambda c: bias_ref[0, prev_kind]),
        lambda kb: (lambda c: bias_ref[0, DIAG_BLOCK]),
        acc_ref, s0_ref, s1_ref, pa_ref, pb_ref)

    acc = acc_ref[...]
    o = acc[:HEAD_DIM] * (1.0 / acc[HEAD_DIM:HEAD_DIM + 1])
    o = o[:, :t] - lam_ref[0] * o[:, t:]
    y = o * lax.rsqrt(jnp.mean(o * o, axis=0, keepdims=True) + EPS) * g_ref[...] * out_scale
    o_ref[0] = y.T.astype(o_ref.dtype)


def _diff_attention(qkv, lam, bias, sub_g, out_scale):
    b, s, _ = qkv.shape
    t = min(ATTN_BLOCK, s)
    nkb = s // t
    qt = jnp.swapaxes(qkv[:, :, :A_WIDTH], 1, 2)
    vt = jnp.transpose(qkv[:, :, 2 * A_WIDTH:].reshape(b, nkb, t, A_HEADS, HEAD_DIM),
                       (0, 3, 1, 4, 2))
    vt = _with_sum_rows(vt)
    vt = jnp.concatenate([vt, jnp.zeros_like(vt[:, :, :1])], axis=2)
    kern = functools.partial(_diff_attn_kernel, t=t, out_scale=out_scale)
    return pl.pallas_call(
        kern,
        grid=(b, A_HEADS, s // t),
        in_specs=[pl.BlockSpec(memory_space=pltpu.SMEM),
                  pl.BlockSpec((1, HEAD_DIM, t), lambda bi, h, qi: (bi, h, qi)),
                  pl.BlockSpec((1, s, HEAD_DIM), lambda bi, h, qi: (bi, 0, A_HEADS + h)),
                  pl.BlockSpec((1, 1, nkb + 1, HEAD_DIM + SUM_ROWS, t), lambda bi, h, qi: (bi, h, 0, 0, 0)),
                  pl.BlockSpec((1, 3, t, t), lambda bi, h, qi: (h, 0, 0, 0)),
                  pl.BlockSpec((HEAD_DIM, 1), lambda bi, h, qi: (0, 0))],
        out_specs=pl.BlockSpec((1, t, HEAD_DIM), lambda bi, h, qi: (bi, qi, h)),
        out_shape=jax.ShapeDtypeStruct((b, s, A_WIDTH), BF16),
        scratch_shapes=[pltpu.VMEM((HEAD_DIM + SUM_ROWS, 2 * t), F32),
                        pltpu.VMEM((t, 2 * t), F32), pltpu.VMEM((t, 2 * t), F32),
                        pltpu.VMEM((t, 2 * t), BF16), pltpu.VMEM((t, 2 * t), BF16)],
        compiler_params=_params(("parallel", "parallel", "arbitrary"), 32),
        name="diff_attention",
    )(lam, qt, qkv, vt, bias, sub_g.reshape(HEAD_DIM, 1))


def _dsa_prep_kernel(cq_ref, ckv_ref, kk_ref, wi_ref, gq_ref, gkv_ref, wuq_ref, wuk_ref, wqi_ref,
                     qlat_ref, qidx_ref, ckvn_ref, kkb_ref, wis_ref):
    cqn = _rms(cq_ref[...], gq_ref[...]).astype(BF16)
    ckvn_ref[...] = _rms(ckv_ref[...], gkv_ref[...]).astype(BF16)
    q = _dot(cqn, wuq_ref[...])
    for h in range(B_HEADS):
        qh = q[:, h * HEAD_DIM:(h + 1) * HEAD_DIM].astype(BF16)
        qlat = _dot(qh, wuk_ref[h]) * (HEAD_DIM ** -0.5 * LOG2E)
        qlat_ref[:, h * B_KV_RANK:(h + 1) * B_KV_RANK] = qlat.astype(BF16)
    qidx_ref[...] = _dot(cqn, wqi_ref[...]).astype(BF16)
    kkb_ref[...] = kk_ref[...].astype(BF16)
    wis_ref[...] = wi_ref[:, :IDX_HEADS] * (IDX_HEADS ** -0.5 * IDX_DIM ** -0.5)


def _dsa_prep(rest, gq, gkv, wuq, wuk, wqi, tm):
    m = rest.shape[0]
    row = lambda c: (lambda i: (i, c))
    full2 = lambda i: (0, 0)
    return pl.pallas_call(
        _dsa_prep_kernel,
        grid=(m // tm,),
        in_specs=[pl.BlockSpec((tm, B_Q_RANK), row(2)),
                  pl.BlockSpec((tm, B_KV_RANK), row(6)),
                  pl.BlockSpec((tm, 128), row(14)),
                  pl.BlockSpec((tm, 128), row(15)),
                  pl.BlockSpec((1, B_Q_RANK), full2),
                  pl.BlockSpec((1, B_KV_RANK), full2),
                  pl.BlockSpec((B_Q_RANK, B_WIDTH), full2),
                  pl.BlockSpec((B_HEADS, HEAD_DIM, B_KV_RANK), lambda i: (0, 0, 0)),
                  pl.BlockSpec((B_Q_RANK, IDX_HEADS * IDX_DIM), full2)],
        out_specs=[pl.BlockSpec((tm, B_HEADS * B_KV_RANK), row(0)),
                   pl.BlockSpec((tm, IDX_HEADS * IDX_DIM), row(0)),
                   pl.BlockSpec((tm, B_KV_RANK), row(0)),
                   pl.BlockSpec((tm, 128), row(0)),
                   pl.BlockSpec((tm, IDX_HEADS), row(0))],
        out_shape=[jax.ShapeDtypeStruct((m, B_HEADS * B_KV_RANK), BF16),
                   jax.ShapeDtypeStruct((m, IDX_HEADS * IDX_DIM), BF16),
                   jax.ShapeDtypeStruct((m, B_KV_RANK), BF16),
                   jax.ShapeDtypeStruct((m, 128), BF16),
                   jax.ShapeDtypeStruct((m, IDX_HEADS), F32)],
        compiler_params=_params(("parallel",), 32),
        name="dsa_prep",
    )(rest, rest, rest, rest, gq.reshape(1, -1), gkv.reshape(1, -1), wuq, wuk, wqi)


def _dsa_kernel(qlat_ref, qidx_ref, wis_ref, kk_ref, ckvn_ref, ckvnt_ref, bias_ref, wuv_ref, o_ref,
                key_ref, hi_ref, lo_ref, acc_ref, s0_ref, s1_ref, pa_ref, pb_ref, *, t, topk):
    qi = pl.program_id(1)
    nblk = qi + 1
    nkb = key_ref.shape[0]
    key_pos = lax.broadcasted_iota(jnp.int32, (t, t), 0)
    query_pos = lax.broadcasted_iota(jnp.int32, (t, t), 1)

    qidx = qidx_ref[0]
    wis = wis_ref[0]
    sub = lax.broadcasted_iota(jnp.int32, (2 * IDX_DIM, t), 0)
    q_heads = []
    for g in range(IDX_HEADS):
        pair = qidx[(g // 2) * 2 * IDX_DIM:(g // 2 + 1) * 2 * IDX_DIM]
        keep = (sub < IDX_DIM) if g % 2 == 0 else (sub >= IDX_DIM)
        q_heads.append(jnp.where(keep, pair, jnp.zeros_like(pair)))

    def score_block(kb, c):
        ks = pl.multiple_of(kb * t, t)
        kblk = kk_ref[0, pl.ds(ks, t), :]
        score = jnp.zeros((t, t), F32)
        for g in range(IDX_HEADS):
            d = _dot(kblk, q_heads[g])
            score = score + jnp.maximum(d, 0.0) * wis[g:g + 1]
        bits = pltpu.bitcast(score, jnp.int32)
        key = jnp.where(bits < 0, bits ^ jnp.int32(0x7FFFFFFF), bits)
        causal = (kb < qi) | (query_pos >= key_pos)
        key = jnp.where(causal, key, jnp.int32(INT_MIN))
        key_ref[kb] = key
        hi_ref[kb] = (key >> 16).astype(jnp.int16)
        lo_ref[kb] = ((key & 0xFFFF) + I16_MIN).astype(jnp.int16)
        return c

    lax.fori_loop(0, nblk, score_block, 0)

    hi_ref[nkb] = jnp.full((t, t), I16_MIN, jnp.int16)
    lo_ref[nkb] = jnp.full((t, t), I16_MIN, jnp.int16)
    one, zero = jnp.int16(1), jnp.int16(0)

    def count_ge(plane_ref, cand):
        cand_b = jnp.broadcast_to(cand.astype(jnp.int16), (t, t))

        def body(i, c):
            second = jnp.where(2 * i + 1 < nblk, 2 * i + 1, nkb)
            for kb in (2 * i, second):
                x = jnp.where(plane_ref[kb] >= cand_b, one, zero)
                for r in range(t // 16):
                    c = c + x[r * 16:(r + 1) * 16]
            return c

        c = lax.fori_loop(0, (nblk + 1) // 2, body, jnp.zeros((16, t), jnp.int16))
        return jnp.sum(c.astype(jnp.int32), axis=0, keepdims=True)

    def greedy_bits(count_at_least):
        def bit_step(i, v):
            cand = v + jnp.left_shift(jnp.int32(1), 15 - i)
            return jnp.where(count_at_least(cand) >= topk, cand, v)
        return lax.fori_loop(0, 16, bit_step, jnp.full((1, t), I16_MIN, jnp.int32))

    hi = greedy_bits(lambda cand: count_ge(hi_ref, cand))
    above = count_ge(hi_ref, hi + 1)
    hi_b = jnp.broadcast_to(hi.astype(jnp.int16), (t, t))

    def restrict(kb, c):
        lo_ref[kb] = jnp.where(hi_ref[kb] == hi_b, lo_ref[kb], jnp.int16(I16_MIN))
        return c

    lax.fori_loop(0, nblk, restrict, 0)
    lo = greedy_bits(lambda cand: above + count_ge(lo_ref, cand))
    thr = jnp.left_shift(hi, 16) + (lo - I16_MIN)
    thr_b = jnp.broadcast_to(jnp.maximum(thr, jnp.int32(INT_MIN + 1)), (t, t))

    def lhs_block(kb):
        return ckvn_ref[0, pl.ds(pl.multiple_of(kb * t, t), t), :]

    def selection(kb):
        return jnp.where(key_ref[kb] >= thr_b, 0.0, NEG)

    def far_addend(kb):
        mask = selection(kb)
        return lambda h: mask

    def near_addend(kind):
        def addend(kb):
            mask = selection(kb)
            return lambda h: bias_ref[h, kind] + mask
        return addend

    _attend_causal_blocks(
        qi, t, B_HEADS, nkb, lhs_block, qlat_ref[0, 0], lambda kb: ckvnt_ref[0, kb],
        far_addend, near_addend(jnp.where(qi >= 1, PREV_BLOCK, NO_BLOCK)), near_addend(DIAG_BLOCK),
        acc_ref, s0_ref, s1_ref, pa_ref, pb_ref)

    for h in range(B_HEADS):
        cols = slice(h * t, (h + 1) * t)
        o_lat = (acc_ref[:B_KV_RANK, cols] * (1.0 / acc_ref[B_KV_RANK:B_KV_RANK + 1, cols])).astype(BF16)
        o = _dot(wuv_ref[h], o_lat)
        o_ref[0, :, h * HEAD_DIM:(h + 1) * HEAD_DIM] = o.T.astype(o_ref.dtype)


def _dsa_attention(qlat, qidx, wis, kk, ckvn, bias, wuv):
    b, s, _ = qlat.shape
    t = min(ATTN_BLOCK, s)
    nkb = s // t
    topk = min(TOPK_MAX, s // 4)
    qlat_t = jnp.transpose(qlat.reshape(b, nkb, t, B_HEADS, B_KV_RANK), (0, 1, 4, 3, 2)).reshape(
        b, nkb, B_KV_RANK, B_HEADS * t)
    qidx_t = jnp.swapaxes(qidx, 1, 2)
    wis_t = jnp.swapaxes(wis, 1, 2)
    ckvn_t = _with_sum_rows(jnp.swapaxes(ckvn.reshape(b, nkb, t, B_KV_RANK), 2, 3))
    ckvn_t = jnp.concatenate([ckvn_t, jnp.zeros_like(ckvn_t[:, :1])], axis=1)
    kern = functools.partial(_dsa_kernel, t=t, topk=topk)
    qcol = lambda bi, qi: (bi, 0, qi)
    allk = lambda bi, qi: (bi, 0, 0)
    return pl.pallas_call(
        kern,
        grid=(b, s // t),
        in_specs=[pl.BlockSpec((1, 1, B_KV_RANK, B_HEADS * t), lambda bi, qi: (bi, qi, 0, 0)),
                  pl.BlockSpec((1, IDX_HEADS * IDX_DIM, t), qcol),
                  pl.BlockSpec((1, IDX_HEADS, t), qcol),
                  pl.BlockSpec((1, s, 128), allk),
                  pl.BlockSpec((1, s, B_KV_RANK), allk),
                  pl.BlockSpec((1, nkb + 1, B_KV_RANK + SUM_ROWS, t), lambda bi, qi: (bi, 0, 0, 0)),
                  pl.BlockSpec((B_HEADS, 3, t, t), lambda bi, qi: (0, 0, 0, 0)),
                  pl.BlockSpec((B_HEADS, HEAD_DIM, B_KV_RANK), lambda bi, qi: (0, 0, 0))],
        out_specs=pl.BlockSpec((1, t, B_WIDTH), lambda bi, qi: (bi, qi, 0)),
        out_shape=jax.ShapeDtypeStruct((b, s, B_WIDTH), BF16),
        scratch_shapes=[pltpu.VMEM((nkb, t, t), jnp.int32),
                        pltpu.VMEM((nkb + 1, t, t), jnp.int16), pltpu.VMEM((nkb + 1, t, t), jnp.int16),
                        pltpu.VMEM((B_KV_RANK + SUM_ROWS, B_HEADS * t), F32),
                        pltpu.VMEM((t, B_HEADS * t), F32), pltpu.VMEM((t, B_HEADS * t), F32),
                        pltpu.VMEM((t, B_HEADS * t), BF16), pltpu.VMEM((t, B_HEADS * t), BF16)],
        compiler_params=_params(("parallel", "arbitrary"), 56),
        name="dsa_attention",
    )(qlat_t, qidx_t, wis_t, kk, ckvn, ckvn_t, bias, wuv)


def _gelu(x):
    return x * (0.5 * (1.0 + jnp.tanh(math.sqrt(2.0 / math.pi) * (x + 0.044715 * (x * x * x)))))


def _gmlp_kernel(cu_ref, cv_ref, g_ref, b_ref, ws_ref, bs_ref, o_ref, *, ts):
    u = _gelu(cu_ref[...])
    v = _gelu(cv_ref[...])
    mu = jnp.mean(v, axis=-1, keepdims=True)
    vc = v - mu
    var = jnp.mean(vc * vc, axis=-1, keepdims=True)
    vn = (vc * lax.rsqrt(var + EPS) * g_ref[...] + b_ref[...]).astype(BF16)
    row = lax.broadcasted_iota(jnp.int32, (CHUNK, CHUNK), 0)
    col = lax.broadcasted_iota(jnp.int32, (CHUNK, CHUNK), 1)
    for g in range(C_GROUPS):
        w = jnp.where(row >= col, ws_ref[g], 0.0).astype(BF16)
        bcol = bs_ref[:, g:g + 1]
        cs = slice(g * 128, (g + 1) * 128)
        for c in range(ts // CHUNK):
            rs = slice(c * CHUNK, (c + 1) * CHUNK)
            y = _dot(w, vn[rs, cs]) + bcol
            o_ref[rs, cs] = (u[rs, cs] * y).astype(o_ref.dtype)


def _gmlp(rest, ln_g, ln_b, w_s, b_s, ts):
    m = rest.shape[0]
    kern = functools.partial(_gmlp_kernel, ts=ts)
    return pl.pallas_call(
        kern,
        grid=(m // ts,),
        in_specs=[pl.BlockSpec((ts, C_WIDTH), lambda i: (i, 0)),
                  pl.BlockSpec((ts, C_WIDTH), lambda i: (i, 1)),
                  pl.BlockSpec((1, C_WIDTH), lambda i: (0, 0)),
                  pl.BlockSpec((1, C_WIDTH), lambda i: (0, 0)),
                  pl.BlockSpec((C_GROUPS, CHUNK, CHUNK), lambda i: (0, 0, 0)),
                  pl.BlockSpec((CHUNK, C_GROUPS), lambda i: (0, 0))],
        out_specs=pl.BlockSpec((ts, C_WIDTH), lambda i: (i, 0)),
        out_shape=jax.ShapeDtypeStruct((m, C_WIDTH), BF16),
        compiler_params=_params(("parallel",), 32),
        name="gmlp",
    )(rest, rest, ln_g.reshape(1, -1), ln_b.reshape(1, -1), w_s, jnp.transpose(b_s))


def _out_proj_kernel(h_ref, oa_ref, ob_ref, oc_ref, wa_ref, wb_ref, wc_ref, g_ref, o_ref):
    y = _dot(oa_ref[...], wa_ref[...]) + _dot(ob_ref[...], wb_ref[...]) + _dot(oc_ref[...], wc_ref[...])
    o_ref[...] = h_ref[...] + _rms(y, g_ref[...])


def _out_proj(h, oa, ob, oc, wa, wb, wc, g, tm):
    m, d = h.shape
    rowblk = lambda i: (i, 0)
    full = lambda i: (0, 0)
    return pl.pallas_call(
        _out_proj_kernel,
        grid=(m // tm,),
        in_specs=[pl.BlockSpec((tm, d), rowblk),
                  pl.BlockSpec((tm, A_WIDTH), rowblk),
                  pl.BlockSpec((tm, B_WIDTH), rowblk),
                  pl.BlockSpec((tm, C_WIDTH), rowblk),
                  pl.BlockSpec((A_WIDTH, d), full),
                  pl.BlockSpec((B_WIDTH, d), full),
                  pl.BlockSpec((C_WIDTH, d), full),
                  pl.BlockSpec((1, d), full)],
        out_specs=pl.BlockSpec((tm, d), rowblk),
        out_shape=jax.ShapeDtypeStruct((m, d), F32),
        compiler_params=_params(("parallel",), 48),
        name="out_proj",
    )(h, oa, ob, oc, wa, wb, wc, g.reshape(1, d))


def _xattn_kernel(h_ref, gpre_ref, wq_ref, k_ref, v_ref, wo_ref, gpost_ref, o_ref):
    x = h_ref[0]
    xn = _rms(x, gpre_ref[...]).astype(BF16)
    q = _dot(xn, wq_ref[...])
    scale = X_DIM ** -0.5
    outs = []
    for hh in range(X_HEADS):
        cs = slice(hh * X_DIM, (hh + 1) * X_DIM)
        s = _dot_nt(q[:, cs].astype(BF16), k_ref[0, :, cs]) * scale
        p = jnp.exp(s - jnp.max(s, axis=-1, keepdims=True))
        p = p * (1.0 / jnp.sum(p, axis=-1, keepdims=True))
        outs.append(_dot(p.astype(BF16), v_ref[0, :, cs]).astype(BF16))
    o = jnp.concatenate(outs, axis=1)
    xa = _dot(o, wo_ref[...])
    o_ref[0] = x + _rms(xa, gpost_ref[...])


def _xattn(h, kv, gpre, wq, wo, gpost, tm):
    b, s, d = h.shape
    mlen = kv.shape[1]
    hw = X_HEADS * X_DIM
    return pl.pallas_call(
        _xattn_kernel,
        grid=(b, s // tm),
        in_specs=[pl.BlockSpec((1, tm, d), lambda bi, i: (bi, i, 0)),
                  pl.BlockSpec((1, d), lambda bi, i: (0, 0)),
                  pl.BlockSpec((d, hw), lambda bi, i: (0, 0)),
                  pl.BlockSpec((1, mlen, hw), lambda bi, i: (bi, 0, 0)),
                  pl.BlockSpec((1, mlen, hw), lambda bi, i: (bi, 0, 1)),
                  pl.BlockSpec((hw, d), lambda bi, i: (0, 0)),
                  pl.BlockSpec((1, d), lambda bi, i: (0, 0))],
        out_specs=pl.BlockSpec((1, tm, d), lambda bi, i: (bi, i, 0)),
        out_shape=jax.ShapeDtypeStruct((b, s, d), F32),
        compiler_params=_params(("parallel", "parallel"), 48),
        name="xattn",
    )(h, gpre.reshape(1, d), wq, kv, kv, wo, gpost.reshape(1, d))


def _ffn_kernel(x_ref, gpre_ref, wg_ref, wu_ref, wd_ref, gpost_ref, o_ref, xn_ref, acc_ref):
    j = pl.program_id(1)

    @pl.when(j == 0)
    def _():
        xn_ref[...] = _rms(x_ref[...], gpre_ref[...]).astype(BF16)
        acc_ref[...] = jnp.zeros(acc_ref.shape, F32)

    xn = xn_ref[...]
    gate = _dot(xn, wg_ref[...])
    up = _dot(xn, wu_ref[...])
    act = gate * (1.0 / (1.0 + jnp.exp(-gate))) * up
    acc_ref[...] += _dot(act.astype(BF16), wd_ref[...])

    @pl.when(j == pl.num_programs(1) - 1)
    def _():
        o_ref[...] = x_ref[...] + _rms(acc_ref[...], gpost_ref[...])


def _ffn(h, gpre, w_gu, w_down, gpost, tm, tf):
    m, d = h.shape
    f = w_down.shape[0]
    nf = f // tf
    return pl.pallas_call(
        _ffn_kernel,
        grid=(m // tm, nf),
        in_specs=[pl.BlockSpec((tm, d), lambda i, j: (i, 0)),
                  pl.BlockSpec((1, d), lambda i, j: (0, 0)),
                  pl.BlockSpec((d, tf), lambda i, j: (0, j)),
                  pl.BlockSpec((d, tf), lambda i, j: (0, nf + j)),
                  pl.BlockSpec((tf, d), lambda i, j: (j, 0)),
                  pl.BlockSpec((1, d), lambda i, j: (0, 0))],
        out_specs=pl.BlockSpec((tm, d), lambda i, j: (i, 0)),
        out_shape=jax.ShapeDtypeStruct((m, d), F32),
        scratch_shapes=[pltpu.VMEM((tm, d), BF16), pltpu.VMEM((tm, d), F32)],
        compiler_params=_params(("parallel", "arbitrary"), 56),
        name="ffn",
    )(h, gpre.reshape(1, d), w_gu, w_gu, w_down, gpost.reshape(1, d))


def _tile(n, pref):
    t = min(pref, n)
    assert n % t == 0
    return t


def kernel(x, mem, rel_bias, mix_pre_g, mix_post_g, w_in, w_out, a_lambda, a_sub_g, b_cq_g, b_ckv_g, b_w_uq, b_w_qidx, b_w_uk, b_w_uv, c_ln_g, c_ln_b, c_w_s, c_b_s, x_pre_g, x_post_g, mem_g, x_wq, x_wkv, x_wo, f_pre_g, f_post_g, f_w_gu, f_w_down):
    bsz, s_len, d = x.shape
    depth = w_in.shape[0]
    tokens = bsz * s_len
    mlen = mem.shape[1]
    t_attn = min(ATTN_BLOCK, s_len)
    bias_a = _bias_tiles(rel_bias[:, :A_HEADS], t_attn)
    bias_b = _bias_tiles(rel_bias[:, A_HEADS:], t_attn)
    sizes = (A_WIDTH, A_WIDTH, A_WIDTH, B_Q_RANK, B_KV_RANK, IDX_DIM, IDX_HEADS, C_WIDTH, C_WIDTH)
    offs = np.concatenate([[0], np.cumsum(sizes)])
    tm = _tile(tokens, 512)

    h = x.reshape(tokens, d)
    mem2 = mem.reshape(bsz * mlen, d)
    for l in range(depth):
        lam_init = 0.8 - 0.6 * math.exp(-0.3 * l)
        wl = w_in[l]
        cols = [wl[:, offs[i]:offs[i + 1]] for i in range(len(sizes))]
        wqa, wka, wva, wcq, wckv, wkidx, wwidx, wcu, wcv = cols
        w_qkv = jnp.concatenate([wqa * (A_HALF ** -0.5 * LOG2E), wka, wva], axis=1).astype(BF16)
        w_rest = jnp.concatenate(
            [wcu, wcv, wcq, wckv, wkidx, wkidx, wwidx,
             jnp.zeros((d, 128 - IDX_HEADS), wl.dtype)], axis=1).astype(BF16)

        qkv = _norm_matmul(h, mix_pre_g[l], w_qkv, BF16, tm, 768)
        rest = _norm_matmul(h, mix_pre_g[l], w_rest, F32, tm, 1024)

        lp = a_lambda[l].astype(F32)
        lam = jnp.exp(jnp.sum(lp[0] * lp[1])) - jnp.exp(jnp.sum(lp[2] * lp[3])) + lam_init
        oa = _diff_attention(qkv.reshape(bsz, s_len, 3 * A_WIDTH), lam.reshape(1), bias_a,
                             a_sub_g[l], 1.0 - lam_init)

        wuq = b_w_uq[l].reshape(B_Q_RANK, B_WIDTH).astype(BF16)
        wuk = jnp.transpose(b_w_uk[l], (1, 2, 0)).astype(BF16)
        wqi = b_w_qidx[l].reshape(B_Q_RANK, IDX_HEADS * IDX_DIM).astype(BF16)
        wuv = jnp.transpose(b_w_uv[l], (1, 2, 0)).astype(BF16)
        qlat, qidx, ckvn, kk, wis = _dsa_prep(rest, b_cq_g[l], b_ckv_g[l], wuq, wuk, wqi, tm)
        r3 = lambda a: a.reshape(bsz, s_len, a.shape[-1])
        ob = _dsa_attention(r3(qlat), r3(qidx), r3(wis), r3(kk), r3(ckvn), bias_b, wuv)

        oc = _gmlp(rest, c_ln_g[l], c_ln_b[l], c_w_s[l], c_b_s[l], tm)

        wo_l = w_out[l].astype(BF16)
        h = _out_proj(h, oa.reshape(tokens, A_WIDTH), ob.reshape(tokens, B_WIDTH), oc,
                      wo_l[:A_WIDTH], wo_l[A_WIDTH:A_WIDTH + B_WIDTH], wo_l[A_WIDTH + B_WIDTH:],
                      mix_post_g[l], tm)

        hw = X_HEADS * X_DIM
        kv = _norm_matmul(mem2, mem_g[l], x_wkv[l].reshape(d, 2 * hw).astype(BF16), BF16,
                          _tile(bsz * mlen, 512), 512)
        h = _xattn(h.reshape(bsz, s_len, d), kv.reshape(bsz, mlen, 2 * hw), x_pre_g[l],
                   x_wq[l].reshape(d, hw).astype(BF16), x_wo[l].reshape(hw, d).astype(BF16),
                   x_post_g[l], _tile(s_len, 512)).reshape(tokens, d)

        fh = f_w_down.shape[1]
        h = _ffn(h, f_pre_g[l], f_w_gu[l].reshape(d, 2 * fh).astype(BF16),
                 f_w_down[l].astype(BF16), f_post_g[l], tm, 512)
    return h.reshape(bsz, s_len, d)
```

```python
import functools
import math

import numpy as np
import jax
import jax.numpy as jnp
from jax import lax
from jax.experimental import pallas as pl
from jax.experimental.pallas import tpu as pltpu

F32 = jnp.float32
BF16 = jnp.bfloat16
EPS = 1e-6
NEG = -1e30
INT_MIN = -(2 ** 31)
I16_MIN = -(2 ** 15)

HEAD_DIM = 128
A_HEADS = 6
A_HALF = 64
B_HEADS = 6
B_Q_RANK = 512
B_KV_RANK = 256
IDX_HEADS = 16
IDX_DIM = 64
TOPK_MAX = 256
C_GROUPS = 4
C_WIDTH = 512
CHUNK = 128
X_HEADS = 4
X_DIM = 128
REL_BUCKETS = 32
REL_MAX_DIST = 128
A_WIDTH = A_HEADS * HEAD_DIM
B_WIDTH = B_HEADS * HEAD_DIM

ATTN_BLOCK = 256
MIB = 1024 * 1024


def _params(semantics, vmem_mib):
    return pltpu.CompilerParams(dimension_semantics=semantics,
                                vmem_limit_bytes=vmem_mib * MIB)


def _rms(x, g):
    return x * lax.rsqrt(jnp.mean(x * x, axis=-1, keepdims=True) + EPS) * g


def _dot(a, b):
    return jnp.dot(a, b, preferred_element_type=F32)


def _dot_nt(a, b):
    return lax.dot_general(a, b, (((1,), (1,)), ((), ())), preferred_element_type=F32)


def _norm_matmul_kernel(x_ref, g_ref, w_ref, o_ref, xn_ref):
    @pl.when(pl.program_id(1) == 0)
    def _():
        xn_ref[...] = _rms(x_ref[...], g_ref[...]).astype(BF16)

    o_ref[...] = _dot(xn_ref[...], w_ref[...]).astype(o_ref.dtype)


def _norm_matmul(x, g, w, out_dtype, tm, tn):
    m, d = x.shape
    n = w.shape[1]
    return pl.pallas_call(
        _norm_matmul_kernel,
        grid=(m // tm, n // tn),
        in_specs=[pl.BlockSpec((tm, d), lambda i, j: (i, 0)),
                  pl.BlockSpec((1, d), lambda i, j: (0, 0)),
                  pl.BlockSpec((d, tn), lambda i, j: (0, j))],
        out_specs=pl.BlockSpec((tm, tn), lambda i, j: (i, j)),
        out_shape=jax.ShapeDtypeStruct((m, n), out_dtype),
        scratch_shapes=[pltpu.VMEM((tm, d), BF16)],
        compiler_params=_params(("parallel", "arbitrary"), 48),
        name="norm_matmul",
    )(x, g.reshape(1, d), w)


IN_PROJ_CHUNK = 768


def _in_proj_kernel(x_ref, g_ref, w_ref, qkv_ref, rest_ref):
    xn = _rms(x_ref[...], g_ref[...]).astype(BF16)
    n_qkv = qkv_ref.shape[1]
    for c0 in range(0, w_ref.shape[1], IN_PROJ_CHUNK):
        c1 = min(c0 + IN_PROJ_CHUNK, w_ref.shape[1])
        y = _dot(xn, w_ref[:, c0:c1])
        if c1 <= n_qkv:
            qkv_ref[:, c0:c1] = y.astype(qkv_ref.dtype)
        else:
            assert c0 >= n_qkv
            rest_ref[:, c0 - n_qkv:c1 - n_qkv] = y


def _in_proj(x, g, w, n_qkv, tm):
    m, d = x.shape
    n_rest = w.shape[1] - n_qkv
    return pl.pallas_call(
        _in_proj_kernel,
        grid=(m // tm,),
        in_specs=[pl.BlockSpec((tm, d), lambda i: (i, 0)),
                  pl.BlockSpec((1, d), lambda i: (0, 0)),
                  pl.BlockSpec((d, w.shape[1]), lambda i: (0, 0), pipeline_mode=pl.Buffered(1))],
        out_specs=[pl.BlockSpec((tm, n_qkv), lambda i: (i, 0)),
                   pl.BlockSpec((tm, n_rest), lambda i: (i, 0))],
        out_shape=[jax.ShapeDtypeStruct((m, n_qkv), BF16),
                   jax.ShapeDtypeStruct((m, n_rest), F32)],
        compiler_params=_params(("parallel",), 56),
        name="in_proj",
    )(x, g.reshape(1, d), w)


SUM_ROWS = 16
LOG2E = math.log2(math.e)


def _accumulate(acc_ref, alpha, vt, p):
    acc_ref[...] = alpha * acc_ref[...] + _dot(vt, p)


def _attend_causal_blocks(qi, t, n_col, zero_block, lhs_block, rhs, values_at,
                          far_addend, prev_addend, diag_addend,
                          acc_ref, s0_ref, s1_ref, pa_ref, pb_ref):
    nq = n_col * t
    n_far_pairs = qi // 2
    pad = 2 * n_far_pairs - (qi - 1)

    def block_of(j):
        return jnp.clip(j - pad, 0, qi)

    def scores_into(ref, j):
        ref[...] = _dot(lhs_block(block_of(j)), rhs)

    def values(j):
        kb = j - pad
        return values_at(jnp.where(kb < 0, zero_block, kb))

    def probs_into(p_ref, s_ref, m, addend):
        m_out, alpha_out = [], []
        for c in range(n_col):
            cols = slice(c * t, (c + 1) * t)
            s = s_ref[:, cols]
            if addend is not None:
                s = s + addend(c)
            m_new = jnp.maximum(m[:, cols], jnp.max(s, axis=0, keepdims=True))
            p_ref[:, cols] = jnp.exp2(s - m_new).astype(BF16)
            alpha_out.append(jnp.exp2(m[:, cols] - m_new))
            m_out.append(m_new)
        return jnp.concatenate(m_out, axis=1), jnp.concatenate(alpha_out, axis=1)

    def step(j, carry, addend_a, addend_b):
        m, alpha_late = carry
        _accumulate(acc_ref, alpha_late, values(j - 1), pb_ref[...])
        scores_into(s1_ref, j + 1)
        m, alpha = probs_into(pa_ref, s0_ref, m, addend_a(block_of(j)))
        _accumulate(acc_ref, alpha, values(j), pa_ref[...])
        scores_into(s0_ref, j + 2)
        m, alpha_late = probs_into(pb_ref, s1_ref, m, addend_b(block_of(j + 1)))
        return m, alpha_late

    acc_ref[...] = jnp.zeros(acc_ref.shape, F32)
    pb_ref[...] = jnp.zeros(pb_ref.shape, BF16)
    scores_into(s0_ref, 0)
    carry = (jnp.full((1, nq), NEG, F32), jnp.ones((1, nq), F32))
    carry = lax.fori_loop(0, n_far_pairs, lambda i, c: step(2 * i, c, far_addend, far_addend), carry)
    j_near = 2 * n_far_pairs
    _, alpha_late = step(j_near, carry, prev_addend, diag_addend)
    _accumulate(acc_ref, alpha_late, values(j_near + 1), pb_ref[...])


def _with_sum_rows(vt):
    lead = vt.shape[:-2]
    tk = vt.shape[-1]
    extra = jnp.concatenate([jnp.ones(lead + (1, tk), vt.dtype),
                             jnp.zeros(lead + (SUM_ROWS - 1, tk), vt.dtype)], axis=-2)
    return jnp.concatenate([vt, extra], axis=-2)


def _bucket_np(dist):
    n = np.maximum(dist, 0)
    max_exact = REL_BUCKETS // 2
    nf = np.maximum(n, 1).astype(np.float64)
    large = max_exact + (np.log(nf / max_exact) / math.log(REL_MAX_DIST / max_exact)
                         * (REL_BUCKETS - max_exact)).astype(np.int32)
    large = np.minimum(large, REL_BUCKETS - 1)
    return np.where(n < max_exact, n, large)


NO_BLOCK, PREV_BLOCK, DIAG_BLOCK = 0, 1, 2


def _bias_tiles(tab, t):
    assert t >= REL_MAX_DIST
    length = 2 * t
    bucket = jnp.asarray(_bucket_np(np.arange(length)))
    by_dist = ((jnp.take(tab, bucket, axis=0) - tab[REL_BUCKETS - 1]) * LOG2E).T

    def skew(v):
        flat = jnp.tile(v, (1, t))[:, :t * (length - 1)]
        return flat.reshape(-1, t, length - 1)[:, :, :t]

    diag = skew(by_dist)
    prev = skew(jnp.roll(by_dist, -t, axis=1))
    key = np.arange(t)[:, None]
    query = np.arange(t)[None, :]
    diag = jnp.where(jnp.asarray(query >= key)[None], diag, NEG)
    return jnp.stack([jnp.full_like(prev, NEG), prev, diag], axis=1).astype(F32)


def _diff_attn_kernel(lam_ref, qt_ref, k_ref, vt_ref, bias_ref, g_ref, o_ref, acc_ref, s0_ref, s1_ref, pa_ref, pb_ref,
                      *, t, out_scale):
    qi = pl.program_id(2)
    qt = qt_ref[0]
    sub = lax.broadcasted_iota(jnp.int32, qt.shape, 0)
    zero = jnp.zeros_like(qt)
    qs = jnp.concatenate([jnp.where(sub < A_HALF, qt, zero),
                          jnp.where(sub >= A_HALF, qt, zero)], axis=1)
    def lhs_block(kb):
        return k_ref[0, pl.ds(pl.multiple_of(kb * t, t), t), :]

    prev_kind = jnp.where(qi >= 1, PREV_BLOCK, NO_BLOCK)
    _attend_causal_blocks(
        qi, t, 2, vt_ref.shape[2] - 1, lhs_block, qs, lambda kb: vt_ref[0, 0, kb],
        lambda kb: None,
        lambda kb: (lambda c: bias_ref[0, prev_kind]),
        lambda kb: (lambda c: bias_ref[0, DIAG_BLOCK]),
        acc_ref, s0_ref, s1_ref, pa_ref, pb_ref)

    acc = acc_ref[...]
    o = acc[:HEAD_DIM] * (1.0 / acc[HEAD_DIM:HEAD_DIM + 1])
    o = o[:, :t] - lam_ref[0] * o[:, t:]
    y = o * lax.rsqrt(jnp.mean(o * o, axis=0, keepdims=True) + EPS) * g_ref[...] * out_scale
    o_ref[0] = y.T.astype(o_ref.dtype)


def _diff_attention(qkv, lam, bias, sub_g, out_scale):
    b, s, _ = qkv.shape
    t = min(ATTN_BLOCK, s)
    nkb = s // t
    qt = jnp.swapaxes(qkv[:, :, :A_WIDTH], 1, 2)
    vt = jnp.transpose(qkv[:, :, 2 * A_WIDTH:].reshape(b, nkb, t, A_HEADS, HEAD_DIM),
                       (0, 3, 1, 4, 2))
    vt = _with_sum_rows(vt)
    vt = jnp.concatenate([vt, jnp.zeros_like(vt[:, :, :1])], axis=2)
    kern = functools.partial(_diff_attn_kernel, t=t, out_scale=out_scale)
    return pl.pallas_call(
        kern,
        grid=(b, A_HEADS, s // t),
        in_specs=[pl.BlockSpec(memory_space=pltpu.SMEM),
                  pl.BlockSpec((1, HEAD_DIM, t), lambda bi, h, qi: (bi, h, qi)),
                  pl.BlockSpec((1, s, HEAD_DIM), lambda bi, h, qi: (bi, 0, A_HEADS + h)),
                  pl.BlockSpec((1, 1, nkb + 1, HEAD_DIM + SUM_ROWS, t), lambda bi, h, qi: (bi, h, 0, 0, 0)),
                  pl.BlockSpec((1, 3, t, t), lambda bi, h, qi: (h, 0, 0, 0)),
                  pl.BlockSpec((HEAD_DIM, 1), lambda bi, h, qi: (0, 0))],
        out_specs=pl.BlockSpec((1, t, HEAD_DIM), lambda bi, h, qi: (bi, qi, h)),
        out_shape=jax.ShapeDtypeStruct((b, s, A_WIDTH), BF16),
        scratch_shapes=[pltpu.VMEM((HEAD_DIM + SUM_ROWS, 2 * t), F32),
                        pltpu.VMEM((t, 2 * t), F32), pltpu.VMEM((t, 2 * t), F32),
                        pltpu.VMEM((t, 2 * t), BF16), pltpu.VMEM((t, 2 * t), BF16)],
        compiler_params=_params(("parallel", "parallel", "arbitrary"), 32),
        name="diff_attention",
    )(lam, qt, qkv, vt, bias, sub_g.reshape(HEAD_DIM, 1))


def _dsa_prep_kernel(cq_ref, ckv_ref, kk_ref, wi_ref, gq_ref, gkv_ref, wuq_ref, wuk_ref, wqi_ref,
                     qlat_ref, qidx_ref, ckvn_ref, kkb_ref, wis_ref):
    cqn = _rms(cq_ref[...], gq_ref[...]).astype(BF16)
    ckvn_ref[...] = _rms(ckv_ref[...], gkv_ref[...]).astype(BF16)
    q = _dot(cqn, wuq_ref[...])
    for h in range(B_HEADS):
        qh = q[:, h * HEAD_DIM:(h + 1) * HEAD_DIM].astype(BF16)
        qlat = _dot(qh, wuk_ref[h]) * (HEAD_DIM ** -0.5 * LOG2E)
        qlat_ref[:, h * B_KV_RANK:(h + 1) * B_KV_RANK] = qlat.astype(BF16)
    qidx_ref[...] = _dot(cqn, wqi_ref[...]).astype(BF16)
    kkb_ref[...] = kk_ref[...].astype(BF16)
    wis_ref[...] = wi_ref[:, :IDX_HEADS] * (IDX_HEADS ** -0.5 * IDX_DIM ** -0.5)


def _dsa_prep(rest, gq, gkv, wuq, wuk, wqi, tm):
    m = rest.shape[0]
    row = lambda c: (lambda i: (i, c))
    full2 = lambda i: (0, 0)
    return pl.pallas_call(
        _dsa_prep_kernel,
        grid=(m // tm,),
        in_specs=[pl.BlockSpec((tm, B_Q_RANK), row(2)),
                  pl.BlockSpec((tm, B_KV_RANK), row(6)),
                  pl.BlockSpec((tm, 128), row(14)),
                  pl.BlockSpec((tm, 128), row(15)),
                  pl.BlockSpec((1, B_Q_RANK), full2),
                  pl.BlockSpec((1, B_KV_RANK), full2),
                  pl.BlockSpec((B_Q_RANK, B_WIDTH), full2),
                  pl.BlockSpec((B_HEADS, HEAD_DIM, B_KV_RANK), lambda i: (0, 0, 0)),
                  pl.BlockSpec((B_Q_RANK, IDX_HEADS * IDX_DIM), full2)],
        out_specs=[pl.BlockSpec((tm, B_HEADS * B_KV_RANK), row(0)),
                   pl.BlockSpec((tm, IDX_HEADS * IDX_DIM), row(0)),
                   pl.BlockSpec((tm, B_KV_RANK), row(0)),
                   pl.BlockSpec((tm, 128), row(0)),
                   pl.BlockSpec((tm, IDX_HEADS), row(0))],
        out_shape=[jax.ShapeDtypeStruct((m, B_HEADS * B_KV_RANK), BF16),
                   jax.ShapeDtypeStruct((m, IDX_HEADS * IDX_DIM), BF16),
                   jax.ShapeDtypeStruct((m, B_KV_RANK), BF16),
                   jax.ShapeDtypeStruct((m, 128), BF16),
                   jax.ShapeDtypeStruct((m, IDX_HEADS), F32)],
        compiler_params=_params(("parallel",), 32),
        name="dsa_prep",
    )(rest, rest, rest, rest, gq.reshape(1, -1), gkv.reshape(1, -1), wuq, wuk, wqi)


def _dsa_kernel(qlat_ref, qidx_ref, wis_ref, kk_ref, ckvn_ref, ckvnt_ref, bias_ref, wuv_ref, o_ref,
                key_ref, hi_ref, lo_ref, acc_ref, s0_ref, s1_ref, pa_ref, pb_ref, *, t, topk):
    qi = pl.program_id(1)
    nblk = qi + 1
    nkb = key_ref.shape[0]
    key_pos = lax.broadcasted_iota(jnp.int32, (t, t), 0)
    query_pos = lax.broadcasted_iota(jnp.int32, (t, t), 1)

    qidx = qidx_ref[0]
    wis = wis_ref[0]
    sub = lax.broadcasted_iota(jnp.int32, (2 * IDX_DIM, t), 0)
    q_heads = []
    for g in range(IDX_HEADS):
        pair = qidx[(g // 2) * 2 * IDX_DIM:(g // 2 + 1) * 2 * IDX_DIM]
        keep = (sub < IDX_DIM) if g % 2 == 0 else (sub >= IDX_DIM)
        q_heads.append(jnp.where(keep, pair, jnp.zeros_like(pair)))

    def score_block(kb, c):
        ks = pl.multiple_of(kb * t, t)
        kblk = kk_ref[0, pl.ds(ks, t), :]
        score = jnp.zeros((t, t), F32)
        for g in range(IDX_HEADS):
            d = _dot(kblk, q_heads[g])
            score = score + jnp.maximum(d, 0.0) * wis[g:g + 1]
        bits = pltpu.bitcast(score, jnp.int32)
        key = jnp.where(bits < 0, bits ^ jnp.int32(0x7FFFFFFF), bits)
        causal = (kb < qi) | (query_pos >= key_pos)
        key = jnp.where(causal, key, jnp.int32(INT_MIN))
        key_ref[kb] = key
        hi_ref[kb] = (key >> 16).astype(jnp.int16)
        lo_ref[kb] = ((key & 0xFFFF) + I16_MIN).astype(jnp.int16)
        return c

    lax.fori_loop(0, nblk, score_block, 0)

    hi_ref[nkb] = jnp.full((t, t), I16_MIN, jnp.int16)
    lo_ref[nkb] = jnp.full((t, t), I16_MIN, jnp.int16)
    one, zero = jnp.int16(1), jnp.int16(0)

    def count_ge(plane_ref, cand):
        cand_b = jnp.broadcast_to(cand.astype(jnp.int16), (t, t))

        def body(i, c):
            second = jnp.where(2 * i + 1 < nblk, 2 * i + 1, nkb)
            for kb in (2 * i, second):
                x = jnp.where(plane_ref[kb] >= cand_b, one, zero)
                for r in range(t // 16):
                    c = c + x[r * 16:(r + 1) * 16]
            return c

        c = lax.fori_loop(0, (nblk + 1) // 2, body, jnp.zeros((16, t), jnp.int16))
        return jnp.sum(c.astype(jnp.int32), axis=0, keepdims=True)

    def greedy_bits(count_at_least):
        def bit_step(i, v):
            cand = v + jnp.left_shift(jnp.int32(1), 15 - i)
            return jnp.where(count_at_least(cand) >= topk, cand, v)
        return lax.fori_loop(0, 16, bit_step, jnp.full((1, t), I16_MIN, jnp.int32))

    hi = greedy_bits(lambda cand: count_ge(hi_ref, cand))
    above = count_ge(hi_ref, hi + 1)
    hi_b = jnp.broadcast_to(hi.astype(jnp.int16), (t, t))

    def restrict(kb, c):
        lo_ref[kb] = jnp.where(hi_ref[kb] == hi_b, lo_ref[kb], jnp.int16(I16_MIN))
        return c

    lax.fori_loop(0, nblk, restrict, 0)
    lo = greedy_bits(lambda cand: above + count_ge(lo_ref, cand))
    thr = jnp.left_shift(hi, 16) + (lo - I16_MIN)
    thr_b = jnp.broadcast_to(jnp.maximum(thr, jnp.int32(INT_MIN + 1)), (t, t))

    def lhs_block(kb):
        return ckvn_ref[0, pl.ds(pl.multiple_of(kb * t, t), t), :]

    def selection(kb):
        return jnp.where(key_ref[kb] >= thr_b, 0.0, NEG)

    def far_addend(kb):
        mask = selection(kb)
        return lambda h: mask

    def near_addend(kind):
        def addend(kb):
            mask = selection(kb)
            return lambda h: bias_ref[h, kind] + mask
        return addend

    _attend_causal_blocks(
        qi, t, B_HEADS, nkb, lhs_block, qlat_ref[0, 0], lambda kb: ckvnt_ref[0, kb],
        far_addend, near_addend(jnp.where(qi >= 1, PREV_BLOCK, NO_BLOCK)), near_addend(DIAG_BLOCK),
        acc_ref, s0_ref, s1_ref, pa_ref, pb_ref)

    for h in range(B_HEADS):
        cols = slice(h * t, (h + 1) * t)
        o_lat = (acc_ref[:B_KV_RANK, cols] * (1.0 / acc_ref[B_KV_RANK:B_KV_RANK + 1, cols])).astype(BF16)
        o = _dot(wuv_ref[h], o_lat)
        o_ref[0, :, h * HEAD_DIM:(h + 1) * HEAD_DIM] = o.T.astype(o_ref.dtype)


def _dsa_attention(qlat, qidx, wis, kk, ckvn, bias, wuv):
    b, s, _ = qlat.shape
    t = min(ATTN_BLOCK, s)
    nkb = s // t
    topk = min(TOPK_MAX, s // 4)
    qlat_t = jnp.transpose(qlat.reshape(b, nkb, t, B_HEADS, B_KV_RANK), (0, 1, 4, 3, 2)).reshape(
        b, nkb, B_KV_RANK, B_HEADS * t)
    qidx_t = jnp.swapaxes(qidx, 1, 2)
    wis_t = jnp.swapaxes(wis, 1, 2)
    ckvn_t = _with_sum_rows(jnp.swapaxes(ckvn.reshape(b, nkb, t, B_KV_RANK), 2, 3))
    ckvn_t = jnp.concatenate([ckvn_t, jnp.zeros_like(ckvn_t[:, :1])], axis=1)
    kern = functools.partial(_dsa_kernel, t=t, topk=topk)
    qcol = lambda bi, qi: (bi, 0, qi)
    allk = lambda bi, qi: (bi, 0, 0)
    return pl.pallas_call(
        kern,
        grid=(b, s // t),
        in_specs=[pl.BlockSpec((1, 1, B_KV_RANK, B_HEADS * t), lambda bi, qi: (bi, qi, 0, 0)),
                  pl.BlockSpec((1, IDX_HEADS * IDX_DIM, t), qcol),
                  pl.BlockSpec((1, IDX_HEADS, t), qcol),
                  pl.BlockSpec((1, s, 128), allk),
                  pl.BlockSpec((1, s, B_KV_RANK), allk),
                  pl.BlockSpec((1, nkb + 1, B_KV_RANK + SUM_ROWS, t), lambda bi, qi: (bi, 0, 0, 0)),
                  pl.BlockSpec((B_HEADS, 3, t, t), lambda bi, qi: (0, 0, 0, 0)),
                  pl.BlockSpec((B_HEADS, HEAD_DIM, B_KV_RANK), lambda bi, qi: (0, 0, 0))],
        out_specs=pl.BlockSpec((1, t, B_WIDTH), lambda bi, qi: (bi, qi, 0)),
        out_shape=jax.ShapeDtypeStruct((b, s, B_WIDTH), BF16),
        scratch_shapes=[pltpu.VMEM((nkb, t, t), jnp.int32),
                        pltpu.VMEM((nkb + 1, t, t), jnp.int16), pltpu.VMEM((nkb + 1, t, t), jnp.int16),
                        pltpu.VMEM((B_KV_RANK + SUM_ROWS, B_HEADS * t), F32),
                        pltpu.VMEM((t, B_HEADS * t), F32), pltpu.VMEM((t, B_HEADS * t), F32),
                        pltpu.VMEM((t, B_HEADS * t), BF16), pltpu.VMEM((t, B_HEADS * t), BF16)],
        compiler_params=_params(("parallel", "arbitrary"), 56),
        name="dsa_attention",
    )(qlat_t, qidx_t, wis_t, kk, ckvn, ckvn_t, bias, wuv)


def _gelu(x):
    return x * (0.5 * (1.0 + jnp.tanh(math.sqrt(2.0 / math.pi) * (x + 0.044715 * (x * x * x)))))


def _gmlp_kernel(cu_ref, cv_ref, g_ref, b_ref, ws_ref, bs_ref, o_ref, *, ts):
    u = _gelu(cu_ref[...])
    v = _gelu(cv_ref[...])
    mu = jnp.mean(v, axis=-1, keepdims=True)
    vc = v - mu
    var = jnp.mean(vc * vc, axis=-1, keepdims=True)
    vn = (vc * lax.rsqrt(var + EPS) * g_ref[...] + b_ref[...]).astype(BF16)
    row = lax.broadcasted_iota(jnp.int32, (CHUNK, CHUNK), 0)
    col = lax.broadcasted_iota(jnp.int32, (CHUNK, CHUNK), 1)
    for g in range(C_GROUPS):
        w = jnp.where(row >= col, ws_ref[g], 0.0).astype(BF16)
        bcol = bs_ref[:, g:g + 1]
        cs = slice(g * 128, (g + 1) * 128)
        for c in range(ts // CHUNK):
            rs = slice(c * CHUNK, (c + 1) * CHUNK)
            y = _dot(w, vn[rs, cs]) + bcol
            o_ref[rs, cs] = (u[rs, cs] * y).astype(o_ref.dtype)


def _gmlp(rest, ln_g, ln_b, w_s, b_s, ts):
    m = rest.shape[0]
    kern = functools.partial(_gmlp_kernel, ts=ts)
    return pl.pallas_call(
        kern,
        grid=(m // ts,),
        in_specs=[pl.BlockSpec((ts, C_WIDTH), lambda i: (i, 0)),
                  pl.BlockSpec((ts, C_WIDTH), lambda i: (i, 1)),
                  pl.BlockSpec((1, C_WIDTH), lambda i: (0, 0)),
                  pl.BlockSpec((1, C_WIDTH), lambda i: (0, 0)),
                  pl.BlockSpec((C_GROUPS, CHUNK, CHUNK), lambda i: (0, 0, 0)),
                  pl.BlockSpec((CHUNK, C_GROUPS), lambda i: (0, 0))],
        out_specs=pl.BlockSpec((ts, C_WIDTH), lambda i: (i, 0)),
        out_shape=jax.ShapeDtypeStruct((m, C_WIDTH), BF16),
        compiler_params=_params(("parallel",), 32),
        name="gmlp",
    )(rest, rest, ln_g.reshape(1, -1), ln_b.reshape(1, -1), w_s, jnp.transpose(b_s))


def _out_proj_kernel(h_ref, oa_ref, ob_ref, oc_ref, wa_ref, wb_ref, wc_ref, g_ref, o_ref):
    y = _dot(oa_ref[...], wa_ref[...]) + _dot(ob_ref[...], wb_ref[...]) + _dot(oc_ref[...], wc_ref[...])
    o_ref[...] = h_ref[...] + _rms(y, g_ref[...])


def _out_proj(h, oa, ob, oc, wa, wb, wc, g, tm):
    m, d = h.shape
    rowblk = lambda i: (i, 0)
    full = lambda i: (0, 0)
    return pl.pallas_call(
        _out_proj_kernel,
        grid=(m // tm,),
        in_specs=[pl.BlockSpec((tm, d), rowblk),
                  pl.BlockSpec((tm, A_WIDTH), rowblk),
                  pl.BlockSpec((tm, B_WIDTH), rowblk),
                  pl.BlockSpec((tm, C_WIDTH), rowblk),
                  pl.BlockSpec((A_WIDTH, d), full),
                  pl.BlockSpec((B_WIDTH, d), full),
                  pl.BlockSpec((C_WIDTH, d), full),
                  pl.BlockSpec((1, d), full)],
        out_specs=pl.BlockSpec((tm, d), rowblk),
        out_shape=jax.ShapeDtypeStruct((m, d), F32),
        compiler_params=_params(("parallel",), 48),
        name="out_proj",
    )(h, oa, ob, oc, wa, wb, wc, g.reshape(1, d))


def _xattn_kernel(h_ref, gpre_ref, wq_ref, k_ref, v_ref, wo_ref, gpost_ref, o_ref):
    x = h_ref[0]
    xn = _rms(x, gpre_ref[...]).astype(BF16)
    q = _dot(xn, wq_ref[...])
    scale = X_DIM ** -0.5
    outs = []
    for hh in range(X_HEADS):
        cs = slice(hh * X_DIM, (hh + 1) * X_DIM)
        s = _dot_nt(q[:, cs].astype(BF16), k_ref[0, :, cs]) * scale
        p = jnp.exp(s - jnp.max(s, axis=-1, keepdims=True))
        p = p * (1.0 / jnp.sum(p, axis=-1, keepdims=True))
        outs.append(_dot(p.astype(BF16), v_ref[0, :, cs]).astype(BF16))
    o = jnp.concatenate(outs, axis=1)
    xa = _dot(o, wo_ref[...])
    o_ref[0] = x + _rms(xa, gpost_ref[...])


def _xattn(h, kv, gpre, wq, wo, gpost, tm):
    b, s, d = h.shape
    mlen = kv.shape[1]
    hw = X_HEADS * X_DIM
    return pl.pallas_call(
        _xattn_kernel,
        grid=(b, s // tm),
        in_specs=[pl.BlockSpec((1, tm, d), lambda bi, i: (bi, i, 0)),
                  pl.BlockSpec((1, d), lambda bi, i: (0, 0)),
                  pl.BlockSpec((d, hw), lambda bi, i: (0, 0)),
                  pl.BlockSpec((1, mlen, hw), lambda bi, i: (bi, 0, 0)),
                  pl.BlockSpec((1, mlen, hw), lambda bi, i: (bi, 0, 1)),
                  pl.BlockSpec((hw, d), lambda bi, i: (0, 0)),
                  pl.BlockSpec((1, d), lambda bi, i: (0, 0))],
        out_specs=pl.BlockSpec((1, tm, d), lambda bi, i: (bi, i, 0)),
        out_shape=jax.ShapeDtypeStruct((b, s, d), F32),
        compiler_params=_params(("parallel", "parallel"), 48),
        name="xattn",
    )(h, gpre.reshape(1, d), wq, kv, kv, wo, gpost.reshape(1, d))


def _ffn_kernel(x_ref, gpre_ref, wg_ref, wu_ref, wd_ref, gpost_ref, o_ref, xn_ref, acc_ref):
    j = pl.program_id(1)

    @pl.when(j == 0)
    def _():
        xn_ref[...] = _rms(x_ref[...], gpre_ref[...]).astype(BF16)
        acc_ref[...] = jnp.zeros(acc_ref.shape, F32)

    xn = xn_ref[...]
    gate = _dot(xn, wg_ref[...])
    up = _dot(xn, wu_ref[...])
    act = gate * (1.0 / (1.0 + jnp.exp(-gate))) * up
    acc_ref[...] += _dot(act.astype(BF16), wd_ref[...])

    @pl.when(j == pl.num_programs(1) - 1)
    def _():
        o_ref[...] = x_ref[...] + _rms(acc_ref[...], gpost_ref[...])


def _ffn(h, gpre, w_gu, w_down, gpost, tm, tf):
    m, d = h.shape
    f = w_down.shape[0]
    nf = f // tf
    return pl.pallas_call(
        _ffn_kernel,
        grid=(m // tm, nf),
        in_specs=[pl.BlockSpec((tm, d), lambda i, j: (i, 0)),
                  pl.BlockSpec((1, d), lambda i, j: (0, 0)),
                  pl.BlockSpec((d, tf), lambda i, j: (0, j)),
                  pl.BlockSpec((d, tf), lambda i, j: (0, nf + j)),
                  pl.BlockSpec((tf, d), lambda i, j: (j, 0)),
                  pl.BlockSpec((1, d), lambda i, j: (0, 0))],
        out_specs=pl.BlockSpec((tm, d), lambda i, j: (i, 0)),
        out_shape=jax.ShapeDtypeStruct((m, d), F32),
        scratch_shapes=[pltpu.VMEM((tm, d), BF16), pltpu.VMEM((tm, d), F32)],
        compiler_params=_params(("parallel", "arbitrary"), 56),
        name="ffn",
    )(h, gpre.reshape(1, d), w_gu, w_gu, w_down, gpost.reshape(1, d))


def _tile(n, pref):
    t = min(pref, n)
    assert n % t == 0
    return t


def kernel(x, mem, rel_bias, mix_pre_g, mix_post_g, w_in, w_out, a_lambda, a_sub_g, b_cq_g, b_ckv_g, b_w_uq, b_w_qidx, b_w_uk, b_w_uv, c_ln_g, c_ln_b, c_w_s, c_b_s, x_pre_g, x_post_g, mem_g, x_wq, x_wkv, x_wo, f_pre_g, f_post_g, f_w_gu, f_w_down):
    bsz, s_len, d = x.shape
    depth = w_in.shape[0]
    tokens = bsz * s_len
    mlen = mem.shape[1]
    t_attn = min(ATTN_BLOCK, s_len)
    bias_a = _bias_tiles(rel_bias[:, :A_HEADS], t_attn)
    bias_b = _bias_tiles(rel_bias[:, A_HEADS:], t_attn)
    sizes = (A_WIDTH, A_WIDTH, A_WIDTH, B_Q_RANK, B_KV_RANK, IDX_DIM, IDX_HEADS, C_WIDTH, C_WIDTH)
    offs = np.concatenate([[0], np.cumsum(sizes)])
    tm = _tile(tokens, 512)

    h = x.reshape(tokens, d)
    mem2 = mem.reshape(bsz * mlen, d)
    for l in range(depth):
        lam_init = 0.8 - 0.6 * math.exp(-0.3 * l)
        wl = w_in[l]
        cols = [wl[:, offs[i]:offs[i + 1]] for i in range(len(sizes))]
        wqa, wka, wva, wcq, wckv, wkidx, wwidx, wcu, wcv = cols
        w_all = jnp.concatenate(
            [wqa * (A_HALF ** -0.5 * LOG2E), wka, wva,
             wcu, wcv, wcq, wckv, wkidx, wkidx, wwidx,
             jnp.zeros((d, 128 - IDX_HEADS), wl.dtype)], axis=1).astype(BF16)

        qkv, rest = _in_proj(h, mix_pre_g[l], w_all, 3 * A_WIDTH, tm)

        lp = a_lambda[l].astype(F32)
        lam = jnp.exp(jnp.sum(lp[0] * lp[1])) - jnp.exp(jnp.sum(lp[2] * lp[3])) + lam_init
        oa = _diff_attention(qkv.reshape(bsz, s_len, 3 * A_WIDTH), lam.reshape(1), bias_a,
                             a_sub_g[l], 1.0 - lam_init)

        wuq = b_w_uq[l].reshape(B_Q_RANK, B_WIDTH).astype(BF16)
        wuk = jnp.transpose(b_w_uk[l], (1, 2, 0)).astype(BF16)
        wqi = b_w_qidx[l].reshape(B_Q_RANK, IDX_HEADS * IDX_DIM).astype(BF16)
        wuv = jnp.transpose(b_w_uv[l], (1, 2, 0)).astype(BF16)
        qlat, qidx, ckvn, kk, wis = _dsa_prep(rest, b_cq_g[l], b_ckv_g[l], wuq, wuk, wqi, tm)
        r3 = lambda a: a.reshape(bsz, s_len, a.shape[-1])
        ob = _dsa_attention(r3(qlat), r3(qidx), r3(wis), r3(kk), r3(ckvn), bias_b, wuv)

        oc = _gmlp(rest, c_ln_g[l], c_ln_b[l], c_w_s[l], c_b_s[l], tm)

        wo_l = w_out[l].astype(BF16)
        h = _out_proj(h, oa.reshape(tokens, A_WIDTH), ob.reshape(tokens, B_WIDTH), oc,
                      wo_l[:A_WIDTH], wo_l[A_WIDTH:A_WIDTH + B_WIDTH], wo_l[A_WIDTH + B_WIDTH:],
                      mix_post_g[l], tm)

        hw = X_HEADS * X_DIM
        kv = _norm_matmul(mem2, mem_g[l], x_wkv[l].reshape(d, 2 * hw).astype(BF16), BF16,
                          _tile(bsz * mlen, 512), 512)
        h = _xattn(h.reshape(bsz, s_len, d), kv.reshape(bsz, mlen, 2 * hw), x_pre_g[l],
                   x_wq[l].reshape(d, hw).astype(BF16), x_wo[l].reshape(hw, d).astype(BF16),
                   x_post_g[l], _tile(s_len, 512)).reshape(tokens, d)

        fh = f_w_down.shape[1]
        h = _ffn(h, f_pre_g[l], f_w_gu[l].reshape(d, 2 * fh).astype(BF16),
                 f_w_down[l].astype(BF16), f_post_g[l], tm, 512)
    return h.reshape(bsz, s_len, d)
```

```python
import functools
import math

import numpy as np
import jax
import jax.numpy as jnp
from jax import lax
from jax.experimental import pallas as pl
from jax.experimental.pallas import tpu as pltpu

F32 = jnp.float32
BF16 = jnp.bfloat16
EPS = 1e-6
NEG = -1e30
INT_MIN = -(2 ** 31)
I16_MIN = -(2 ** 15)

HEAD_DIM = 128
A_HEADS = 6
A_HALF = 64
B_HEADS = 6
B_Q_RANK = 512
B_KV_RANK = 256
IDX_HEADS = 16
IDX_DIM = 64
TOPK_MAX = 256
C_GROUPS = 4
C_WIDTH = 512
CHUNK = 128
X_HEADS = 4
X_DIM = 128
REL_BUCKETS = 32
REL_MAX_DIST = 128
A_WIDTH = A_HEADS * HEAD_DIM
B_WIDTH = B_HEADS * HEAD_DIM

ATTN_BLOCK = 256
MIB = 1024 * 1024


def _params(semantics, vmem_mib):
    return pltpu.CompilerParams(dimension_semantics=semantics,
                                vmem_limit_bytes=vmem_mib * MIB)


def _rms(x, g):
    return x * lax.rsqrt(jnp.mean(x * x, axis=-1, keepdims=True) + EPS) * g


def _dot(a, b):
    return jnp.dot(a, b, preferred_element_type=F32)


def _dot_nt(a, b):
    return lax.dot_general(a, b, (((1,), (1,)), ((), ())), preferred_element_type=F32)


def _norm_matmul_kernel(x_ref, g_ref, w_ref, o_ref, xn_ref):
    @pl.when(pl.program_id(1) == 0)
    def _():
        xn_ref[...] = _rms(x_ref[...], g_ref[...]).astype(BF16)

    o_ref[...] = _dot(xn_ref[...], w_ref[...]).astype(o_ref.dtype)


def _norm_matmul(x, g, w, out_dtype, tm, tn):
    m, d = x.shape
    n = w.shape[1]
    return pl.pallas_call(
        _norm_matmul_kernel,
        grid=(m // tm, n // tn),
        in_specs=[pl.BlockSpec((tm, d), lambda i, j: (i, 0)),
                  pl.BlockSpec((1, d), lambda i, j: (0, 0)),
                  pl.BlockSpec((d, tn), lambda i, j: (0, j))],
        out_specs=pl.BlockSpec((tm, tn), lambda i, j: (i, j)),
        out_shape=jax.ShapeDtypeStruct((m, n), out_dtype),
        scratch_shapes=[pltpu.VMEM((tm, d), BF16)],
        compiler_params=_params(("parallel", "arbitrary"), 48),
        name="norm_matmul",
    )(x, g.reshape(1, d), w)


SUM_ROWS = 16
IN_PROJ_CHUNK = 1024


def _sum_rows(lead, t, dtype):
    first = lax.broadcasted_iota(jnp.int32, lead + (SUM_ROWS, t), len(lead)) == 0
    return jnp.where(first, 1.0, 0.0).astype(dtype)


def _in_proj_kernel(x_ref, g_ref, wqt_ref, wvt_ref, w_ref, qt_ref, k_ref, vt_ref, rest_ref, *, t):
    xn = _rms(x_ref[0], g_ref[...]).astype(BF16)
    tm = xn.shape[0]
    n_k = k_ref.shape[2]
    k_ref[0] = _dot(xn, w_ref[:, :n_k]).astype(k_ref.dtype)
    for c0 in range(0, rest_ref.shape[2], IN_PROJ_CHUNK):
        c1 = min(c0 + IN_PROJ_CHUNK, rest_ref.shape[2])
        rest_ref[0, :, c0:c1] = _dot(xn, w_ref[:, n_k + c0:n_k + c1])
    qt_ref[0] = _dot_nt(wqt_ref[...], xn).astype(qt_ref.dtype)
    vt = _dot_nt(wvt_ref[...], xn)
    heads, dv = vt_ref.shape[1], vt_ref.shape[3] - SUM_ROWS
    for h in range(heads):
        for blk in range(tm // t):
            vt_ref[0, h, blk, :dv, :] = vt[h * dv:(h + 1) * dv, blk * t:(blk + 1) * t].astype(vt_ref.dtype)
    vt_ref[0, :, :, dv:, :] = _sum_rows((heads, tm // t), t, vt_ref.dtype)


def _in_proj(x, g, wqt, wvt, w, t, tm):
    b, s, d = x.shape
    n_rest = w.shape[1] - A_WIDTH
    resident = lambda shape: pl.BlockSpec(shape, lambda bi, i: (0,) * len(shape), pipeline_mode=pl.Buffered(1))
    kern = functools.partial(_in_proj_kernel, t=t)
    return pl.pallas_call(
        kern,
        grid=(b, s // tm),
        in_specs=[pl.BlockSpec((1, tm, d), lambda bi, i: (bi, i, 0)),
                  pl.BlockSpec((1, d), lambda bi, i: (0, 0)),
                  resident((A_WIDTH, d)), resident((A_WIDTH, d)), resident((d, w.shape[1]))],
        out_specs=[pl.BlockSpec((1, A_WIDTH, tm), lambda bi, i: (bi, 0, i)),
                   pl.BlockSpec((1, tm, A_WIDTH), lambda bi, i: (bi, i, 0)),
                   pl.BlockSpec((1, A_HEADS, tm // t, HEAD_DIM + SUM_ROWS, t), lambda bi, i: (bi, 0, i, 0, 0)),
                   pl.BlockSpec((1, tm, n_rest), lambda bi, i: (bi, i, 0))],
        out_shape=[jax.ShapeDtypeStruct((b, A_WIDTH, s), BF16),
                   jax.ShapeDtypeStruct((b, s, A_WIDTH), BF16),
                   jax.ShapeDtypeStruct((b, A_HEADS, s // t, HEAD_DIM + SUM_ROWS, t), BF16),
                   jax.ShapeDtypeStruct((b, s, n_rest), F32)],
        compiler_params=_params(("parallel", "parallel"), 56),
        name="in_proj",
    )(x, g.reshape(1, d), wqt, wvt, w)


LOG2E = math.log2(math.e)


def _accumulate(acc_ref, alpha, vt, p):
    acc_ref[...] = alpha * acc_ref[...] + _dot(vt, p)


def _attend_causal_blocks(qi, t, n_col, lhs_block, rhs, values_at,
                          far_addend, prev_addend, diag_addend,
                          acc_ref, s0_ref, s1_ref, pa_ref, pb_ref):
    nq = n_col * t
    n_far_pairs = qi // 2
    pad = 2 * n_far_pairs - (qi - 1)

    def block_of(j):
        return jnp.clip(j - pad, 0, qi)

    def scores_into(ref, j):
        ref[...] = _dot(lhs_block(block_of(j)), rhs)

    def values(j):
        return values_at(block_of(j))

    def probs_into(p_ref, s_ref, m, addend):
        m_out, alpha_out = [], []
        for c in range(n_col):
            cols = slice(c * t, (c + 1) * t)
            s = s_ref[:, cols]
            if addend is not None:
                s = s + addend(c)
            m_new = jnp.maximum(m[:, cols], jnp.max(s, axis=0, keepdims=True))
            p_ref[:, cols] = jnp.exp2(s - m_new).astype(BF16)
            alpha_out.append(jnp.exp2(m[:, cols] - m_new))
            m_out.append(m_new)
        return jnp.concatenate(m_out, axis=1), jnp.concatenate(alpha_out, axis=1)

    def step(j, carry, addend_a, addend_b):
        m, alpha_late = carry
        _accumulate(acc_ref, alpha_late, values(j - 1), pb_ref[...])
        scores_into(s1_ref, j + 1)
        m, alpha = probs_into(pa_ref, s0_ref, m, addend_a(block_of(j)))
        _accumulate(acc_ref, alpha, values(j), pa_ref[...])
        scores_into(s0_ref, j + 2)
        m, alpha_late = probs_into(pb_ref, s1_ref, m, addend_b(block_of(j + 1)))
        return m, jnp.where(j < pad, 0.0, alpha_late)

    acc_ref[...] = jnp.zeros(acc_ref.shape, F32)
    pb_ref[...] = jnp.zeros(pb_ref.shape, BF16)
    scores_into(s0_ref, 0)
    carry = (jnp.full((1, nq), NEG, F32), jnp.ones((1, nq), F32))
    carry = lax.fori_loop(0, n_far_pairs, lambda i, c: step(2 * i, c, far_addend, far_addend), carry)
    j_near = 2 * n_far_pairs
    _, alpha_late = step(j_near, carry, prev_addend, diag_addend)
    _accumulate(acc_ref, alpha_late, values(j_near + 1), pb_ref[...])


def _bucket_np(dist):
    n = np.maximum(dist, 0)
    max_exact = REL_BUCKETS // 2
    nf = np.maximum(n, 1).astype(np.float64)
    large = max_exact + (np.log(nf / max_exact) / math.log(REL_MAX_DIST / max_exact)
                         * (REL_BUCKETS - max_exact)).astype(np.int32)
    large = np.minimum(large, REL_BUCKETS - 1)
    return np.where(n < max_exact, n, large)


NO_BLOCK, PREV_BLOCK, DIAG_BLOCK = 0, 1, 2


def _bias_tiles(tab, t):
    assert t >= REL_MAX_DIST
    length = 2 * t
    bucket = jnp.asarray(_bucket_np(np.arange(length)))
    by_dist = ((jnp.take(tab, bucket, axis=0) - tab[REL_BUCKETS - 1]) * LOG2E).T

    def skew(v):
        flat = jnp.tile(v, (1, t))[:, :t * (length - 1)]
        return flat.reshape(-1, t, length - 1)[:, :, :t]

    diag = skew(by_dist)
    prev = skew(jnp.roll(by_dist, -t, axis=1))
    key = np.arange(t)[:, None]
    query = np.arange(t)[None, :]
    diag = jnp.where(jnp.asarray(query >= key)[None], diag, NEG)
    return jnp.stack([jnp.full_like(prev, NEG), prev, diag], axis=1).astype(F32)


def _diff_attn_kernel(lam_ref, qt_ref, k_ref, vt_ref, bias_ref, g_ref, o_ref, acc_ref, s0_ref, s1_ref, pa_ref, pb_ref,
                      *, t, out_scale):
    qi = pl.program_id(2)
    qt = qt_ref[0]
    sub = lax.broadcasted_iota(jnp.int32, qt.shape, 0)
    zero = jnp.zeros_like(qt)
    qs = jnp.concatenate([jnp.where(sub < A_HALF, qt, zero),
                          jnp.where(sub >= A_HALF, qt, zero)], axis=1)
    def lhs_block(kb):
        return k_ref[0, pl.ds(pl.multiple_of(kb * t, t), t), :]

    prev_kind = jnp.where(qi >= 1, PREV_BLOCK, NO_BLOCK)
    _attend_causal_blocks(
        qi, t, 2, lhs_block, qs, lambda kb: vt_ref[0, 0, kb],
        lambda kb: None,
        lambda kb: (lambda c: bias_ref[0, prev_kind]),
        lambda kb: (lambda c: bias_ref[0, DIAG_BLOCK]),
        acc_ref, s0_ref, s1_ref, pa_ref, pb_ref)

    acc = acc_ref[...]
    o = acc[:HEAD_DIM] * (1.0 / acc[HEAD_DIM:HEAD_DIM + 1])
    o = o[:, :t] - lam_ref[0] * o[:, t:]
    y = o * lax.rsqrt(jnp.mean(o * o, axis=0, keepdims=True) + EPS) * g_ref[...] * out_scale
    o_ref[0] = y.T.astype(o_ref.dtype)


def _diff_attention(qt, k, vt, lam, bias, sub_g, out_scale):
    b, s, _ = k.shape
    nkb, t = vt.shape[2], vt.shape[4]
    kern = functools.partial(_diff_attn_kernel, t=t, out_scale=out_scale)
    return pl.pallas_call(
        kern,
        grid=(b, A_HEADS, s // t),
        in_specs=[pl.BlockSpec(memory_space=pltpu.SMEM),
                  pl.BlockSpec((1, HEAD_DIM, t), lambda bi, h, qi: (bi, h, qi)),
                  pl.BlockSpec((1, s, HEAD_DIM), lambda bi, h, qi: (bi, 0, h)),
                  pl.BlockSpec((1, 1, nkb, HEAD_DIM + SUM_ROWS, t), lambda bi, h, qi: (bi, h, 0, 0, 0)),
                  pl.BlockSpec((1, 3, t, t), lambda bi, h, qi: (h, 0, 0, 0)),
                  pl.BlockSpec((HEAD_DIM, 1), lambda bi, h, qi: (0, 0))],
        out_specs=pl.BlockSpec((1, t, HEAD_DIM), lambda bi, h, qi: (bi, qi, h)),
        out_shape=jax.ShapeDtypeStruct((b, s, A_WIDTH), BF16),
        scratch_shapes=[pltpu.VMEM((HEAD_DIM + SUM_ROWS, 2 * t), F32),
                        pltpu.VMEM((t, 2 * t), F32), pltpu.VMEM((t, 2 * t), F32),
                        pltpu.VMEM((t, 2 * t), BF16), pltpu.VMEM((t, 2 * t), BF16)],
        compiler_params=_params(("parallel", "parallel", "arbitrary"), 32),
        name="diff_attention",
    )(lam, qt, k, vt, bias, sub_g.reshape(HEAD_DIM, 1))


def _dsa_prep_kernel(cq_ref, ckv_ref, kk_ref, wi_ref, gq_ref, gkv_ref, wuq_ref, wuk_ref, wqit_ref,
                     qlat_ref, qidx_ref, wis_ref, kkb_ref, ckvn_ref, ckvnt_ref, *, t):
    cqn = _rms(cq_ref[0], gq_ref[...]).astype(BF16)
    tm = cqn.shape[0]
    q = _dot(cqn, wuq_ref[...])
    for h in range(B_HEADS):
        qh = q[:, h * HEAD_DIM:(h + 1) * HEAD_DIM].astype(BF16)
        qlat = (_dot_nt(wuk_ref[h], qh) * (HEAD_DIM ** -0.5 * LOG2E)).astype(BF16)
        for blk in range(tm // t):
            qlat_ref[0, blk, :, h * t:(h + 1) * t] = qlat[:, blk * t:(blk + 1) * t]
    qidx_ref[0] = _dot_nt(wqit_ref[...], cqn).astype(BF16)
    wis = wi_ref[0] * (IDX_HEADS ** -0.5 * IDX_DIM ** -0.5)
    wis_ref[0] = wis.T[:IDX_HEADS]
    kkb_ref[0] = kk_ref[0].astype(BF16)
    ckvn = _rms(ckv_ref[0], gkv_ref[...])
    ckvn_ref[0] = ckvn.astype(BF16)
    ckvn_t = ckvn.T.astype(BF16)
    for blk in range(tm // t):
        ckvnt_ref[0, blk, :B_KV_RANK, :] = ckvn_t[:, blk * t:(blk + 1) * t]
    ckvnt_ref[0, :, B_KV_RANK:, :] = _sum_rows((tm // t,), t, BF16)


def _dsa_prep(rest, gq, gkv, wuq, wuk, wqit, t, tm):
    b, s, _ = rest.shape
    col = lambda c: (lambda bi, i: (bi, i, c))
    const = lambda n: (lambda bi, i: (0,) * n)
    kern = functools.partial(_dsa_prep_kernel, t=t)
    return pl.pallas_call(
        kern,
        grid=(b, s // tm),
        in_specs=[pl.BlockSpec((1, tm, B_Q_RANK), col(2)),
                  pl.BlockSpec((1, tm, B_KV_RANK), col(6)),
                  pl.BlockSpec((1, tm, 128), col(14)),
                  pl.BlockSpec((1, tm, 128), col(15)),
                  pl.BlockSpec((1, B_Q_RANK), const(2)),
                  pl.BlockSpec((1, B_KV_RANK), const(2)),
                  pl.BlockSpec((B_Q_RANK, B_WIDTH), const(2)),
                  pl.BlockSpec((B_HEADS, B_KV_RANK, HEAD_DIM), const(3)),
                  pl.BlockSpec((IDX_HEADS * IDX_DIM, B_Q_RANK), const(2))],
        out_specs=[pl.BlockSpec((1, tm // t, B_KV_RANK, B_HEADS * t), lambda bi, i: (bi, i, 0, 0)),
                   pl.BlockSpec((1, IDX_HEADS * IDX_DIM, tm), lambda bi, i: (bi, 0, i)),
                   pl.BlockSpec((1, IDX_HEADS, tm), lambda bi, i: (bi, 0, i)),
                   pl.BlockSpec((1, tm, 128), lambda bi, i: (bi, i, 0)),
                   pl.BlockSpec((1, tm, B_KV_RANK), lambda bi, i: (bi, i, 0)),
                   pl.BlockSpec((1, tm // t, B_KV_RANK + SUM_ROWS, t), lambda bi, i: (bi, i, 0, 0))],
        out_shape=[jax.ShapeDtypeStruct((b, s // t, B_KV_RANK, B_HEADS * t), BF16),
                   jax.ShapeDtypeStruct((b, IDX_HEADS * IDX_DIM, s), BF16),
                   jax.ShapeDtypeStruct((b, IDX_HEADS, s), F32),
                   jax.ShapeDtypeStruct((b, s, 128), BF16),
                   jax.ShapeDtypeStruct((b, s, B_KV_RANK), BF16),
                   jax.ShapeDtypeStruct((b, s // t, B_KV_RANK + SUM_ROWS, t), BF16)],
        compiler_params=_params(("parallel", "parallel"), 40),
        name="dsa_prep",
    )(rest, rest, rest, rest, gq.reshape(1, -1), gkv.reshape(1, -1), wuq, wuk, wqit)


def _dsa_kernel(qlat_ref, qidx_ref, wis_ref, kk_ref, ckvn_ref, ckvnt_ref, bias_ref, wuv_ref, o_ref,
                key_ref, hi_ref, lo_ref, acc_ref, s0_ref, s1_ref, pa_ref, pb_ref, *, t, topk):
    qi = pl.program_id(1)
    nblk = qi + 1
    nkb = key_ref.shape[0]
    key_pos = lax.broadcasted_iota(jnp.int32, (t, t), 0)
    query_pos = lax.broadcasted_iota(jnp.int32, (t, t), 1)

    qidx = qidx_ref[0]
    wis = wis_ref[0]
    sub = lax.broadcasted_iota(jnp.int32, (2 * IDX_DIM, t), 0)
    q_heads = []
    for g in range(IDX_HEADS):
        pair = qidx[(g // 2) * 2 * IDX_DIM:(g // 2 + 1) * 2 * IDX_DIM]
        keep = (sub < IDX_DIM) if g % 2 == 0 else (sub >= IDX_DIM)
        q_heads.append(jnp.where(keep, pair, jnp.zeros_like(pair)))

    def score_block(kb, c):
        ks = pl.multiple_of(kb * t, t)
        kblk = kk_ref[0, pl.ds(ks, t), :]
        score = jnp.zeros((t, t), F32)
        for g in range(IDX_HEADS):
            d = _dot(kblk, q_heads[g])
            score = score + jnp.maximum(d, 0.0) * wis[g:g + 1]
        bits = pltpu.bitcast(score, jnp.int32)
        key = jnp.where(bits < 0, bits ^ jnp.int32(0x7FFFFFFF), bits)
        causal = (kb < qi) | (query_pos >= key_pos)
        key = jnp.where(causal, key, jnp.int32(INT_MIN))
        key_ref[kb] = key
        hi_ref[kb] = (key >> 16).astype(jnp.int16)
        lo_ref[kb] = ((key & 0xFFFF) + I16_MIN).astype(jnp.int16)
        return c

    lax.fori_loop(0, nblk, score_block, 0)

    hi_ref[nkb] = jnp.full((t, t), I16_MIN, jnp.int16)
    lo_ref[nkb] = jnp.full((t, t), I16_MIN, jnp.int16)
    one, zero = jnp.int16(1), jnp.int16(0)

    def count_ge(plane_ref, cand):
        cand_b = jnp.broadcast_to(cand.astype(jnp.int16), (t, t))

        def body(i, c):
            second = jnp.where(2 * i + 1 < nblk, 2 * i + 1, nkb)
            for kb in (2 * i, second):
                x = jnp.where(plane_ref[kb] >= cand_b, one, zero)
                for r in range(t // 16):
                    c = c + x[r * 16:(r + 1) * 16]
            return c

        c = lax.fori_loop(0, (nblk + 1) // 2, body, jnp.zeros((16, t), jnp.int16))
        return jnp.sum(c.astype(jnp.int32), axis=0, keepdims=True)

    def greedy_bits(count_at_least):
        def bit_step(i, v):
            cand = v + jnp.left_shift(jnp.int32(1), 15 - i)
            return jnp.where(count_at_least(cand) >= topk, cand, v)
        return lax.fori_loop(0, 16, bit_step, jnp.full((1, t), I16_MIN, jnp.int32))

    hi = greedy_bits(lambda cand: count_ge(hi_ref, cand))
    above = count_ge(hi_ref, hi + 1)
    hi_b = jnp.broadcast_to(hi.astype(jnp.int16), (t, t))

    def restrict(kb, c):
        lo_ref[kb] = jnp.where(hi_ref[kb] == hi_b, lo_ref[kb], jnp.int16(I16_MIN))
        return c

    lax.fori_loop(0, nblk, restrict, 0)
    lo = greedy_bits(lambda cand: above + count_ge(lo_ref, cand))
    thr = jnp.left_shift(hi, 16) + (lo - I16_MIN)
    thr_b = jnp.broadcast_to(jnp.maximum(thr, jnp.int32(INT_MIN + 1)), (t, t))

    def lhs_block(kb):
        return ckvn_ref[0, pl.ds(pl.multiple_of(kb * t, t), t), :]

    def selection(kb):
        return jnp.where(key_ref[kb] >= thr_b, 0.0, NEG)

    def far_addend(kb):
        mask = selection(kb)
        return lambda h: mask

    def near_addend(kind):
        def addend(kb):
            mask = selection(kb)
            return lambda h: bias_ref[h, kind] + mask
        return addend

    _attend_causal_blocks(
        qi, t, B_HEADS, lhs_block, qlat_ref[0, 0], lambda kb: ckvnt_ref[0, kb],
        far_addend, near_addend(jnp.where(qi >= 1, PREV_BLOCK, NO_BLOCK)), near_addend(DIAG_BLOCK),
        acc_ref, s0_ref, s1_ref, pa_ref, pb_ref)

    for h in range(B_HEADS):
        cols = slice(h * t, (h + 1) * t)
        o_lat = (acc_ref[:B_KV_RANK, cols] * (1.0 / acc_ref[B_KV_RANK:B_KV_RANK + 1, cols])).astype(BF16)
        o = _dot(wuv_ref[h], o_lat)
        o_ref[0, :, h * HEAD_DIM:(h + 1) * HEAD_DIM] = o.T.astype(o_ref.dtype)


def _dsa_attention(qlat_t, qidx_t, wis_t, kk, ckvn, ckvn_t, bias, wuv):
    b, s, _ = ckvn.shape
    nkb, t = ckvn_t.shape[1], ckvn_t.shape[3]
    topk = min(TOPK_MAX, s // 4)
    kern = functools.partial(_dsa_kernel, t=t, topk=topk)
    qcol = lambda bi, qi: (bi, 0, qi)
    allk = lambda bi, qi: (bi, 0, 0)
    return pl.pallas_call(
        kern,
        grid=(b, s // t),
        in_specs=[pl.BlockSpec((1, 1, B_KV_RANK, B_HEADS * t), lambda bi, qi: (bi, qi, 0, 0)),
                  pl.BlockSpec((1, IDX_HEADS * IDX_DIM, t), qcol),
                  pl.BlockSpec((1, IDX_HEADS, t), qcol),
                  pl.BlockSpec((1, s, 128), allk),
                  pl.BlockSpec((1, s, B_KV_RANK), allk),
                  pl.BlockSpec((1, nkb, B_KV_RANK + SUM_ROWS, t), lambda bi, qi: (bi, 0, 0, 0)),
                  pl.BlockSpec((B_HEADS, 3, t, t), lambda bi, qi: (0, 0, 0, 0)),
                  pl.BlockSpec((B_HEADS, HEAD_DIM, B_KV_RANK), lambda bi, qi: (0, 0, 0))],
        out_specs=pl.BlockSpec((1, t, B_WIDTH), lambda bi, qi: (bi, qi, 0)),
        out_shape=jax.ShapeDtypeStruct((b, s, B_WIDTH), BF16),
        scratch_shapes=[pltpu.VMEM((nkb, t, t), jnp.int32),
                        pltpu.VMEM((nkb + 1, t, t), jnp.int16), pltpu.VMEM((nkb + 1, t, t), jnp.int16),
                        pltpu.VMEM((B_KV_RANK + SUM_ROWS, B_HEADS * t), F32),
                        pltpu.VMEM((t, B_HEADS * t), F32), pltpu.VMEM((t, B_HEADS * t), F32),
                        pltpu.VMEM((t, B_HEADS * t), BF16), pltpu.VMEM((t, B_HEADS * t), BF16)],
        compiler_params=_params(("parallel", "arbitrary"), 56),
        name="dsa_attention",
    )(qlat_t, qidx_t, wis_t, kk, ckvn, ckvn_t, bias, wuv)


def _gelu(x):
    return x * (0.5 * (1.0 + jnp.tanh(math.sqrt(2.0 / math.pi) * (x + 0.044715 * (x * x * x)))))


def _gmlp_kernel(cu_ref, cv_ref, g_ref, b_ref, ws_ref, bs_ref, o_ref, *, ts):
    u = _gelu(cu_ref[...])
    v = _gelu(cv_ref[...])
    mu = jnp.mean(v, axis=-1, keepdims=True)
    vc = v - mu
    var = jnp.mean(vc * vc, axis=-1, keepdims=True)
    vn = (vc * lax.rsqrt(var + EPS) * g_ref[...] + b_ref[...]).astype(BF16)
    row = lax.broadcasted_iota(jnp.int32, (CHUNK, CHUNK), 0)
    col = lax.broadcasted_iota(jnp.int32, (CHUNK, CHUNK), 1)
    for g in range(C_GROUPS):
        w = jnp.where(row >= col, ws_ref[g], 0.0).astype(BF16)
        bcol = bs_ref[:, g:g + 1]
        cs = slice(g * 128, (g + 1) * 128)
        for c in range(ts // CHUNK):
            rs = slice(c * CHUNK, (c + 1) * CHUNK)
            y = _dot(w, vn[rs, cs]) + bcol
            o_ref[rs, cs] = (u[rs, cs] * y).astype(o_ref.dtype)


def _gmlp(rest, ln_g, ln_b, w_s, b_s, ts):
    m = rest.shape[0]
    kern = functools.partial(_gmlp_kernel, ts=ts)
    return pl.pallas_call(
        kern,
        grid=(m // ts,),
        in_specs=[pl.BlockSpec((ts, C_WIDTH), lambda i: (i, 0)),
                  pl.BlockSpec((ts, C_WIDTH), lambda i: (i, 1)),
                  pl.BlockSpec((1, C_WIDTH), lambda i: (0, 0)),
                  pl.BlockSpec((1, C_WIDTH), lambda i: (0, 0)),
                  pl.BlockSpec((C_GROUPS, CHUNK, CHUNK), lambda i: (0, 0, 0)),
                  pl.BlockSpec((CHUNK, C_GROUPS), lambda i: (0, 0))],
        out_specs=pl.BlockSpec((ts, C_WIDTH), lambda i: (i, 0)),
        out_shape=jax.ShapeDtypeStruct((m, C_WIDTH), BF16),
        compiler_params=_params(("parallel",), 32),
        name="gmlp",
    )(rest, rest, ln_g.reshape(1, -1), ln_b.reshape(1, -1), w_s, jnp.transpose(b_s))


def _out_proj_kernel(h_ref, oa_ref, ob_ref, oc_ref, wa_ref, wb_ref, wc_ref, g_ref, o_ref):
    y = _dot(oa_ref[...], wa_ref[...]) + _dot(ob_ref[...], wb_ref[...]) + _dot(oc_ref[...], wc_ref[...])
    o_ref[...] = h_ref[...] + _rms(y, g_ref[...])


def _out_proj(h, oa, ob, oc, wa, wb, wc, g, tm):
    m, d = h.shape
    rowblk = lambda i: (i, 0)
    full = lambda i: (0, 0)
    return pl.pallas_call(
        _out_proj_kernel,
        grid=(m // tm,),
        in_specs=[pl.BlockSpec((tm, d), rowblk),
                  pl.BlockSpec((tm, A_WIDTH), rowblk),
                  pl.BlockSpec((tm, B_WIDTH), rowblk),
                  pl.BlockSpec((tm, C_WIDTH), rowblk),
                  pl.BlockSpec((A_WIDTH, d), full),
                  pl.BlockSpec((B_WIDTH, d), full),
                  pl.BlockSpec((C_WIDTH, d), full),
                  pl.BlockSpec((1, d), full)],
        out_specs=pl.BlockSpec((tm, d), rowblk),
        out_shape=jax.ShapeDtypeStruct((m, d), F32),
        compiler_params=_params(("parallel",), 48),
        name="out_proj",
    )(h, oa, ob, oc, wa, wb, wc, g.reshape(1, d))


def _xattn_kernel(h_ref, gpre_ref, wq_ref, k_ref, v_ref, wo_ref, gpost_ref, o_ref):
    x = h_ref[0]
    xn = _rms(x, gpre_ref[...]).astype(BF16)
    q = _dot(xn, wq_ref[...])
    scale = X_DIM ** -0.5
    outs = []
    for hh in range(X_HEADS):
        cs = slice(hh * X_DIM, (hh + 1) * X_DIM)
        s = _dot_nt(q[:, cs].astype(BF16), k_ref[0, :, cs]) * scale
        p = jnp.exp(s - jnp.max(s, axis=-1, keepdims=True))
        p = p * (1.0 / jnp.sum(p, axis=-1, keepdims=True))
        outs.append(_dot(p.astype(BF16), v_ref[0, :, cs]).astype(BF16))
    o = jnp.concatenate(outs, axis=1)
    xa = _dot(o, wo_ref[...])
    o_ref[0] = x + _rms(xa, gpost_ref[...])


def _xattn(h, kv, gpre, wq, wo, gpost, tm):
    b, s, d = h.shape
    mlen = kv.shape[1]
    hw = X_HEADS * X_DIM
    return pl.pallas_call(
        _xattn_kernel,
        grid=(b, s // tm),
        in_specs=[pl.BlockSpec((1, tm, d), lambda bi, i: (bi, i, 0)),
                  pl.BlockSpec((1, d), lambda bi, i: (0, 0)),
                  pl.BlockSpec((d, hw), lambda bi, i: (0, 0)),
                  pl.BlockSpec((1, mlen, hw), lambda bi, i: (bi, 0, 0)),
                  pl.BlockSpec((1, mlen, hw), lambda bi, i: (bi, 0, 1)),
                  pl.BlockSpec((hw, d), lambda bi, i: (0, 0)),
                  pl.BlockSpec((1, d), lambda bi, i: (0, 0))],
        out_specs=pl.BlockSpec((1, tm, d), lambda bi, i: (bi, i, 0)),
        out_shape=jax.ShapeDtypeStruct((b, s, d), F32),
        compiler_params=_params(("parallel", "parallel"), 48),
        name="xattn",
    )(h, gpre.reshape(1, d), wq, kv, kv, wo, gpost.reshape(1, d))


def _ffn_kernel(x_ref, gpre_ref, wg_ref, wu_ref, wd_ref, gpost_ref, o_ref, xn_ref, acc_ref):
    j = pl.program_id(1)

    @pl.when(j == 0)
    def _():
        xn_ref[...] = _rms(x_ref[...], gpre_ref[...]).astype(BF16)
        acc_ref[...] = jnp.zeros(acc_ref.shape, F32)

    xn = xn_ref[...]
    gate = _dot(xn, wg_ref[...])
    up = _dot(xn, wu_ref[...])
    act = gate * (1.0 / (1.0 + jnp.exp(-gate))) * up
    acc_ref[...] += _dot(act.astype(BF16), wd_ref[...])

    @pl.when(j == pl.num_programs(1) - 1)
    def _():
        o_ref[...] = x_ref[...] + _rms(acc_ref[...], gpost_ref[...])


def _ffn(h, gpre, w_gu, w_down, gpost, tm, tf):
    m, d = h.shape
    f = w_down.shape[0]
    nf = f // tf
    return pl.pallas_call(
        _ffn_kernel,
        grid=(m // tm, nf),
        in_specs=[pl.BlockSpec((tm, d), lambda i, j: (i, 0)),
                  pl.BlockSpec((1, d), lambda i, j: (0, 0)),
                  pl.BlockSpec((d, tf), lambda i, j: (0, j)),
                  pl.BlockSpec((d, tf), lambda i, j: (0, nf + j)),
                  pl.BlockSpec((tf, d), lambda i, j: (j, 0)),
                  pl.BlockSpec((1, d), lambda i, j: (0, 0))],
        out_specs=pl.BlockSpec((tm, d), lambda i, j: (i, 0)),
        out_shape=jax.ShapeDtypeStruct((m, d), F32),
        scratch_shapes=[pltpu.VMEM((tm, d), BF16), pltpu.VMEM((tm, d), F32)],
        compiler_params=_params(("parallel", "arbitrary"), 56),
        name="ffn",
    )(h, gpre.reshape(1, d), w_gu, w_gu, w_down, gpost.reshape(1, d))


def _tile(n, pref):
    t = min(pref, n)
    assert n % t == 0
    return t


def kernel(x, mem, rel_bias, mix_pre_g, mix_post_g, w_in, w_out, a_lambda, a_sub_g, b_cq_g, b_ckv_g, b_w_uq, b_w_qidx, b_w_uk, b_w_uv, c_ln_g, c_ln_b, c_w_s, c_b_s, x_pre_g, x_post_g, mem_g, x_wq, x_wkv, x_wo, f_pre_g, f_post_g, f_w_gu, f_w_down):
    bsz, s_len, d = x.shape
    depth = w_in.shape[0]
    tokens = bsz * s_len
    mlen = mem.shape[1]
    t_attn = min(ATTN_BLOCK, s_len)
    bias_a = _bias_tiles(rel_bias[:, :A_HEADS], t_attn)
    bias_b = _bias_tiles(rel_bias[:, A_HEADS:], t_attn)
    sizes = (A_WIDTH, A_WIDTH, A_WIDTH, B_Q_RANK, B_KV_RANK, IDX_DIM, IDX_HEADS, C_WIDTH, C_WIDTH)
    offs = np.concatenate([[0], np.cumsum(sizes)])
    tm = _tile(tokens, 512)

    h = x.reshape(tokens, d)
    mem2 = mem.reshape(bsz * mlen, d)
    for l in range(depth):
        lam_init = 0.8 - 0.6 * math.exp(-0.3 * l)
        wl = w_in[l]
        cols = [wl[:, offs[i]:offs[i + 1]] for i in range(len(sizes))]
        wqa, wka, wva, wcq, wckv, wkidx, wwidx, wcu, wcv = cols
        w_k_rest = jnp.concatenate(
            [wka, wcu, wcv, wcq, wckv, wkidx, wkidx, wwidx,
             jnp.zeros((d, 128 - IDX_HEADS), wl.dtype)], axis=1).astype(BF16)
        wqt = jnp.transpose(wqa * (A_HALF ** -0.5 * LOG2E)).astype(BF16)
        wvt = jnp.transpose(wva).astype(BF16)

        qt, k, vt, rest = _in_proj(h.reshape(bsz, s_len, d), mix_pre_g[l], wqt, wvt, w_k_rest,
                                   t_attn, _tile(s_len, 512))

        lp = a_lambda[l].astype(F32)
        lam = jnp.exp(jnp.sum(lp[0] * lp[1])) - jnp.exp(jnp.sum(lp[2] * lp[3])) + lam_init
        oa = _diff_attention(qt, k, vt, lam.reshape(1), bias_a, a_sub_g[l], 1.0 - lam_init)

        wuq = b_w_uq[l].reshape(B_Q_RANK, B_WIDTH).astype(BF16)
        wuk = jnp.transpose(b_w_uk[l], (1, 0, 2)).astype(BF16)
        wqit = jnp.transpose(b_w_qidx[l].reshape(B_Q_RANK, IDX_HEADS * IDX_DIM)).astype(BF16)
        wuv = jnp.transpose(b_w_uv[l], (1, 2, 0)).astype(BF16)
        prep = _dsa_prep(rest, b_cq_g[l], b_ckv_g[l], wuq, wuk, wqit, t_attn, _tile(s_len, 512))
        ob = _dsa_attention(*prep, bias_b, wuv)

        rest = rest.reshape(tokens, rest.shape[-1])
        oc = _gmlp(rest, c_ln_g[l], c_ln_b[l], c_w_s[l], c_b_s[l], tm)

        wo_l = w_out[l].astype(BF16)
        h = _out_proj(h, oa.reshape(tokens, A_WIDTH), ob.reshape(tokens, B_WIDTH), oc,
                      wo_l[:A_WIDTH], wo_l[A_WIDTH:A_WIDTH + B_WIDTH], wo_l[A_WIDTH + B_WIDTH:],
                      mix_post_g[l], tm)

        hw = X_HEADS * X_DIM
        kv = _norm_matmul(mem2, mem_g[l], x_wkv[l].reshape(d, 2 * hw).astype(BF16), BF16,
                          _tile(bsz * mlen, 512), 512)
        h = _xattn(h.reshape(bsz, s_len, d), kv.reshape(bsz, mlen, 2 * hw), x_pre_g[l],
                   x_wq[l].reshape(d, hw).astype(BF16), x_wo[l].reshape(hw, d).astype(BF16),
                   x_post_g[l], _tile(s_len, 512)).reshape(tokens, d)

        fh = f_w_down.shape[1]
        h = _ffn(h, f_pre_g[l], f_w_gu[l].reshape(d, 2 * fh).astype(BF16),
                 f_w_down[l].astype(BF16), f_post_g[l], tm, 512)
    return h.reshape(bsz, s_len, d)
```

```python
import functools
import math

import numpy as np
import jax
import jax.numpy as jnp
from jax import lax
from jax.experimental import pallas as pl
from jax.experimental.pallas import tpu as pltpu

F32 = jnp.float32
BF16 = jnp.bfloat16
EPS = 1e-6
NEG = -1e30
INT_MIN = -(2 ** 31)
I16_MIN = -(2 ** 15)

HEAD_DIM = 128
A_HEADS = 6
A_HALF = 64
B_HEADS = 6
B_Q_RANK = 512
B_KV_RANK = 256
IDX_HEADS = 16
IDX_DIM = 64
TOPK_MAX = 256
C_GROUPS = 4
C_WIDTH = 512
CHUNK = 128
X_HEADS = 4
X_DIM = 128
REL_BUCKETS = 32
REL_MAX_DIST = 128
A_WIDTH = A_HEADS * HEAD_DIM
B_WIDTH = B_HEADS * HEAD_DIM

ATTN_BLOCK = 256
SCAN_UNROLL = 2
MIB = 1024 * 1024


def _params(semantics, vmem_mib):
    return pltpu.CompilerParams(dimension_semantics=semantics,
                                vmem_limit_bytes=vmem_mib * MIB)


def _rms(x, g):
    return x * lax.rsqrt(jnp.mean(x * x, axis=-1, keepdims=True) + EPS) * g


def _dot(a, b):
    return jnp.dot(a, b, preferred_element_type=F32)


def _dot_nt(a, b):
    return lax.dot_general(a, b, (((1,), (1,)), ((), ())), preferred_element_type=F32)


def _norm_matmul_kernel(x_ref, g_ref, w_ref, o_ref, xn_ref):
    @pl.when(pl.program_id(1) == 0)
    def _():
        xn_ref[...] = _rms(x_ref[...], g_ref[...]).astype(BF16)

    o_ref[...] = _dot(xn_ref[...], w_ref[...]).astype(o_ref.dtype)


def _norm_matmul(x, g, w, out_dtype, tm, tn):
    m, d = x.shape
    n = w.shape[1]
    return pl.pallas_call(
        _norm_matmul_kernel,
        grid=(m // tm, n // tn),
        in_specs=[pl.BlockSpec((tm, d), lambda i, j: (i, 0)),
                  pl.BlockSpec((1, d), lambda i, j: (0, 0)),
                  pl.BlockSpec((d, tn), lambda i, j: (0, j))],
        out_specs=pl.BlockSpec((tm, tn), lambda i, j: (i, j)),
        out_shape=jax.ShapeDtypeStruct((m, n), out_dtype),
        scratch_shapes=[pltpu.VMEM((tm, d), BF16)],
        compiler_params=_params(("parallel", "arbitrary"), 48),
        name="norm_matmul",
    )(x, g.reshape(1, d), w)


SUM_ROWS = 16
IN_PROJ_CHUNK = 1024


def _sum_rows(lead, t, dtype):
    first = lax.broadcasted_iota(jnp.int32, lead + (SUM_ROWS, t), len(lead)) == 0
    return jnp.where(first, 1.0, 0.0).astype(dtype)


def _in_proj_kernel(x_ref, g_ref, wqt_ref, wvt_ref, w_ref, qt_ref, k_ref, vt_ref, rest_ref, *, t):
    xn = _rms(x_ref[0], g_ref[...]).astype(BF16)
    tm = xn.shape[0]
    n_k = k_ref.shape[2]
    k_ref[0] = _dot(xn, w_ref[:, :n_k]).astype(k_ref.dtype)
    for c0 in range(0, rest_ref.shape[2], IN_PROJ_CHUNK):
        c1 = min(c0 + IN_PROJ_CHUNK, rest_ref.shape[2])
        rest_ref[0, :, c0:c1] = _dot(xn, w_ref[:, n_k + c0:n_k + c1])
    qt_ref[0] = _dot_nt(wqt_ref[...], xn).astype(qt_ref.dtype)
    vt = _dot_nt(wvt_ref[...], xn)
    heads, dv = vt_ref.shape[1], vt_ref.shape[3] - SUM_ROWS
    for h in range(heads):
        for blk in range(tm // t):
            vt_ref[0, h, blk, :dv, :] = vt[h * dv:(h + 1) * dv, blk * t:(blk + 1) * t].astype(vt_ref.dtype)
    vt_ref[0, :, :, dv:, :] = _sum_rows((heads, tm // t), t, vt_ref.dtype)


def _in_proj(x, g, wqt, wvt, w, t, tm):
    b, s, d = x.shape
    n_rest = w.shape[1] - A_WIDTH
    resident = lambda shape: pl.BlockSpec(shape, lambda bi, i: (0,) * len(shape), pipeline_mode=pl.Buffered(1))
    kern = functools.partial(_in_proj_kernel, t=t)
    return pl.pallas_call(
        kern,
        grid=(b, s // tm),
        in_specs=[pl.BlockSpec((1, tm, d), lambda bi, i: (bi, i, 0)),
                  pl.BlockSpec((1, d), lambda bi, i: (0, 0)),
                  resident((A_WIDTH, d)), resident((A_WIDTH, d)), resident((d, w.shape[1]))],
        out_specs=[pl.BlockSpec((1, A_WIDTH, tm), lambda bi, i: (bi, 0, i)),
                   pl.BlockSpec((1, tm, A_WIDTH), lambda bi, i: (bi, i, 0)),
                   pl.BlockSpec((1, A_HEADS, tm // t, HEAD_DIM + SUM_ROWS, t), lambda bi, i: (bi, 0, i, 0, 0)),
                   pl.BlockSpec((1, tm, n_rest), lambda bi, i: (bi, i, 0))],
        out_shape=[jax.ShapeDtypeStruct((b, A_WIDTH, s), BF16),
                   jax.ShapeDtypeStruct((b, s, A_WIDTH), BF16),
                   jax.ShapeDtypeStruct((b, A_HEADS, s // t, HEAD_DIM + SUM_ROWS, t), BF16),
                   jax.ShapeDtypeStruct((b, s, n_rest), F32)],
        compiler_params=_params(("parallel", "parallel"), 56),
        name="in_proj",
    )(x, g.reshape(1, d), wqt, wvt, w)


LOG2E = math.log2(math.e)


def _accumulate(acc_ref, alpha, vt, p):
    acc_ref[...] = alpha * acc_ref[...] + _dot(vt, p)


def _attend_causal_blocks(qi, t, n_col, lhs_block, rhs, values_at,
                          far_addend, prev_addend, diag_addend,
                          acc_ref, s0_ref, s1_ref, pa_ref, pb_ref):
    nq = n_col * t
    n_far_pairs = qi // 2
    pad = 2 * n_far_pairs - (qi - 1)

    def block_of(j):
        return jnp.clip(j - pad, 0, qi)

    def scores_into(ref, j):
        ref[...] = _dot(lhs_block(block_of(j)), rhs)

    def values(j):
        return values_at(block_of(j))

    def probs_into(p_ref, s_ref, m, addend):
        m_out, alpha_out = [], []
        for c in range(n_col):
            cols = slice(c * t, (c + 1) * t)
            s = s_ref[:, cols]
            if addend is not None:
                s = s + addend(c)
            m_new = jnp.maximum(m[:, cols], jnp.max(s, axis=0, keepdims=True))
            p_ref[:, cols] = jnp.exp2(s - m_new).astype(BF16)
            alpha_out.append(jnp.exp2(m[:, cols] - m_new))
            m_out.append(m_new)
        return jnp.concatenate(m_out, axis=1), jnp.concatenate(alpha_out, axis=1)

    def step(j, carry, addend_a, addend_b):
        m, alpha_late = carry
        _accumulate(acc_ref, alpha_late, values(j - 1), pb_ref[...])
        scores_into(s1_ref, j + 1)
        m, alpha = probs_into(pa_ref, s0_ref, m, addend_a(block_of(j)))
        _accumulate(acc_ref, alpha, values(j), pa_ref[...])
        scores_into(s0_ref, j + 2)
        m, alpha_late = probs_into(pb_ref, s1_ref, m, addend_b(block_of(j + 1)))
        return m, jnp.where(j < pad, 0.0, alpha_late)

    acc_ref[...] = jnp.zeros(acc_ref.shape, F32)
    pb_ref[...] = jnp.zeros(pb_ref.shape, BF16)
    scores_into(s0_ref, 0)
    carry = (jnp.full((1, nq), NEG, F32), jnp.ones((1, nq), F32))
    carry = lax.fori_loop(0, n_far_pairs, lambda i, c: step(2 * i, c, far_addend, far_addend), carry)
    j_near = 2 * n_far_pairs
    _, alpha_late = step(j_near, carry, prev_addend, diag_addend)
    _accumulate(acc_ref, alpha_late, values(j_near + 1), pb_ref[...])


def _bucket_np(dist):
    n = np.maximum(dist, 0)
    max_exact = REL_BUCKETS // 2
    nf = np.maximum(n, 1).astype(np.float64)
    large = max_exact + (np.log(nf / max_exact) / math.log(REL_MAX_DIST / max_exact)
                         * (REL_BUCKETS - max_exact)).astype(np.int32)
    large = np.minimum(large, REL_BUCKETS - 1)
    return np.where(n < max_exact, n, large)


NO_BLOCK, PREV_BLOCK, DIAG_BLOCK = 0, 1, 2


def _bias_tiles(tab, t):
    assert t >= REL_MAX_DIST
    length = 2 * t
    bucket = jnp.asarray(_bucket_np(np.arange(length)))
    by_dist = ((jnp.take(tab, bucket, axis=0) - tab[REL_BUCKETS - 1]) * LOG2E).T

    def skew(v):
        flat = jnp.tile(v, (1, t))[:, :t * (length - 1)]
        return flat.reshape(-1, t, length - 1)[:, :, :t]

    diag = skew(by_dist)
    prev = skew(jnp.roll(by_dist, -t, axis=1))
    key = np.arange(t)[:, None]
    query = np.arange(t)[None, :]
    diag = jnp.where(jnp.asarray(query >= key)[None], diag, NEG)
    return jnp.stack([jnp.full_like(prev, NEG), prev, diag], axis=1).astype(F32)


def _diff_attn_kernel(lam_ref, qt_ref, k_ref, vt_ref, bias_ref, g_ref, o_ref, acc_ref, s0_ref, s1_ref, pa_ref, pb_ref,
                      *, t, out_scale):
    qi = pl.program_id(2)
    qt = qt_ref[0]
    sub = lax.broadcasted_iota(jnp.int32, qt.shape, 0)
    zero = jnp.zeros_like(qt)
    qs = jnp.concatenate([jnp.where(sub < A_HALF, qt, zero),
                          jnp.where(sub >= A_HALF, qt, zero)], axis=1)
    def lhs_block(kb):
        return k_ref[0, pl.ds(pl.multiple_of(kb * t, t), t), :]

    prev_kind = jnp.where(qi >= 1, PREV_BLOCK, NO_BLOCK)
    _attend_causal_blocks(
        qi, t, 2, lhs_block, qs, lambda kb: vt_ref[0, 0, kb],
        lambda kb: None,
        lambda kb: (lambda c: bias_ref[0, prev_kind]),
        lambda kb: (lambda c: bias_ref[0, DIAG_BLOCK]),
        acc_ref, s0_ref, s1_ref, pa_ref, pb_ref)

    acc = acc_ref[...]
    o = acc[:HEAD_DIM] * (1.0 / acc[HEAD_DIM:HEAD_DIM + 1])
    o = o[:, :t] - lam_ref[0] * o[:, t:]
    y = o * lax.rsqrt(jnp.mean(o * o, axis=0, keepdims=True) + EPS) * g_ref[...] * out_scale
    o_ref[0] = y.T.astype(o_ref.dtype)


def _diff_attention(qt, k, vt, lam, bias, sub_g, out_scale):
    b, s, _ = k.shape
    nkb, t = vt.shape[2], vt.shape[4]
    kern = functools.partial(_diff_attn_kernel, t=t, out_scale=out_scale)
    return pl.pallas_call(
        kern,
        grid=(b, A_HEADS, s // t),
        in_specs=[pl.BlockSpec(memory_space=pltpu.SMEM),
                  pl.BlockSpec((1, HEAD_DIM, t), lambda bi, h, qi: (bi, h, qi)),
                  pl.BlockSpec((1, s, HEAD_DIM), lambda bi, h, qi: (bi, 0, h)),
                  pl.BlockSpec((1, 1, nkb, HEAD_DIM + SUM_ROWS, t), lambda bi, h, qi: (bi, h, 0, 0, 0)),
                  pl.BlockSpec((1, 3, t, t), lambda bi, h, qi: (h, 0, 0, 0)),
                  pl.BlockSpec((HEAD_DIM, 1), lambda bi, h, qi: (0, 0))],
        out_specs=pl.BlockSpec((1, t, HEAD_DIM), lambda bi, h, qi: (bi, qi, h)),
        out_shape=jax.ShapeDtypeStruct((b, s, A_WIDTH), BF16),
        scratch_shapes=[pltpu.VMEM((HEAD_DIM + SUM_ROWS, 2 * t), F32),
                        pltpu.VMEM((t, 2 * t), F32), pltpu.VMEM((t, 2 * t), F32),
                        pltpu.VMEM((t, 2 * t), BF16), pltpu.VMEM((t, 2 * t), BF16)],
        compiler_params=_params(("parallel", "parallel", "arbitrary"), 32),
        name="diff_attention",
    )(lam, qt, k, vt, bias, sub_g.reshape(HEAD_DIM, 1))


def _dsa_prep_kernel(cq_ref, ckv_ref, kk_ref, wi_ref, gq_ref, gkv_ref, wuq_ref, wuk_ref, wqit_ref,
                     qlat_ref, qidx_ref, wis_ref, kkb_ref, ckvn_ref, ckvnt_ref, *, t):
    cqn = _rms(cq_ref[0], gq_ref[...]).astype(BF16)
    tm = cqn.shape[0]
    q = _dot(cqn, wuq_ref[...])
    for h in range(B_HEADS):
        qh = q[:, h * HEAD_DIM:(h + 1) * HEAD_DIM].astype(BF16)
        qlat = (_dot_nt(wuk_ref[h], qh) * (HEAD_DIM ** -0.5 * LOG2E)).astype(BF16)
        for blk in range(tm // t):
            qlat_ref[0, blk, :, h * t:(h + 1) * t] = qlat[:, blk * t:(blk + 1) * t]
    qidx_ref[0] = _dot_nt(wqit_ref[...], cqn).astype(BF16)
    wis = wi_ref[0] * (IDX_HEADS ** -0.5 * IDX_DIM ** -0.5)
    wis_ref[0] = wis.T[:IDX_HEADS]
    kkb_ref[0] = kk_ref[0].astype(BF16)
    ckvn = _rms(ckv_ref[0], gkv_ref[...])
    ckvn_ref[0] = ckvn.astype(BF16)
    ckvn_t = ckvn.T.astype(BF16)
    for blk in range(tm // t):
        ckvnt_ref[0, blk, :B_KV_RANK, :] = ckvn_t[:, blk * t:(blk + 1) * t]
    ckvnt_ref[0, :, B_KV_RANK:, :] = _sum_rows((tm // t,), t, BF16)


def _dsa_prep(rest, gq, gkv, wuq, wuk, wqit, t, tm):
    b, s, _ = rest.shape
    col = lambda c: (lambda bi, i: (bi, i, c))
    const = lambda n: (lambda bi, i: (0,) * n)
    kern = functools.partial(_dsa_prep_kernel, t=t)
    return pl.pallas_call(
        kern,
        grid=(b, s // tm),
        in_specs=[pl.BlockSpec((1, tm, B_Q_RANK), col(2)),
                  pl.BlockSpec((1, tm, B_KV_RANK), col(6)),
                  pl.BlockSpec((1, tm, 128), col(14)),
                  pl.BlockSpec((1, tm, 128), col(15)),
                  pl.BlockSpec((1, B_Q_RANK), const(2)),
                  pl.BlockSpec((1, B_KV_RANK), const(2)),
                  pl.BlockSpec((B_Q_RANK, B_WIDTH), const(2)),
                  pl.BlockSpec((B_HEADS, B_KV_RANK, HEAD_DIM), const(3)),
                  pl.BlockSpec((IDX_HEADS * IDX_DIM, B_Q_RANK), const(2))],
        out_specs=[pl.BlockSpec((1, tm // t, B_KV_RANK, B_HEADS * t), lambda bi, i: (bi, i, 0, 0)),
                   pl.BlockSpec((1, IDX_HEADS * IDX_DIM, tm), lambda bi, i: (bi, 0, i)),
                   pl.BlockSpec((1, IDX_HEADS, tm), lambda bi, i: (bi, 0, i)),
                   pl.BlockSpec((1, tm, 128), lambda bi, i: (bi, i, 0)),
                   pl.BlockSpec((1, tm, B_KV_RANK), lambda bi, i: (bi, i, 0)),
                   pl.BlockSpec((1, tm // t, B_KV_RANK + SUM_ROWS, t), lambda bi, i: (bi, i, 0, 0))],
        out_shape=[jax.ShapeDtypeStruct((b, s // t, B_KV_RANK, B_HEADS * t), BF16),
                   jax.ShapeDtypeStruct((b, IDX_HEADS * IDX_DIM, s), BF16),
                   jax.ShapeDtypeStruct((b, IDX_HEADS, s), F32),
                   jax.ShapeDtypeStruct((b, s, 128), BF16),
                   jax.ShapeDtypeStruct((b, s, B_KV_RANK), BF16),
                   jax.ShapeDtypeStruct((b, s // t, B_KV_RANK + SUM_ROWS, t), BF16)],
        compiler_params=_params(("parallel", "parallel"), 40),
        name="dsa_prep",
    )(rest, rest, rest, rest, gq.reshape(1, -1), gkv.reshape(1, -1), wuq, wuk, wqit)


def _dsa_kernel(qlat_ref, qidx_ref, wis_ref, kk_ref, ckvn_ref, ckvnt_ref, bias_ref, wuv_ref, o_ref,
                key_ref, hi_ref, lo_ref, acc_ref, s0_ref, s1_ref, pa_ref, pb_ref, *, t, topk):
    qi = pl.program_id(1)
    nblk = qi + 1
    nkb = key_ref.shape[0]
    key_pos = lax.broadcasted_iota(jnp.int32, (t, t), 0)
    query_pos = lax.broadcasted_iota(jnp.int32, (t, t), 1)

    qidx = qidx_ref[0]
    wis = wis_ref[0]
    sub = lax.broadcasted_iota(jnp.int32, (2 * IDX_DIM, t), 0)
    q_heads = []
    for g in range(IDX_HEADS):
        pair = qidx[(g // 2) * 2 * IDX_DIM:(g // 2 + 1) * 2 * IDX_DIM]
        keep = (sub < IDX_DIM) if g % 2 == 0 else (sub >= IDX_DIM)
        q_heads.append(jnp.where(keep, pair, jnp.zeros_like(pair)))

    def score_block(kb, c):
        ks = pl.multiple_of(kb * t, t)
        kblk = kk_ref[0, pl.ds(ks, t), :]
        score = jnp.zeros((t, t), F32)
        for g in range(IDX_HEADS):
            d = _dot(kblk, q_heads[g])
            score = score + jnp.maximum(d, 0.0) * wis[g:g + 1]
        bits = pltpu.bitcast(score, jnp.int32)
        key = jnp.where(bits < 0, bits ^ jnp.int32(0x7FFFFFFF), bits)
        causal = (kb < qi) | (query_pos >= key_pos)
        key = jnp.where(causal, key, jnp.int32(INT_MIN))
        key_ref[kb] = key
        hi_ref[kb] = (key >> 16).astype(jnp.int16)
        lo_ref[kb] = ((key & 0xFFFF) + I16_MIN).astype(jnp.int16)
        return c

    lax.fori_loop(0, nblk, score_block, 0)

    hi_ref[nkb] = jnp.full((t, t), I16_MIN, jnp.int16)
    lo_ref[nkb] = jnp.full((t, t), I16_MIN, jnp.int16)
    one, zero = jnp.int16(1), jnp.int16(0)

    def count_ge(plane_ref, cand):
        cand_b = jnp.broadcast_to(cand.astype(jnp.int16), (t, t))

        def body(i, c):
            for u in range(SCAN_UNROLL):
                kb = SCAN_UNROLL * i + u
                x = jnp.where(plane_ref[jnp.where(kb < nblk, kb, nkb)] >= cand_b, one, zero)
                parts = [x[r * 16:(r + 1) * 16] for r in range(t // 16)]
                while len(parts) > 1:
                    parts = [a + b for a, b in zip(parts[0::2], parts[1::2])]
                c = c + parts[0]
            return c

        trips = (nblk + SCAN_UNROLL - 1) // SCAN_UNROLL
        c = lax.fori_loop(0, trips, body, jnp.zeros((16, t), jnp.int16))
        return jnp.sum(c.astype(jnp.int32), axis=0, keepdims=True)

    def greedy_bits(count_at_least):
        def bit_step(i, v):
            cand = v + jnp.left_shift(jnp.int32(1), 15 - i)
            return jnp.where(count_at_least(cand) >= topk, cand, v)
        return lax.fori_loop(0, 16, bit_step, jnp.full((1, t), I16_MIN, jnp.int32))

    hi = greedy_bits(lambda cand: count_ge(hi_ref, cand))
    above = count_ge(hi_ref, hi + 1)
    hi_b = jnp.broadcast_to(hi.astype(jnp.int16), (t, t))

    def restrict(kb, c):
        lo_ref[kb] = jnp.where(hi_ref[kb] == hi_b, lo_ref[kb], jnp.int16(I16_MIN))
        return c

    lax.fori_loop(0, nblk, restrict, 0)
    lo = greedy_bits(lambda cand: above + count_ge(lo_ref, cand))
    thr = jnp.left_shift(hi, 16) + (lo - I16_MIN)
    thr_b = jnp.broadcast_to(jnp.maximum(thr, jnp.int32(INT_MIN + 1)), (t, t))

    def count_keys(pred):
        def body(kb, c):
            x = jnp.where(pred(key_ref[kb], kb * t), 1, 0)
            return c + jnp.sum(x.reshape(t // 8, 8, t), axis=0)
        c = lax.fori_loop(0, nblk, body, jnp.zeros((8, t), jnp.int32))
        return jnp.sum(c, axis=0, keepdims=True)

    selected = count_keys(lambda key, base: key >= thr_b)

    @pl.when(jnp.max(selected) > topk)
    def _():
        need = topk - count_keys(lambda key, base: key > thr_b)

        def index_bit(i, cut):
            cand = cut + jnp.left_shift(jnp.int32(1), index_bits - 1 - i)
            cand_b = jnp.broadcast_to(cand, (t, t))
            below = count_keys(lambda key, base: (key == thr_b) & (key_pos + base < cand_b))
            return jnp.where(below < need, cand, cut)

        index_bits = max(1, (nkb * t - 1).bit_length())
        cut = lax.fori_loop(0, index_bits, index_bit, jnp.zeros((1, t), jnp.int32))
        cut_b = jnp.broadcast_to(cut, (t, t))

        def demote(kb, c):
            key = key_ref[kb]
            key_ref[kb] = jnp.where((key == thr_b) & (key_pos + kb * t > cut_b), key - 1, key)
            return c

        lax.fori_loop(0, nblk, demote, 0)

    def lhs_block(kb):
        return ckvn_ref[0, pl.ds(pl.multiple_of(kb * t, t), t), :]

    def selection(kb):
        return jnp.where(key_ref[kb] >= thr_b, 0.0, NEG)

    def far_addend(kb):
        mask = selection(kb)
        return lambda h: mask

    def near_addend(kind):
        def addend(kb):
            mask = selection(kb)
            return lambda h: bias_ref[h, kind] + mask
        return addend

    _attend_causal_blocks(
        qi, t, B_HEADS, lhs_block, qlat_ref[0, 0], lambda kb: ckvnt_ref[0, kb],
        far_addend, near_addend(jnp.where(qi >= 1, PREV_BLOCK, NO_BLOCK)), near_addend(DIAG_BLOCK),
        acc_ref, s0_ref, s1_ref, pa_ref, pb_ref)

    for h in range(B_HEADS):
        cols = slice(h * t, (h + 1) * t)
        o_lat = (acc_ref[:B_KV_RANK, cols] * (1.0 / acc_ref[B_KV_RANK:B_KV_RANK + 1, cols])).astype(BF16)
        o = _dot(wuv_ref[h], o_lat)
        o_ref[0, :, h * HEAD_DIM:(h + 1) * HEAD_DIM] = o.T.astype(o_ref.dtype)


def _dsa_attention(qlat_t, qidx_t, wis_t, kk, ckvn, ckvn_t, bias, wuv):
    b, s, _ = ckvn.shape
    nkb, t = ckvn_t.shape[1], ckvn_t.shape[3]
    topk = min(TOPK_MAX, s // 4)
    kern = functools.partial(_dsa_kernel, t=t, topk=topk)
    qcol = lambda bi, qi: (bi, 0, qi)
    allk = lambda bi, qi: (bi, 0, 0)
    return pl.pallas_call(
        kern,
        grid=(b, s // t),
        in_specs=[pl.BlockSpec((1, 1, B_KV_RANK, B_HEADS * t), lambda bi, qi: (bi, qi, 0, 0)),
                  pl.BlockSpec((1, IDX_HEADS * IDX_DIM, t), qcol),
                  pl.BlockSpec((1, IDX_HEADS, t), qcol),
                  pl.BlockSpec((1, s, 128), allk),
                  pl.BlockSpec((1, s, B_KV_RANK), allk),
                  pl.BlockSpec((1, nkb, B_KV_RANK + SUM_ROWS, t), lambda bi, qi: (bi, 0, 0, 0)),
                  pl.BlockSpec((B_HEADS, 3, t, t), lambda bi, qi: (0, 0, 0, 0)),
                  pl.BlockSpec((B_HEADS, HEAD_DIM, B_KV_RANK), lambda bi, qi: (0, 0, 0))],
        out_specs=pl.BlockSpec((1, t, B_WIDTH), lambda bi, qi: (bi, qi, 0)),
        out_shape=jax.ShapeDtypeStruct((b, s, B_WIDTH), BF16),
        scratch_shapes=[pltpu.VMEM((nkb, t, t), jnp.int32),
                        pltpu.VMEM((nkb + 1, t, t), jnp.int16), pltpu.VMEM((nkb + 1, t, t), jnp.int16),
                        pltpu.VMEM((B_KV_RANK + SUM_ROWS, B_HEADS * t), F32),
                        pltpu.VMEM((t, B_HEADS * t), F32), pltpu.VMEM((t, B_HEADS * t), F32),
                        pltpu.VMEM((t, B_HEADS * t), BF16), pltpu.VMEM((t, B_HEADS * t), BF16)],
        compiler_params=_params(("parallel", "arbitrary"), 56),
        name="dsa_attention",
    )(qlat_t, qidx_t, wis_t, kk, ckvn, ckvn_t, bias, wuv)


def _gelu(x):
    return x * (0.5 * (1.0 + jnp.tanh(math.sqrt(2.0 / math.pi) * (x + 0.044715 * (x * x * x)))))


def _gmlp_kernel(cu_ref, cv_ref, g_ref, b_ref, ws_ref, bs_ref, o_ref, *, ts):
    u = _gelu(cu_ref[...])
    v = _gelu(cv_ref[...])
    mu = jnp.mean(v, axis=-1, keepdims=True)
    vc = v - mu
    var = jnp.mean(vc * vc, axis=-1, keepdims=True)
    vn = (vc * lax.rsqrt(var + EPS) * g_ref[...] + b_ref[...]).astype(BF16)
    row = lax.broadcasted_iota(jnp.int32, (CHUNK, CHUNK), 0)
    col = lax.broadcasted_iota(jnp.int32, (CHUNK, CHUNK), 1)
    for g in range(C_GROUPS):
        w = jnp.where(row >= col, ws_ref[g], 0.0).astype(BF16)
        bcol = bs_ref[:, g:g + 1]
        cs = slice(g * 128, (g + 1) * 128)
        for c in range(ts // CHUNK):
            rs = slice(c * CHUNK, (c + 1) * CHUNK)
            y = _dot(w, vn[rs, cs]) + bcol
            o_ref[rs, cs] = (u[rs, cs] * y).astype(o_ref.dtype)


def _gmlp(rest, ln_g, ln_b, w_s, b_s, ts):
    m = rest.shape[0]
    kern = functools.partial(_gmlp_kernel, ts=ts)
    return pl.pallas_call(
        kern,
        grid=(m // ts,),
        in_specs=[pl.BlockSpec((ts, C_WIDTH), lambda i: (i, 0)),
                  pl.BlockSpec((ts, C_WIDTH), lambda i: (i, 1)),
                  pl.BlockSpec((1, C_WIDTH), lambda i: (0, 0)),
                  pl.BlockSpec((1, C_WIDTH), lambda i: (0, 0)),
                  pl.BlockSpec((C_GROUPS, CHUNK, CHUNK), lambda i: (0, 0, 0)),
                  pl.BlockSpec((CHUNK, C_GROUPS), lambda i: (0, 0))],
        out_specs=pl.BlockSpec((ts, C_WIDTH), lambda i: (i, 0)),
        out_shape=jax.ShapeDtypeStruct((m, C_WIDTH), BF16),
        compiler_params=_params(("parallel",), 32),
        name="gmlp",
    )(rest, rest, ln_g.reshape(1, -1), ln_b.reshape(1, -1), w_s, jnp.transpose(b_s))


def _out_proj_kernel(h_ref, oa_ref, ob_ref, oc_ref, wa_ref, wb_ref, wc_ref, g_ref, o_ref):
    y = _dot(oa_ref[...], wa_ref[...]) + _dot(ob_ref[...], wb_ref[...]) + _dot(oc_ref[...], wc_ref[...])
    o_ref[...] = h_ref[...] + _rms(y, g_ref[...])


def _out_proj(h, oa, ob, oc, wa, wb, wc, g, tm):
    m, d = h.shape
    rowblk = lambda i: (i, 0)
    full = lambda i: (0, 0)
    return pl.pallas_call(
        _out_proj_kernel,
        grid=(m // tm,),
        in_specs=[pl.BlockSpec((tm, d), rowblk),
                  pl.BlockSpec((tm, A_WIDTH), rowblk),
                  pl.BlockSpec((tm, B_WIDTH), rowblk),
                  pl.BlockSpec((tm, C_WIDTH), rowblk),
                  pl.BlockSpec((A_WIDTH, d), full),
                  pl.BlockSpec((B_WIDTH, d), full),
                  pl.BlockSpec((C_WIDTH, d), full),
                  pl.BlockSpec((1, d), full)],
        out_specs=pl.BlockSpec((tm, d), rowblk),
        out_shape=jax.ShapeDtypeStruct((m, d), F32),
        compiler_params=_params(("parallel",), 48),
        name="out_proj",
    )(h, oa, ob, oc, wa, wb, wc, g.reshape(1, d))


def _xattn_kernel(h_ref, gpre_ref, wq_ref, k_ref, v_ref, wo_ref, gpost_ref, o_ref):
    x = h_ref[0]
    xn = _rms(x, gpre_ref[...]).astype(BF16)
    q = _dot(xn, wq_ref[...])
    scale = X_DIM ** -0.5
    outs = []
    for hh in range(X_HEADS):
        cs = slice(hh * X_DIM, (hh + 1) * X_DIM)
        s = _dot_nt(q[:, cs].astype(BF16), k_ref[0, :, cs]) * scale
        p = jnp.exp(s - jnp.max(s, axis=-1, keepdims=True))
        p = p * (1.0 / jnp.sum(p, axis=-1, keepdims=True))
        outs.append(_dot(p.astype(BF16), v_ref[0, :, cs]).astype(BF16))
    o = jnp.concatenate(outs, axis=1)
    xa = _dot(o, wo_ref[...])
    o_ref[0] = x + _rms(xa, gpost_ref[...])


def _xattn(h, kv, gpre, wq, wo, gpost, tm):
    b, s, d = h.shape
    mlen = kv.shape[1]
    hw = X_HEADS * X_DIM
    return pl.pallas_call(
        _xattn_kernel,
        grid=(b, s // tm),
        in_specs=[pl.BlockSpec((1, tm, d), lambda bi, i: (bi, i, 0)),
                  pl.BlockSpec((1, d), lambda bi, i: (0, 0)),
                  pl.BlockSpec((d, hw), lambda bi, i: (0, 0)),
                  pl.BlockSpec((1, mlen, hw), lambda bi, i: (bi, 0, 0)),
                  pl.BlockSpec((1, mlen, hw), lambda bi, i: (bi, 0, 1)),
                  pl.BlockSpec((hw, d), lambda bi, i: (0, 0)),
                  pl.BlockSpec((1, d), lambda bi, i: (0, 0))],
        out_specs=pl.BlockSpec((1, tm, d), lambda bi, i: (bi, i, 0)),
        out_shape=jax.ShapeDtypeStruct((b, s, d), F32),
        compiler_params=_params(("parallel", "parallel"), 48),
        name="xattn",
    )(h, gpre.reshape(1, d), wq, kv, kv, wo, gpost.reshape(1, d))


def _ffn_kernel(x_ref, gpre_ref, wg_ref, wu_ref, wd_ref, gpost_ref, o_ref, xn_ref, acc_ref):
    j = pl.program_id(1)

    @pl.when(j == 0)
    def _():
        xn_ref[...] = _rms(x_ref[...], gpre_ref[...]).astype(BF16)
        acc_ref[...] = jnp.zeros(acc_ref.shape, F32)

    xn = xn_ref[...]
    gate = _dot(xn, wg_ref[...])
    up = _dot(xn, wu_ref[...])
    act = gate * (1.0 / (1.0 + jnp.exp(-gate))) * up
    acc_ref[...] += _dot(act.astype(BF16), wd_ref[...])

    @pl.when(j == pl.num_programs(1) - 1)
    def _():
        o_ref[...] = x_ref[...] + _rms(acc_ref[...], gpost_ref[...])


def _ffn(h, gpre, w_gu, w_down, gpost, tm, tf):
    m, d = h.shape
    f = w_down.shape[0]
    nf = f // tf
    return pl.pallas_call(
        _ffn_kernel,
        grid=(m // tm, nf),
        in_specs=[pl.BlockSpec((tm, d), lambda i, j: (i, 0)),
                  pl.BlockSpec((1, d), lambda i, j: (0, 0)),
                  pl.BlockSpec((d, tf), lambda i, j: (0, j)),
                  pl.BlockSpec((d, tf), lambda i, j: (0, nf + j)),
                  pl.BlockSpec((tf, d), lambda i, j: (j, 0)),
                  pl.BlockSpec((1, d), lambda i, j: (0, 0))],
        out_specs=pl.BlockSpec((tm, d), lambda i, j: (i, 0)),
        out_shape=jax.ShapeDtypeStruct((m, d), F32),
        scratch_shapes=[pltpu.VMEM((tm, d), BF16), pltpu.VMEM((tm, d), F32)],
        compiler_params=_params(("parallel", "arbitrary"), 56),
        name="ffn",
    )(h, gpre.reshape(1, d), w_gu, w_gu, w_down, gpost.reshape(1, d))


def _tile(n, pref):
    t = min(pref, n)
    assert n % t == 0
    return t


def kernel(x, mem, rel_bias, mix_pre_g, mix_post_g, w_in, w_out, a_lambda, a_sub_g, b_cq_g, b_ckv_g, b_w_uq, b_w_qidx, b_w_uk, b_w_uv, c_ln_g, c_ln_b, c_w_s, c_b_s, x_pre_g, x_post_g, mem_g, x_wq, x_wkv, x_wo, f_pre_g, f_post_g, f_w_gu, f_w_down):
    bsz, s_len, d = x.shape
    depth = w_in.shape[0]
    tokens = bsz * s_len
    mlen = mem.shape[1]
    t_attn = min(ATTN_BLOCK, s_len)
    bias_a = _bias_tiles(rel_bias[:, :A_HEADS], t_attn)
    bias_b = _bias_tiles(rel_bias[:, A_HEADS:], t_attn)
    sizes = (A_WIDTH, A_WIDTH, A_WIDTH, B_Q_RANK, B_KV_RANK, IDX_DIM, IDX_HEADS, C_WIDTH, C_WIDTH)
    offs = np.concatenate([[0], np.cumsum(sizes)])
    tm = _tile(tokens, 512)

    h = x.reshape(tokens, d)
    mem2 = mem.reshape(bsz * mlen, d)
    for l in range(depth):
        lam_init = 0.8 - 0.6 * math.exp(-0.3 * l)
        wl = w_in[l]
        cols = [wl[:, offs[i]:offs[i + 1]] for i in range(len(sizes))]
        wqa, wka, wva, wcq, wckv, wkidx, wwidx, wcu, wcv = cols
        w_k_rest = jnp.concatenate(
            [wka, wcu, wcv, wcq, wckv, wkidx, wkidx, wwidx,
             jnp.zeros((d, 128 - IDX_HEADS), wl.dtype)], axis=1).astype(BF16)
        wqt = jnp.transpose(wqa * (A_HALF ** -0.5 * LOG2E)).astype(BF16)
        wvt = jnp.transpose(wva).astype(BF16)

        qt, k, vt, rest = _in_proj(h.reshape(bsz, s_len, d), mix_pre_g[l], wqt, wvt, w_k_rest,
                                   t_attn, _tile(s_len, 512))

        lp = a_lambda[l].astype(F32)
        lam = jnp.exp(jnp.sum(lp[0] * lp[1])) - jnp.exp(jnp.sum(lp[2] * lp[3])) + lam_init
        oa = _diff_attention(qt, k, vt, lam.reshape(1), bias_a, a_sub_g[l], 1.0 - lam_init)

        wuq = b_w_uq[l].reshape(B_Q_RANK, B_WIDTH).astype(BF16)
        wuk = jnp.transpose(b_w_uk[l], (1, 0, 2)).astype(BF16)
        wqit = jnp.transpose(b_w_qidx[l].reshape(B_Q_RANK, IDX_HEADS * IDX_DIM)).astype(BF16)
        wuv = jnp.transpose(b_w_uv[l], (1, 2, 0)).astype(BF16)
        prep = _dsa_prep(rest, b_cq_g[l], b_ckv_g[l], wuq, wuk, wqit, t_attn, _tile(s_len, 512))
        ob = _dsa_attention(*prep, bias_b, wuv)

        rest = rest.reshape(tokens, rest.shape[-1])
        oc = _gmlp(rest, c_ln_g[l], c_ln_b[l], c_w_s[l], c_b_s[l], tm)

        wo_l = w_out[l].astype(BF16)
        h = _out_proj(h, oa.reshape(tokens, A_WIDTH), ob.reshape(tokens, B_WIDTH), oc,
                      wo_l[:A_WIDTH], wo_l[A_WIDTH:A_WIDTH + B_WIDTH], wo_l[A_WIDTH + B_WIDTH:],
                      mix_post_g[l], tm)

        hw = X_HEADS * X_DIM
        kv = _norm_matmul(mem2, mem_g[l], x_wkv[l].reshape(d, 2 * hw).astype(BF16), BF16,
                          _tile(bsz * mlen, 512), 512)
        h = _xattn(h.reshape(bsz, s_len, d), kv.reshape(bsz, mlen, 2 * hw), x_pre_g[l],
                   x_wq[l].reshape(d, hw).astype(BF16), x_wo[l].reshape(hw, d).astype(BF16),
                   x_post_g[l], _tile(s_len, 512)).reshape(tokens, d)

        fh = f_w_down.shape[1]
        h = _ffn(h, f_pre_g[l], f_w_gu[l].reshape(d, 2 * fh).astype(BF16),
                 f_w_down[l].astype(BF16), f_post_g[l], tm, 512)
    return h.reshape(bsz, s_len, d)
```

```python
import functools
import math

import numpy as np
import jax
import jax.numpy as jnp
from jax import lax
from jax.experimental import pallas as pl
from jax.experimental.pallas import tpu as pltpu

F32 = jnp.float32
BF16 = jnp.bfloat16
EPS = 1e-6
NEG = -1e30
INT_MIN = -(2 ** 31)
I16_MIN = -(2 ** 15)

HEAD_DIM = 128
A_HEADS = 6
A_HALF = 64
B_HEADS = 6
B_Q_RANK = 512
B_KV_RANK = 256
IDX_HEADS = 16
IDX_DIM = 64
TOPK_MAX = 256
C_GROUPS = 4
C_WIDTH = 512
CHUNK = 128
X_HEADS = 4
X_DIM = 128
REL_BUCKETS = 32
REL_MAX_DIST = 128
A_WIDTH = A_HEADS * HEAD_DIM
B_WIDTH = B_HEADS * HEAD_DIM

ATTN_BLOCK = 256
SCAN_UNROLL = 2
MIB = 1024 * 1024


def _params(semantics, vmem_mib):
    return pltpu.CompilerParams(dimension_semantics=semantics,
                                vmem_limit_bytes=vmem_mib * MIB)


def _rms(x, g):
    return x * lax.rsqrt(jnp.mean(x * x, axis=-1, keepdims=True) + EPS) * g


def _dot(a, b):
    return jnp.dot(a, b, preferred_element_type=F32)


def _dot_nt(a, b):
    return lax.dot_general(a, b, (((1,), (1,)), ((), ())), preferred_element_type=F32)


def _norm_matmul_kernel(x_ref, g_ref, w_ref, o_ref, xn_ref):
    @pl.when(pl.program_id(1) == 0)
    def _():
        xn_ref[...] = _rms(x_ref[...], g_ref[...]).astype(BF16)

    o_ref[...] = _dot(xn_ref[...], w_ref[...]).astype(o_ref.dtype)


def _norm_matmul(x, g, w, out_dtype, tm, tn):
    m, d = x.shape
    n = w.shape[1]
    return pl.pallas_call(
        _norm_matmul_kernel,
        grid=(m // tm, n // tn),
        in_specs=[pl.BlockSpec((tm, d), lambda i, j: (i, 0)),
                  pl.BlockSpec((1, d), lambda i, j: (0, 0)),
                  pl.BlockSpec((d, tn), lambda i, j: (0, j))],
        out_specs=pl.BlockSpec((tm, tn), lambda i, j: (i, j)),
        out_shape=jax.ShapeDtypeStruct((m, n), out_dtype),
        scratch_shapes=[pltpu.VMEM((tm, d), BF16)],
        compiler_params=_params(("parallel", "arbitrary"), 48),
        name="norm_matmul",
    )(x, g.reshape(1, d), w)


SUM_ROWS = 16
IN_PROJ_CHUNK = 1024


def _sum_rows(lead, t, dtype):
    first = lax.broadcasted_iota(jnp.int32, lead + (SUM_ROWS, t), len(lead)) == 0
    return jnp.where(first, 1.0, 0.0).astype(dtype)


def _in_proj_kernel(x_ref, g_ref, wqt_ref, wvt_ref, w_ref, qt_ref, k_ref, vt_ref, rest_ref, *, t):
    xn = _rms(x_ref[0], g_ref[...]).astype(BF16)
    tm = xn.shape[0]
    n_k = k_ref.shape[2]
    k_ref[0] = _dot(xn, w_ref[:, :n_k]).astype(k_ref.dtype)
    for c0 in range(0, rest_ref.shape[2], IN_PROJ_CHUNK):
        c1 = min(c0 + IN_PROJ_CHUNK, rest_ref.shape[2])
        rest_ref[0, :, c0:c1] = _dot(xn, w_ref[:, n_k + c0:n_k + c1])
    qt_ref[0] = _dot_nt(wqt_ref[...], xn).astype(qt_ref.dtype)
    vt = _dot_nt(wvt_ref[...], xn)
    heads, dv = vt_ref.shape[1], vt_ref.shape[3] - SUM_ROWS
    for h in range(heads):
        for blk in range(tm // t):
            vt_ref[0, h, blk, :dv, :] = vt[h * dv:(h + 1) * dv, blk * t:(blk + 1) * t].astype(vt_ref.dtype)
    vt_ref[0, :, :, dv:, :] = _sum_rows((heads, tm // t), t, vt_ref.dtype)


def _in_proj(x, g, wqt, wvt, w, t, tm):
    b, s, d = x.shape
    n_rest = w.shape[1] - A_WIDTH
    resident = lambda shape: pl.BlockSpec(shape, lambda bi, i: (0,) * len(shape), pipeline_mode=pl.Buffered(1))
    kern = functools.partial(_in_proj_kernel, t=t)
    return pl.pallas_call(
        kern,
        grid=(b, s // tm),
        in_specs=[pl.BlockSpec((1, tm, d), lambda bi, i: (bi, i, 0)),
                  pl.BlockSpec((1, d), lambda bi, i: (0, 0)),
                  resident((A_WIDTH, d)), resident((A_WIDTH, d)), resident((d, w.shape[1]))],
        out_specs=[pl.BlockSpec((1, A_WIDTH, tm), lambda bi, i: (bi, 0, i)),
                   pl.BlockSpec((1, tm, A_WIDTH), lambda bi, i: (bi, i, 0)),
                   pl.BlockSpec((1, A_HEADS, tm // t, HEAD_DIM + SUM_ROWS, t), lambda bi, i: (bi, 0, i, 0, 0)),
                   pl.BlockSpec((1, tm, n_rest), lambda bi, i: (bi, i, 0))],
        out_shape=[jax.ShapeDtypeStruct((b, A_WIDTH, s), BF16),
                   jax.ShapeDtypeStruct((b, s, A_WIDTH), BF16),
                   jax.ShapeDtypeStruct((b, A_HEADS, s // t, HEAD_DIM + SUM_ROWS, t), BF16),
                   jax.ShapeDtypeStruct((b, s, n_rest), F32)],
        compiler_params=_params(("parallel", "parallel"), 56),
        name="in_proj",
    )(x, g.reshape(1, d), wqt, wvt, w)


LOG2E = math.log2(math.e)


def _attend_causal_blocks(qi, t, n_col, lhs_block, rhs, values_at,
                          far_addend, prev_addend, diag_addend,
                          acc_ref, s0_ref, s1_ref, pa_ref, pb_ref):
    nq = n_col * t
    n_far_pairs = qi // 2
    pad = 2 * n_far_pairs - (qi - 1)

    def block_of(j):
        return jnp.clip(j - pad, 0, qi)

    def cols(c):
        return slice(c * t, (c + 1) * t)

    def scores_into(ref, j, c):
        ref[:, cols(c)] = _dot(lhs_block(block_of(j)), rhs[:, cols(c)])

    def accumulate(alpha, j, p_ref, c):
        acc_ref[:, cols(c)] = (alpha[:, cols(c)] * acc_ref[:, cols(c)]
                               + _dot(values_at(block_of(j)), p_ref[:, cols(c)]))

    def probs_into(p_ref, s_ref, m, addend, c):
        s = s_ref[:, cols(c)]
        if addend is not None:
            s = s + addend(c)
        m_new = jnp.maximum(m[:, cols(c)], jnp.max(s, axis=0, keepdims=True))
        p_ref[:, cols(c)] = jnp.exp2(s - m_new).astype(BF16)
        return m_new, jnp.exp2(m[:, cols(c)] - m_new)

    def step(j, carry, addend_a, addend_b):
        m, alpha_late = carry
        add_a, add_b = addend_a(block_of(j)), addend_b(block_of(j + 1))
        m_a, alpha_a = [], []
        for c in range(n_col):
            accumulate(alpha_late, j - 1, pb_ref, c)
            scores_into(s1_ref, j + 1, c)
            m_c, alpha_c = probs_into(pa_ref, s0_ref, m, add_a, c)
            m_a.append(m_c)
            alpha_a.append(alpha_c)
        m, alpha = jnp.concatenate(m_a, axis=1), jnp.concatenate(alpha_a, axis=1)
        m_b, alpha_b = [], []
        for c in range(n_col):
            accumulate(alpha, j, pa_ref, c)
            scores_into(s0_ref, j + 2, c)
            m_c, alpha_c = probs_into(pb_ref, s1_ref, m, add_b, c)
            m_b.append(m_c)
            alpha_b.append(alpha_c)
        m, alpha_late = jnp.concatenate(m_b, axis=1), jnp.concatenate(alpha_b, axis=1)
        return m, jnp.where(j < pad, 0.0, alpha_late)

    acc_ref[...] = jnp.zeros(acc_ref.shape, F32)
    pb_ref[...] = jnp.zeros(pb_ref.shape, BF16)
    for c in range(n_col):
        scores_into(s0_ref, 0, c)
    carry = (jnp.full((1, nq), NEG, F32), jnp.ones((1, nq), F32))
    carry = lax.fori_loop(0, n_far_pairs, lambda i, c: step(2 * i, c, far_addend, far_addend), carry)
    j_near = 2 * n_far_pairs
    _, alpha_late = step(j_near, carry, prev_addend, diag_addend)
    for c in range(n_col):
        accumulate(alpha_late, j_near + 1, pb_ref, c)


def _bucket_np(dist):
    n = np.maximum(dist, 0)
    max_exact = REL_BUCKETS // 2
    nf = np.maximum(n, 1).astype(np.float64)
    large = max_exact + (np.log(nf / max_exact) / math.log(REL_MAX_DIST / max_exact)
                         * (REL_BUCKETS - max_exact)).astype(np.int32)
    large = np.minimum(large, REL_BUCKETS - 1)
    return np.where(n < max_exact, n, large)


NO_BLOCK, PREV_BLOCK, DIAG_BLOCK = 0, 1, 2


def _bias_tiles(tab, t):
    assert t >= REL_MAX_DIST
    length = 2 * t
    bucket = jnp.asarray(_bucket_np(np.arange(length)))
    by_dist = ((jnp.take(tab, bucket, axis=0) - tab[REL_BUCKETS - 1]) * LOG2E).T

    def skew(v):
        flat = jnp.tile(v, (1, t))[:, :t * (length - 1)]
        return flat.reshape(-1, t, length - 1)[:, :, :t]

    diag = skew(by_dist)
    prev = skew(jnp.roll(by_dist, -t, axis=1))
    key = np.arange(t)[:, None]
    query = np.arange(t)[None, :]
    diag = jnp.where(jnp.asarray(query >= key)[None], diag, NEG)
    return jnp.stack([jnp.full_like(prev, NEG), prev, diag], axis=1).astype(F32)


def _diff_attn_kernel(lam_ref, qt_ref, k_ref, vt_ref, bias_ref, g_ref, o_ref, acc_ref, s0_ref, s1_ref, pa_ref, pb_ref,
                      *, t, out_scale):
    qi = pl.program_id(2)
    qt = qt_ref[0]
    sub = lax.broadcasted_iota(jnp.int32, qt.shape, 0)
    zero = jnp.zeros_like(qt)
    qs = jnp.concatenate([jnp.where(sub < A_HALF, qt, zero),
                          jnp.where(sub >= A_HALF, qt, zero)], axis=1)
    def lhs_block(kb):
        return k_ref[0, pl.ds(pl.multiple_of(kb * t, t), t), :]

    prev_kind = jnp.where(qi >= 1, PREV_BLOCK, NO_BLOCK)
    _attend_causal_blocks(
        qi, t, 2, lhs_block, qs, lambda kb: vt_ref[0, 0, kb],
        lambda kb: None,
        lambda kb: (lambda c: bias_ref[0, prev_kind]),
        lambda kb: (lambda c: bias_ref[0, DIAG_BLOCK]),
        acc_ref, s0_ref, s1_ref, pa_ref, pb_ref)

    acc = acc_ref[...]
    o = acc[:HEAD_DIM] * (1.0 / acc[HEAD_DIM:HEAD_DIM + 1])
    o = o[:, :t] - lam_ref[0] * o[:, t:]
    y = o * lax.rsqrt(jnp.mean(o * o, axis=0, keepdims=True) + EPS) * g_ref[...] * out_scale
    o_ref[0] = y.T.astype(o_ref.dtype)


def _diff_attention(qt, k, vt, lam, bias, sub_g, out_scale):
    b, s, _ = k.shape
    nkb, t = vt.shape[2], vt.shape[4]
    kern = functools.partial(_diff_attn_kernel, t=t, out_scale=out_scale)
    return pl.pallas_call(
        kern,
        grid=(b, A_HEADS, s // t),
        in_specs=[pl.BlockSpec(memory_space=pltpu.SMEM),
                  pl.BlockSpec((1, HEAD_DIM, t), lambda bi, h, qi: (bi, h, qi)),
                  pl.BlockSpec((1, s, HEAD_DIM), lambda bi, h, qi: (bi, 0, h)),
                  pl.BlockSpec((1, 1, nkb, HEAD_DIM + SUM_ROWS, t), lambda bi, h, qi: (bi, h, 0, 0, 0)),
                  pl.BlockSpec((1, 3, t, t), lambda bi, h, qi: (h, 0, 0, 0)),
                  pl.BlockSpec((HEAD_DIM, 1), lambda bi, h, qi: (0, 0))],
        out_specs=pl.BlockSpec((1, t, HEAD_DIM), lambda bi, h, qi: (bi, qi, h)),
        out_shape=jax.ShapeDtypeStruct((b, s, A_WIDTH), BF16),
        scratch_shapes=[pltpu.VMEM((HEAD_DIM + SUM_ROWS, 2 * t), F32),
                        pltpu.VMEM((t, 2 * t), F32), pltpu.VMEM((t, 2 * t), F32),
                        pltpu.VMEM((t, 2 * t), BF16), pltpu.VMEM((t, 2 * t), BF16)],
        compiler_params=_params(("parallel", "parallel", "arbitrary"), 32),
        name="diff_attention",
    )(lam, qt, k, vt, bias, sub_g.reshape(HEAD_DIM, 1))


def _dsa_prep_kernel(cq_ref, ckv_ref, kk_ref, wi_ref, gq_ref, gkv_ref, wuq_ref, wuk_ref, wqit_ref,
                     qlat_ref, qidx_ref, wis_ref, kkb_ref, ckvn_ref, ckvnt_ref, *, t):
    cqn = _rms(cq_ref[0], gq_ref[...]).astype(BF16)
    tm = cqn.shape[0]
    q = _dot(cqn, wuq_ref[...])
    for h in range(B_HEADS):
        qh = q[:, h * HEAD_DIM:(h + 1) * HEAD_DIM].astype(BF16)
        qlat = (_dot_nt(wuk_ref[h], qh) * (HEAD_DIM ** -0.5 * LOG2E)).astype(BF16)
        for blk in range(tm // t):
            qlat_ref[0, blk, :, h * t:(h + 1) * t] = qlat[:, blk * t:(blk + 1) * t]
    qidx_ref[0] = _dot_nt(wqit_ref[...], cqn).astype(BF16)
    wis = wi_ref[0] * (IDX_HEADS ** -0.5 * IDX_DIM ** -0.5)
    wis_ref[0] = wis.T[:IDX_HEADS]
    kkb_ref[0] = kk_ref[0].astype(BF16)
    ckvn = _rms(ckv_ref[0], gkv_ref[...])
    ckvn_ref[0] = ckvn.astype(BF16)
    ckvn_t = ckvn.T.astype(BF16)
    for blk in range(tm // t):
        ckvnt_ref[0, blk, :B_KV_RANK, :] = ckvn_t[:, blk * t:(blk + 1) * t]
    ckvnt_ref[0, :, B_KV_RANK:, :] = _sum_rows((tm // t,), t, BF16)


def _dsa_prep(rest, gq, gkv, wuq, wuk, wqit, t, tm):
    b, s, _ = rest.shape
    col = lambda c: (lambda bi, i: (bi, i, c))
    const = lambda n: (lambda bi, i: (0,) * n)
    kern = functools.partial(_dsa_prep_kernel, t=t)
    return pl.pallas_call(
        kern,
        grid=(b, s // tm),
        in_specs=[pl.BlockSpec((1, tm, B_Q_RANK), col(2)),
                  pl.BlockSpec((1, tm, B_KV_RANK), col(6)),
                  pl.BlockSpec((1, tm, 128), col(14)),
                  pl.BlockSpec((1, tm, 128), col(15)),
                  pl.BlockSpec((1, B_Q_RANK), const(2)),
                  pl.BlockSpec((1, B_KV_RANK), const(2)),
                  pl.BlockSpec((B_Q_RANK, B_WIDTH), const(2)),
                  pl.BlockSpec((B_HEADS, B_KV_RANK, HEAD_DIM), const(3)),
                  pl.BlockSpec((IDX_HEADS * IDX_DIM, B_Q_RANK), const(2))],
        out_specs=[pl.BlockSpec((1, tm // t, B_KV_RANK, B_HEADS * t), lambda bi, i: (bi, i, 0, 0)),
                   pl.BlockSpec((1, IDX_HEADS * IDX_DIM, tm), lambda bi, i: (bi, 0, i)),
                   pl.BlockSpec((1, IDX_HEADS, tm), lambda bi, i: (bi, 0, i)),
                   pl.BlockSpec((1, tm, 128), lambda bi, i: (bi, i, 0)),
                   pl.BlockSpec((1, tm, B_KV_RANK), lambda bi, i: (bi, i, 0)),
                   pl.BlockSpec((1, tm // t, B_KV_RANK + SUM_ROWS, t), lambda bi, i: (bi, i, 0, 0))],
        out_shape=[jax.ShapeDtypeStruct((b, s // t, B_KV_RANK, B_HEADS * t), BF16),
                   jax.ShapeDtypeStruct((b, IDX_HEADS * IDX_DIM, s), BF16),
                   jax.ShapeDtypeStruct((b, IDX_HEADS, s), F32),
                   jax.ShapeDtypeStruct((b, s, 128), BF16),
                   jax.ShapeDtypeStruct((b, s, B_KV_RANK), BF16),
                   jax.ShapeDtypeStruct((b, s // t, B_KV_RANK + SUM_ROWS, t), BF16)],
        compiler_params=_params(("parallel", "parallel"), 40),
        name="dsa_prep",
    )(rest, rest, rest, rest, gq.reshape(1, -1), gkv.reshape(1, -1), wuq, wuk, wqit)


def _dsa_kernel(qlat_ref, qidx_ref, wis_ref, kk_ref, ckvn_ref, ckvnt_ref, bias_ref, wuv_ref, o_ref,
                key_ref, hi_ref, lo_ref, acc_ref, s0_ref, s1_ref, pa_ref, pb_ref, *, t, topk):
    qi = pl.program_id(1)
    nblk = qi + 1
    nkb = key_ref.shape[0]
    key_pos = lax.broadcasted_iota(jnp.int32, (t, t), 0)
    query_pos = lax.broadcasted_iota(jnp.int32, (t, t), 1)

    qidx = qidx_ref[0]
    wis = wis_ref[0]
    sub = lax.broadcasted_iota(jnp.int32, (2 * IDX_DIM, t), 0)
    q_heads = []
    for g in range(IDX_HEADS):
        pair = qidx[(g // 2) * 2 * IDX_DIM:(g // 2 + 1) * 2 * IDX_DIM]
        keep = (sub < IDX_DIM) if g % 2 == 0 else (sub >= IDX_DIM)
        q_heads.append(jnp.where(keep, pair, jnp.zeros_like(pair)))

    def score_block(kb, c):
        ks = pl.multiple_of(kb * t, t)
        kblk = kk_ref[0, pl.ds(ks, t), :]
        score = jnp.zeros((t, t), F32)
        for g in range(IDX_HEADS):
            d = _dot(kblk, q_heads[g])
            score = score + jnp.maximum(d, 0.0) * wis[g:g + 1]
        bits = pltpu.bitcast(score, jnp.int32)
        key = jnp.where(bits < 0, bits ^ jnp.int32(0x7FFFFFFF), bits)
        causal = (kb < qi) | (query_pos >= key_pos)
        key = jnp.where(causal, key, jnp.int32(INT_MIN))
        key_ref[kb] = key
        hi_ref[kb] = (key >> 16).astype(jnp.int16)
        lo_ref[kb] = ((key & 0xFFFF) + I16_MIN).astype(jnp.int16)
        return c

    lax.fori_loop(0, nblk, score_block, 0)

    hi_ref[nkb] = jnp.full((t, t), I16_MIN, jnp.int16)
    lo_ref[nkb] = jnp.full((t, t), I16_MIN, jnp.int16)
    one, zero = jnp.int16(1), jnp.int16(0)

    def count_ge(plane_ref, cand):
        cand_b = jnp.broadcast_to(cand.astype(jnp.int16), (t, t))

        def body(i, c):
            for u in range(SCAN_UNROLL):
                kb = SCAN_UNROLL * i + u
                x = jnp.where(plane_ref[jnp.where(kb < nblk, kb, nkb)] >= cand_b, one, zero)
                parts = [x[r * 16:(r + 1) * 16] for r in range(t // 16)]
                while len(parts) > 1:
                    parts = [a + b for a, b in zip(parts[0::2], parts[1::2])]
                c = c + parts[0]
            return c

        trips = (nblk + SCAN_UNROLL - 1) // SCAN_UNROLL
        c = lax.fori_loop(0, trips, body, jnp.zeros((16, t), jnp.int16))
        return jnp.sum(c.astype(jnp.int32), axis=0, keepdims=True)

    def greedy_bits(count_at_least):
        def bit_step(i, v):
            cand = v + jnp.left_shift(jnp.int32(1), 15 - i)
            return jnp.where(count_at_least(cand) >= topk, cand, v)
        return lax.fori_loop(0, 16, bit_step, jnp.full((1, t), I16_MIN, jnp.int32))

    hi = greedy_bits(lambda cand: count_ge(hi_ref, cand))
    above = count_ge(hi_ref, hi + 1)
    hi_b = jnp.broadcast_to(hi.astype(jnp.int16), (t, t))

    def restrict(kb, c):
        lo_ref[kb] = jnp.where(hi_ref[kb] == hi_b, lo_ref[kb], jnp.int16(I16_MIN))
        return c

    lax.fori_loop(0, nblk, restrict, 0)
    lo = greedy_bits(lambda cand: above + count_ge(lo_ref, cand))
    thr = jnp.left_shift(hi, 16) + (lo - I16_MIN)
    thr_b = jnp.broadcast_to(jnp.maximum(thr, jnp.int32(INT_MIN + 1)), (t, t))

    def count_keys(pred):
        def body(kb, c):
            x = jnp.where(pred(key_ref[kb], kb * t), 1, 0)
            return c + jnp.sum(x.reshape(t // 8, 8, t), axis=0)
        c = lax.fori_loop(0, nblk, body, jnp.zeros((8, t), jnp.int32))
        return jnp.sum(c, axis=0, keepdims=True)

    selected = count_keys(lambda key, base: key >= thr_b)

    @pl.when(jnp.max(selected) > topk)
    def _():
        need = topk - count_keys(lambda key, base: key > thr_b)

        def index_bit(i, cut):
            cand = cut + jnp.left_shift(jnp.int32(1), index_bits - 1 - i)
            cand_b = jnp.broadcast_to(cand, (t, t))
            below = count_keys(lambda key, base: (key == thr_b) & (key_pos + base < cand_b))
            return jnp.where(below < need, cand, cut)

        index_bits = max(1, (nkb * t - 1).bit_length())
        cut = lax.fori_loop(0, index_bits, index_bit, jnp.zeros((1, t), jnp.int32))
        cut_b = jnp.broadcast_to(cut, (t, t))

        def demote(kb, c):
            key = key_ref[kb]
            key_ref[kb] = jnp.where((key == thr_b) & (key_pos + kb * t > cut_b), key - 1, key)
            return c

        lax.fori_loop(0, nblk, demote, 0)

    def lhs_block(kb):
        return ckvn_ref[0, pl.ds(pl.multiple_of(kb * t, t), t), :]

    def selection(kb):
        return jnp.where(key_ref[kb] >= thr_b, 0.0, NEG)

    def far_addend(kb):
        mask = selection(kb)
        return lambda h: mask

    def near_addend(kind):
        def addend(kb):
            mask = selection(kb)
            return lambda h: bias_ref[h, kind] + mask
        return addend

    _attend_causal_blocks(
        qi, t, B_HEADS, lhs_block, qlat_ref[0, 0], lambda kb: ckvnt_ref[0, kb],
        far_addend, near_addend(jnp.where(qi >= 1, PREV_BLOCK, NO_BLOCK)), near_addend(DIAG_BLOCK),
        acc_ref, s0_ref, s1_ref, pa_ref, pb_ref)

    for h in range(B_HEADS):
        cols = slice(h * t, (h + 1) * t)
        o_lat = (acc_ref[:B_KV_RANK, cols] * (1.0 / acc_ref[B_KV_RANK:B_KV_RANK + 1, cols])).astype(BF16)
        o = _dot(wuv_ref[h], o_lat)
        o_ref[0, :, h * HEAD_DIM:(h + 1) * HEAD_DIM] = o.T.astype(o_ref.dtype)


def _dsa_attention(qlat_t, qidx_t, wis_t, kk, ckvn, ckvn_t, bias, wuv):
    b, s, _ = ckvn.shape
    nkb, t = ckvn_t.shape[1], ckvn_t.shape[3]
    topk = min(TOPK_MAX, s // 4)
    kern = functools.partial(_dsa_kernel, t=t, topk=topk)
    qcol = lambda bi, qi: (bi, 0, qi)
    allk = lambda bi, qi: (bi, 0, 0)
    return pl.pallas_call(
        kern,
        grid=(b, s // t),
        in_specs=[pl.BlockSpec((1, 1, B_KV_RANK, B_HEADS * t), lambda bi, qi: (bi, qi, 0, 0)),
                  pl.BlockSpec((1, IDX_HEADS * IDX_DIM, t), qcol),
                  pl.BlockSpec((1, IDX_HEADS, t), qcol),
                  pl.BlockSpec((1, s, 128), allk),
                  pl.BlockSpec((1, s, B_KV_RANK), allk),
                  pl.BlockSpec((1, nkb, B_KV_RANK + SUM_ROWS, t), lambda bi, qi: (bi, 0, 0, 0)),
                  pl.BlockSpec((B_HEADS, 3, t, t), lambda bi, qi: (0, 0, 0, 0)),
                  pl.BlockSpec((B_HEADS, HEAD_DIM, B_KV_RANK), lambda bi, qi: (0, 0, 0))],
        out_specs=pl.BlockSpec((1, t, B_WIDTH), lambda bi, qi: (bi, qi, 0)),
        out_shape=jax.ShapeDtypeStruct((b, s, B_WIDTH), BF16),
        scratch_shapes=[pltpu.VMEM((nkb, t, t), jnp.int32),
                        pltpu.VMEM((nkb + 1, t, t), jnp.int16), pltpu.VMEM((nkb + 1, t, t), jnp.int16),
                        pltpu.VMEM((B_KV_RANK + SUM_ROWS, B_HEADS * t), F32),
                        pltpu.VMEM((t, B_HEADS * t), F32), pltpu.VMEM((t, B_HEADS * t), F32),
                        pltpu.VMEM((t, B_HEADS * t), BF16), pltpu.VMEM((t, B_HEADS * t), BF16)],
        compiler_params=_params(("parallel", "arbitrary"), 56),
        name="dsa_attention",
    )(qlat_t, qidx_t, wis_t, kk, ckvn, ckvn_t, bias, wuv)


def _gelu(x):
    return x * (0.5 * (1.0 + jnp.tanh(math.sqrt(2.0 / math.pi) * (x + 0.044715 * (x * x * x)))))


def _gmlp_kernel(cu_ref, cv_ref, g_ref, b_ref, ws_ref, bs_ref, o_ref, *, ts):
    u = _gelu(cu_ref[...])
    v = _gelu(cv_ref[...])
    mu = jnp.mean(v, axis=-1, keepdims=True)
    vc = v - mu
    var = jnp.mean(vc * vc, axis=-1, keepdims=True)
    vn = (vc * lax.rsqrt(var + EPS) * g_ref[...] + b_ref[...]).astype(BF16)
    row = lax.broadcasted_iota(jnp.int32, (CHUNK, CHUNK), 0)
    col = lax.broadcasted_iota(jnp.int32, (CHUNK, CHUNK), 1)
    for g in range(C_GROUPS):
        w = jnp.where(row >= col, ws_ref[g], 0.0).astype(BF16)
        bcol = bs_ref[:, g:g + 1]
        cs = slice(g * 128, (g + 1) * 128)
        for c in range(ts // CHUNK):
            rs = slice(c * CHUNK, (c + 1) * CHUNK)
            y = _dot(w, vn[rs, cs]) + bcol
            o_ref[rs, cs] = (u[rs, cs] * y).astype(o_ref.dtype)


def _gmlp(rest, ln_g, ln_b, w_s, b_s, ts):
    m = rest.shape[0]
    kern = functools.partial(_gmlp_kernel, ts=ts)
    return pl.pallas_call(
        kern,
        grid=(m // ts,),
        in_specs=[pl.BlockSpec((ts, C_WIDTH), lambda i: (i, 0)),
                  pl.BlockSpec((ts, C_WIDTH), lambda i: (i, 1)),
                  pl.BlockSpec((1, C_WIDTH), lambda i: (0, 0)),
                  pl.BlockSpec((1, C_WIDTH), lambda i: (0, 0)),
                  pl.BlockSpec((C_GROUPS, CHUNK, CHUNK), lambda i: (0, 0, 0)),
                  pl.BlockSpec((CHUNK, C_GROUPS), lambda i: (0, 0))],
        out_specs=pl.BlockSpec((ts, C_WIDTH), lambda i: (i, 0)),
        out_shape=jax.ShapeDtypeStruct((m, C_WIDTH), BF16),
        compiler_params=_params(("parallel",), 32),
        name="gmlp",
    )(rest, rest, ln_g.reshape(1, -1), ln_b.reshape(1, -1), w_s, jnp.transpose(b_s))


def _out_proj_kernel(h_ref, oa_ref, ob_ref, oc_ref, wa_ref, wb_ref, wc_ref, g_ref, o_ref):
    y = _dot(oa_ref[...], wa_ref[...]) + _dot(ob_ref[...], wb_ref[...]) + _dot(oc_ref[...], wc_ref[...])
    o_ref[...] = h_ref[...] + _rms(y, g_ref[...])


def _out_proj(h, oa, ob, oc, wa, wb, wc, g, tm):
    m, d = h.shape
    rowblk = lambda i: (i, 0)
    full = lambda i: (0, 0)
    return pl.pallas_call(
        _out_proj_kernel,
        grid=(m // tm,),
        in_specs=[pl.BlockSpec((tm, d), rowblk),
                  pl.BlockSpec((tm, A_WIDTH), rowblk),
                  pl.BlockSpec((tm, B_WIDTH), rowblk),
                  pl.BlockSpec((tm, C_WIDTH), rowblk),
                  pl.BlockSpec((A_WIDTH, d), full),
                  pl.BlockSpec((B_WIDTH, d), full),
                  pl.BlockSpec((C_WIDTH, d), full),
                  pl.BlockSpec((1, d), full)],
        out_specs=pl.BlockSpec((tm, d), rowblk),
        out_shape=jax.ShapeDtypeStruct((m, d), F32),
        compiler_params=_params(("parallel",), 48),
        name="out_proj",
    )(h, oa, ob, oc, wa, wb, wc, g.reshape(1, d))


def _xattn_kernel(h_ref, gpre_ref, wq_ref, k_ref, v_ref, wo_ref, gpost_ref, o_ref):
    x = h_ref[0]
    xn = _rms(x, gpre_ref[...]).astype(BF16)
    q = _dot(xn, wq_ref[...])
    scale = X_DIM ** -0.5
    outs = []
    for hh in range(X_HEADS):
        cs = slice(hh * X_DIM, (hh + 1) * X_DIM)
        s = _dot_nt(q[:, cs].astype(BF16), k_ref[0, :, cs]) * scale
        p = jnp.exp(s - jnp.max(s, axis=-1, keepdims=True))
        p = p * (1.0 / jnp.sum(p, axis=-1, keepdims=True))
        outs.append(_dot(p.astype(BF16), v_ref[0, :, cs]).astype(BF16))
    o = jnp.concatenate(outs, axis=1)
    xa = _dot(o, wo_ref[...])
    o_ref[0] = x + _rms(xa, gpost_ref[...])


def _xattn(h, kv, gpre, wq, wo, gpost, tm):
    b, s, d = h.shape
    mlen = kv.shape[1]
    hw = X_HEADS * X_DIM
    return pl.pallas_call(
        _xattn_kernel,
        grid=(b, s // tm),
        in_specs=[pl.BlockSpec((1, tm, d), lambda bi, i: (bi, i, 0)),
                  pl.BlockSpec((1, d), lambda bi, i: (0, 0)),
                  pl.BlockSpec((d, hw), lambda bi, i: (0, 0)),
                  pl.BlockSpec((1, mlen, hw), lambda bi, i: (bi, 0, 0)),
                  pl.BlockSpec((1, mlen, hw), lambda bi, i: (bi, 0, 1)),
                  pl.BlockSpec((hw, d), lambda bi, i: (0, 0)),
                  pl.BlockSpec((1, d), lambda bi, i: (0, 0))],
        out_specs=pl.BlockSpec((1, tm, d), lambda bi, i: (bi, i, 0)),
        out_shape=jax.ShapeDtypeStruct((b, s, d), F32),
        compiler_params=_params(("parallel", "parallel"), 48),
        name="xattn",
    )(h, gpre.reshape(1, d), wq, kv, kv, wo, gpost.reshape(1, d))


def _ffn_kernel(x_ref, gpre_ref, wg_ref, wu_ref, wd_ref, gpost_ref, o_ref, xn_ref, acc_ref):
    j = pl.program_id(1)

    @pl.when(j == 0)
    def _():
        xn_ref[...] = _rms(x_ref[...], gpre_ref[...]).astype(BF16)
        acc_ref[...] = jnp.zeros(acc_ref.shape, F32)

    xn = xn_ref[...]
    gate = _dot(xn, wg_ref[...])
    up = _dot(xn, wu_ref[...])
    act = gate * (1.0 / (1.0 + jnp.exp(-gate))) * up
    acc_ref[...] += _dot(act.astype(BF16), wd_ref[...])

    @pl.when(j == pl.num_programs(1) - 1)
    def _():
        o_ref[...] = x_ref[...] + _rms(acc_ref[...], gpost_ref[...])


def _ffn(h, gpre, w_gu, w_down, gpost, tm, tf):
    m, d = h.shape
    f = w_down.shape[0]
    nf = f // tf
    return pl.pallas_call(
        _ffn_kernel,
        grid=(m // tm, nf),
        in_specs=[pl.BlockSpec((tm, d), lambda i, j: (i, 0)),
                  pl.BlockSpec((1, d), lambda i, j: (0, 0)),
                  pl.BlockSpec((d, tf), lambda i, j: (0, j)),
                  pl.BlockSpec((d, tf), lambda i, j: (0, nf + j)),
                  pl.BlockSpec((tf, d), lambda i, j: (j, 0)),
                  pl.BlockSpec((1, d), lambda i, j: (0, 0))],
        out_specs=pl.BlockSpec((tm, d), lambda i, j: (i, 0)),
        out_shape=jax.ShapeDtypeStruct((m, d), F32),
        scratch_shapes=[pltpu.VMEM((tm, d), BF16), pltpu.VMEM((tm, d), F32)],
        compiler_params=_params(("parallel", "arbitrary"), 56),
        name="ffn",
    )(h, gpre.reshape(1, d), w_gu, w_gu, w_down, gpost.reshape(1, d))


def _tile(n, pref):
    t = min(pref, n)
    assert n % t == 0
    return t


def kernel(x, mem, rel_bias, mix_pre_g, mix_post_g, w_in, w_out, a_lambda, a_sub_g, b_cq_g, b_ckv_g, b_w_uq, b_w_qidx, b_w_uk, b_w_uv, c_ln_g, c_ln_b, c_w_s, c_b_s, x_pre_g, x_post_g, mem_g, x_wq, x_wkv, x_wo, f_pre_g, f_post_g, f_w_gu, f_w_down):
    bsz, s_len, d = x.shape
    depth = w_in.shape[0]
    tokens = bsz * s_len
    mlen = mem.shape[1]
    t_attn = min(ATTN_BLOCK, s_len)
    bias_a = _bias_tiles(rel_bias[:, :A_HEADS], t_attn)
    bias_b = _bias_tiles(rel_bias[:, A_HEADS:], t_attn)
    sizes = (A_WIDTH, A_WIDTH, A_WIDTH, B_Q_RANK, B_KV_RANK, IDX_DIM, IDX_HEADS, C_WIDTH, C_WIDTH)
    offs = np.concatenate([[0], np.cumsum(sizes)])
    tm = _tile(tokens, 512)

    h = x.reshape(tokens, d)
    mem2 = mem.reshape(bsz * mlen, d)
    for l in range(depth):
        lam_init = 0.8 - 0.6 * math.exp(-0.3 * l)
        wl = w_in[l]
        cols = [wl[:, offs[i]:offs[i + 1]] for i in range(len(sizes))]
        wqa, wka, wva, wcq, wckv, wkidx, wwidx, wcu, wcv = cols
        w_k_rest = jnp.concatenate(
            [wka, wcu, wcv, wcq, wckv, wkidx, wkidx, wwidx,
             jnp.zeros((d, 128 - IDX_HEADS), wl.dtype)], axis=1).astype(BF16)
        wqt = jnp.transpose(wqa * (A_HALF ** -0.5 * LOG2E)).astype(BF16)
        wvt = jnp.transpose(wva).astype(BF16)

        qt, k, vt, rest = _in_proj(h.reshape(bsz, s_len, d), mix_pre_g[l], wqt, wvt, w_k_rest,
                                   t_attn, _tile(s_len, 512))

        lp = a_lambda[l].astype(F32)
        lam = jnp.exp(jnp.sum(lp[0] * lp[1])) - jnp.exp(jnp.sum(lp[2] * lp[3])) + lam_init
        oa = _diff_attention(qt, k, vt, lam.reshape(1), bias_a, a_sub_g[l], 1.0 - lam_init)

        wuq = b_w_uq[l].reshape(B_Q_RANK, B_WIDTH).astype(BF16)
        wuk = jnp.transpose(b_w_uk[l], (1, 0, 2)).astype(BF16)
        wqit = jnp.transpose(b_w_qidx[l].reshape(B_Q_RANK, IDX_HEADS * IDX_DIM)).astype(BF16)
        wuv = jnp.transpose(b_w_uv[l], (1, 2, 0)).astype(BF16)
        prep = _dsa_prep(rest, b_cq_g[l], b_ckv_g[l], wuq, wuk, wqit, t_attn, _tile(s_len, 512))
        ob = _dsa_attention(*prep, bias_b, wuv)

        rest = rest.reshape(tokens, rest.shape[-1])
        oc = _gmlp(rest, c_ln_g[l], c_ln_b[l], c_w_s[l], c_b_s[l], tm)

        wo_l = w_out[l].astype(BF16)
        h = _out_proj(h, oa.reshape(tokens, A_WIDTH), ob.reshape(tokens, B_WIDTH), oc,
                      wo_l[:A_WIDTH], wo_l[A_WIDTH:A_WIDTH + B_WIDTH], wo_l[A_WIDTH + B_WIDTH:],
                      mix_post_g[l], tm)

        hw = X_HEADS * X_DIM
        kv = _norm_matmul(mem2, mem_g[l], x_wkv[l].reshape(d, 2 * hw).astype(BF16), BF16,
                          _tile(bsz * mlen, 512), 512)
        h = _xattn(h.reshape(bsz, s_len, d), kv.reshape(bsz, mlen, 2 * hw), x_pre_g[l],
                   x_wq[l].reshape(d, hw).astype(BF16), x_wo[l].reshape(hw, d).astype(BF16),
                   x_post_g[l], _tile(s_len, 512)).reshape(tokens, d)

        fh = f_w_down.shape[1]
        h = _ffn(h, f_pre_g[l], f_w_gu[l].reshape(d, 2 * fh).astype(BF16),
                 f_w_down[l].astype(BF16), f_post_g[l], tm, 512)
    return h.reshape(bsz, s_len, d)
```

```python
import functools
import math

import numpy as np
import jax
import jax.numpy as jnp
from jax import lax
from jax.experimental import pallas as pl
from jax.experimental.pallas import tpu as pltpu

F32 = jnp.float32
BF16 = jnp.bfloat16
EPS = 1e-6
NEG = -1e30
INT_MIN = -(2 ** 31)
I16_MIN = -(2 ** 15)

HEAD_DIM = 128
A_HEADS = 6
A_HALF = 64
B_HEADS = 6
B_Q_RANK = 512
B_KV_RANK = 256
IDX_HEADS = 16
IDX_DIM = 64
TOPK_MAX = 256
C_GROUPS = 4
C_WIDTH = 512
CHUNK = 128
X_HEADS = 4
X_DIM = 128
REL_BUCKETS = 32
REL_MAX_DIST = 128
A_WIDTH = A_HEADS * HEAD_DIM
B_WIDTH = B_HEADS * HEAD_DIM

ATTN_BLOCK = 256
DIFF_HEADS_PER_STEP = 6
SCAN_UNROLL = 2
MIB = 1024 * 1024


def _params(semantics, vmem_mib):
    return pltpu.CompilerParams(dimension_semantics=semantics,
                                vmem_limit_bytes=vmem_mib * MIB)


def _rms(x, g):
    return x * lax.rsqrt(jnp.mean(x * x, axis=-1, keepdims=True) + EPS) * g


def _dot(a, b):
    return jnp.dot(a, b, preferred_element_type=F32)


def _dot_nt(a, b):
    return lax.dot_general(a, b, (((1,), (1,)), ((), ())), preferred_element_type=F32)


def _norm_matmul_kernel(x_ref, g_ref, w_ref, o_ref, xn_ref):
    @pl.when(pl.program_id(1) == 0)
    def _():
        xn_ref[...] = _rms(x_ref[...], g_ref[...]).astype(BF16)

    o_ref[...] = _dot(xn_ref[...], w_ref[...]).astype(o_ref.dtype)


def _norm_matmul(x, g, w, out_dtype, tm, tn):
    m, d = x.shape
    n = w.shape[1]
    return pl.pallas_call(
        _norm_matmul_kernel,
        grid=(m // tm, n // tn),
        in_specs=[pl.BlockSpec((tm, d), lambda i, j: (i, 0)),
                  pl.BlockSpec((1, d), lambda i, j: (0, 0)),
                  pl.BlockSpec((d, tn), lambda i, j: (0, j))],
        out_specs=pl.BlockSpec((tm, tn), lambda i, j: (i, j)),
        out_shape=jax.ShapeDtypeStruct((m, n), out_dtype),
        scratch_shapes=[pltpu.VMEM((tm, d), BF16)],
        compiler_params=_params(("parallel", "arbitrary"), 48),
        name="norm_matmul",
    )(x, g.reshape(1, d), w)


SUM_ROWS = 16
IN_PROJ_CHUNK = 1024


def _sum_rows(lead, t, dtype):
    first = lax.broadcasted_iota(jnp.int32, lead + (SUM_ROWS, t), len(lead)) == 0
    return jnp.where(first, 1.0, 0.0).astype(dtype)


def _in_proj_kernel(x_ref, g_ref, wqt_ref, wvt_ref, w_ref, qt_ref, k_ref, vt_ref, rest_ref, *, t):
    xn = _rms(x_ref[0], g_ref[...]).astype(BF16)
    tm = xn.shape[0]
    n_k = k_ref.shape[2]
    k_ref[0] = _dot(xn, w_ref[:, :n_k]).astype(k_ref.dtype)
    for c0 in range(0, rest_ref.shape[2], IN_PROJ_CHUNK):
        c1 = min(c0 + IN_PROJ_CHUNK, rest_ref.shape[2])
        rest_ref[0, :, c0:c1] = _dot(xn, w_ref[:, n_k + c0:n_k + c1])
    qt_ref[0] = _dot_nt(wqt_ref[...], xn).astype(qt_ref.dtype)
    vt = _dot_nt(wvt_ref[...], xn)
    heads, dv = vt_ref.shape[1], vt_ref.shape[3] - SUM_ROWS
    for h in range(heads):
        for blk in range(tm // t):
            vt_ref[0, h, blk, :dv, :] = vt[h * dv:(h + 1) * dv, blk * t:(blk + 1) * t].astype(vt_ref.dtype)
    vt_ref[0, :, :, dv:, :] = _sum_rows((heads, tm // t), t, vt_ref.dtype)


def _in_proj(x, g, wqt, wvt, w, t, tm):
    b, s, d = x.shape
    n_rest = w.shape[1] - A_WIDTH
    resident = lambda shape: pl.BlockSpec(shape, lambda bi, i: (0,) * len(shape), pipeline_mode=pl.Buffered(1))
    kern = functools.partial(_in_proj_kernel, t=t)
    return pl.pallas_call(
        kern,
        grid=(b, s // tm),
        in_specs=[pl.BlockSpec((1, tm, d), lambda bi, i: (bi, i, 0)),
                  pl.BlockSpec((1, d), lambda bi, i: (0, 0)),
                  resident((A_WIDTH, d)), resident((A_WIDTH, d)), resident((d, w.shape[1]))],
        out_specs=[pl.BlockSpec((1, A_WIDTH, tm), lambda bi, i: (bi, 0, i)),
                   pl.BlockSpec((1, tm, A_WIDTH), lambda bi, i: (bi, i, 0)),
                   pl.BlockSpec((1, A_HEADS, tm // t, HEAD_DIM + SUM_ROWS, t), lambda bi, i: (bi, 0, i, 0, 0)),
                   pl.BlockSpec((1, tm, n_rest), lambda bi, i: (bi, i, 0))],
        out_shape=[jax.ShapeDtypeStruct((b, A_WIDTH, s), BF16),
                   jax.ShapeDtypeStruct((b, s, A_WIDTH), BF16),
                   jax.ShapeDtypeStruct((b, A_HEADS, s // t, HEAD_DIM + SUM_ROWS, t), BF16),
                   jax.ShapeDtypeStruct((b, s, n_rest), F32)],
        compiler_params=_params(("parallel", "parallel"), 56),
        name="in_proj",
    )(x, g.reshape(1, d), wqt, wvt, w)


LOG2E = math.log2(math.e)


def _attend_causal_blocks(qi, t, n_col, lhs_block, rhs_col, values_at,
                          far_addend, prev_addend, diag_addend,
                          acc_ref, s0_ref, s1_ref, pa_ref, pb_ref):
    nq = n_col * t
    n_far_pairs = qi // 2
    pad = 2 * n_far_pairs - (qi - 1)

    def block_of(j):
        return jnp.clip(j - pad, 0, qi)

    def cols(c):
        return slice(c * t, (c + 1) * t)

    def scores_into(ref, j, c):
        ref[:, cols(c)] = _dot(lhs_block(block_of(j), c), rhs_col(c))

    def accumulate(alpha, j, p_ref, c):
        acc_ref[:, cols(c)] = (alpha[:, cols(c)] * acc_ref[:, cols(c)]
                               + _dot(values_at(block_of(j), c), p_ref[:, cols(c)]))

    def probs_into(p_ref, s_ref, m, addend, c):
        s = s_ref[:, cols(c)]
        if addend is not None:
            s = s + addend(c)
        m_new = jnp.maximum(m[:, cols(c)], jnp.max(s, axis=0, keepdims=True))
        p_ref[:, cols(c)] = jnp.exp2(s - m_new).astype(BF16)
        return m_new, jnp.exp2(m[:, cols(c)] - m_new)

    def step(j, carry, addend_a, addend_b):
        m, alpha_late = carry
        add_a, add_b = addend_a(block_of(j)), addend_b(block_of(j + 1))
        m_a, alpha_a = [], []
        for c in range(n_col):
            accumulate(alpha_late, j - 1, pb_ref, c)
            scores_into(s1_ref, j + 1, c)
            m_c, alpha_c = probs_into(pa_ref, s0_ref, m, add_a, c)
            m_a.append(m_c)
            alpha_a.append(alpha_c)
        m, alpha = jnp.concatenate(m_a, axis=1), jnp.concatenate(alpha_a, axis=1)
        m_b, alpha_b = [], []
        for c in range(n_col):
            accumulate(alpha, j, pa_ref, c)
            scores_into(s0_ref, j + 2, c)
            m_c, alpha_c = probs_into(pb_ref, s1_ref, m, add_b, c)
            m_b.append(m_c)
            alpha_b.append(alpha_c)
        m, alpha_late = jnp.concatenate(m_b, axis=1), jnp.concatenate(alpha_b, axis=1)
        return m, jnp.where(j < pad, 0.0, alpha_late)

    acc_ref[...] = jnp.zeros(acc_ref.shape, F32)
    pb_ref[...] = jnp.zeros(pb_ref.shape, BF16)
    for c in range(n_col):
        scores_into(s0_ref, 0, c)
    carry = (jnp.full((1, nq), NEG, F32), jnp.ones((1, nq), F32))
    carry = lax.fori_loop(0, n_far_pairs, lambda i, c: step(2 * i, c, far_addend, far_addend), carry)
    j_near = 2 * n_far_pairs
    _, alpha_late = step(j_near, carry, prev_addend, diag_addend)
    for c in range(n_col):
        accumulate(alpha_late, j_near + 1, pb_ref, c)


def _bucket_np(dist):
    n = np.maximum(dist, 0)
    max_exact = REL_BUCKETS // 2
    nf = np.maximum(n, 1).astype(np.float64)
    large = max_exact + (np.log(nf / max_exact) / math.log(REL_MAX_DIST / max_exact)
                         * (REL_BUCKETS - max_exact)).astype(np.int32)
    large = np.minimum(large, REL_BUCKETS - 1)
    return np.where(n < max_exact, n, large)


NO_BLOCK, PREV_BLOCK, DIAG_BLOCK = 0, 1, 2


def _bias_tiles(tab, t):
    assert t >= REL_MAX_DIST
    length = 2 * t
    bucket = jnp.asarray(_bucket_np(np.arange(length)))
    by_dist = ((jnp.take(tab, bucket, axis=0) - tab[REL_BUCKETS - 1]) * LOG2E).T

    def skew(v):
        flat = jnp.tile(v, (1, t))[:, :t * (length - 1)]
        return flat.reshape(-1, t, length - 1)[:, :, :t]

    diag = skew(by_dist)
    prev = skew(jnp.roll(by_dist, -t, axis=1))
    key = np.arange(t)[:, None]
    query = np.arange(t)[None, :]
    diag = jnp.where(jnp.asarray(query >= key)[None], diag, NEG)
    return jnp.stack([jnp.full_like(prev, NEG), prev, diag], axis=1).astype(F32)


def _diff_attn_kernel(lam_ref, qt_ref, k_ref, vt_ref, bias_ref, g_ref, o_ref, acc_ref, s0_ref, s1_ref, pa_ref, pb_ref,
                      *, t, heads, out_scale):
    qi = pl.program_id(2)
    sub = lax.broadcasted_iota(jnp.int32, (HEAD_DIM, t), 0)
    rhs = []
    for g in range(heads):
        qt = qt_ref[0, g * HEAD_DIM:(g + 1) * HEAD_DIM, :]
        zero = jnp.zeros_like(qt)
        rhs += [jnp.where(sub < A_HALF, qt, zero), jnp.where(sub >= A_HALF, qt, zero)]

    def lhs_block(kb, c):
        g = c // 2
        return k_ref[0, pl.ds(pl.multiple_of(kb * t, t), t), g * HEAD_DIM:(g + 1) * HEAD_DIM]

    prev_kind = jnp.where(qi >= 1, PREV_BLOCK, NO_BLOCK)
    _attend_causal_blocks(
        qi, t, 2 * heads, lhs_block, lambda c: rhs[c], lambda kb, c: vt_ref[0, c // 2, kb],
        lambda kb: None,
        lambda kb: (lambda c: bias_ref[c // 2, prev_kind]),
        lambda kb: (lambda c: bias_ref[c // 2, DIAG_BLOCK]),
        acc_ref, s0_ref, s1_ref, pa_ref, pb_ref)

    for g in range(heads):
        acc = acc_ref[:, 2 * g * t:(2 * g + 2) * t]
        o = acc[:HEAD_DIM] * (1.0 / acc[HEAD_DIM:HEAD_DIM + 1])
        o = o[:, :t] - lam_ref[0] * o[:, t:]
        y = o * lax.rsqrt(jnp.mean(o * o, axis=0, keepdims=True) + EPS) * g_ref[...] * out_scale
        o_ref[0, :, g * HEAD_DIM:(g + 1) * HEAD_DIM] = y.T.astype(o_ref.dtype)


def _diff_attention(qt, k, vt, lam, bias, sub_g, out_scale, heads):
    b, s, _ = k.shape
    nkb, t = vt.shape[2], vt.shape[4]
    hw = heads * HEAD_DIM
    nq = 2 * heads * t
    kern = functools.partial(_diff_attn_kernel, t=t, heads=heads, out_scale=out_scale)
    return pl.pallas_call(
        kern,
        grid=(b, A_HEADS // heads, s // t),
        in_specs=[pl.BlockSpec(memory_space=pltpu.SMEM),
                  pl.BlockSpec((1, hw, t), lambda bi, h, qi: (bi, h, qi)),
                  pl.BlockSpec((1, s, hw), lambda bi, h, qi: (bi, 0, h)),
                  pl.BlockSpec((1, heads, nkb, HEAD_DIM + SUM_ROWS, t), lambda bi, h, qi: (bi, h, 0, 0, 0)),
                  pl.BlockSpec((heads, 3, t, t), lambda bi, h, qi: (h, 0, 0, 0)),
                  pl.BlockSpec((HEAD_DIM, 1), lambda bi, h, qi: (0, 0))],
        out_specs=pl.BlockSpec((1, t, hw), lambda bi, h, qi: (bi, qi, h)),
        out_shape=jax.ShapeDtypeStruct((b, s, A_WIDTH), BF16),
        scratch_shapes=[pltpu.VMEM((HEAD_DIM + SUM_ROWS, nq), F32),
                        pltpu.VMEM((t, nq), F32), pltpu.VMEM((t, nq), F32),
                        pltpu.VMEM((t, nq), BF16), pltpu.VMEM((t, nq), BF16)],
        compiler_params=_params(("parallel", "parallel", "arbitrary"), 56),
        name="diff_attention",
    )(lam, qt, k, vt, bias, sub_g.reshape(HEAD_DIM, 1))


def _dsa_prep_kernel(cq_ref, ckv_ref, kk_ref, wi_ref, gq_ref, gkv_ref, wuq_ref, wuk_ref, wqit_ref,
                     qlat_ref, qidx_ref, wis_ref, kkb_ref, ckvn_ref, ckvnt_ref, *, t):
    cqn = _rms(cq_ref[0], gq_ref[...]).astype(BF16)
    tm = cqn.shape[0]
    q = _dot(cqn, wuq_ref[...])
    for h in range(B_HEADS):
        qh = q[:, h * HEAD_DIM:(h + 1) * HEAD_DIM].astype(BF16)
        qlat = (_dot_nt(wuk_ref[h], qh) * (HEAD_DIM ** -0.5 * LOG2E)).astype(BF16)
        for blk in range(tm // t):
            qlat_ref[0, blk, :, h * t:(h + 1) * t] = qlat[:, blk * t:(blk + 1) * t]
    qidx_ref[0] = _dot_nt(wqit_ref[...], cqn).astype(BF16)
    wis = wi_ref[0] * (IDX_HEADS ** -0.5 * IDX_DIM ** -0.5)
    wis_ref[0] = wis.T[:IDX_HEADS]
    kkb_ref[0] = kk_ref[0].astype(BF16)
    ckvn = _rms(ckv_ref[0], gkv_ref[...])
    ckvn_ref[0] = ckvn.astype(BF16)
    ckvn_t = ckvn.T.astype(BF16)
    for blk in range(tm // t):
        ckvnt_ref[0, blk, :B_KV_RANK, :] = ckvn_t[:, blk * t:(blk + 1) * t]
    ckvnt_ref[0, :, B_KV_RANK:, :] = _sum_rows((tm // t,), t, BF16)


def _dsa_prep(rest, gq, gkv, wuq, wuk, wqit, t, tm):
    b, s, _ = rest.shape
    col = lambda c: (lambda bi, i: (bi, i, c))
    const = lambda n: (lambda bi, i: (0,) * n)
    kern = functools.partial(_dsa_prep_kernel, t=t)
    return pl.pallas_call(
        kern,
        grid=(b, s // tm),
        in_specs=[pl.BlockSpec((1, tm, B_Q_RANK), col(2)),
                  pl.BlockSpec((1, tm, B_KV_RANK), col(6)),
                  pl.BlockSpec((1, tm, 128), col(14)),
                  pl.BlockSpec((1, tm, 128), col(15)),
                  pl.BlockSpec((1, B_Q_RANK), const(2)),
                  pl.BlockSpec((1, B_KV_RANK), const(2)),
                  pl.BlockSpec((B_Q_RANK, B_WIDTH), const(2)),
                  pl.BlockSpec((B_HEADS, B_KV_RANK, HEAD_DIM), const(3)),
                  pl.BlockSpec((IDX_HEADS * IDX_DIM, B_Q_RANK), const(2))],
        out_specs=[pl.BlockSpec((1, tm // t, B_KV_RANK, B_HEADS * t), lambda bi, i: (bi, i, 0, 0)),
                   pl.BlockSpec((1, IDX_HEADS * IDX_DIM, tm), lambda bi, i: (bi, 0, i)),
                   pl.BlockSpec((1, IDX_HEADS, tm), lambda bi, i: (bi, 0, i)),
                   pl.BlockSpec((1, tm, 128), lambda bi, i: (bi, i, 0)),
                   pl.BlockSpec((1, tm, B_KV_RANK), lambda bi, i: (bi, i, 0)),
                   pl.BlockSpec((1, tm // t, B_KV_RANK + SUM_ROWS, t), lambda bi, i: (bi, i, 0, 0))],
        out_shape=[jax.ShapeDtypeStruct((b, s // t, B_KV_RANK, B_HEADS * t), BF16),
                   jax.ShapeDtypeStruct((b, IDX_HEADS * IDX_DIM, s), BF16),
                   jax.ShapeDtypeStruct((b, IDX_HEADS, s), F32),
                   jax.ShapeDtypeStruct((b, s, 128), BF16),
                   jax.ShapeDtypeStruct((b, s, B_KV_RANK), BF16),
                   jax.ShapeDtypeStruct((b, s // t, B_KV_RANK + SUM_ROWS, t), BF16)],
        compiler_params=_params(("parallel", "parallel"), 40),
        name="dsa_prep",
    )(rest, rest, rest, rest, gq.reshape(1, -1), gkv.reshape(1, -1), wuq, wuk, wqit)


def _dsa_kernel(qlat_ref, qidx_ref, wis_ref, kk_ref, ckvn_ref, ckvnt_ref, bias_ref, wuv_ref, o_ref,
                key_ref, hi_ref, lo_ref, acc_ref, s0_ref, s1_ref, pa_ref, pb_ref, *, t, topk):
    qi = pl.program_id(1)
    nblk = qi + 1
    nkb = key_ref.shape[0]
    key_pos = lax.broadcasted_iota(jnp.int32, (t, t), 0)
    query_pos = lax.broadcasted_iota(jnp.int32, (t, t), 1)

    qidx = qidx_ref[0]
    wis = wis_ref[0]
    sub = lax.broadcasted_iota(jnp.int32, (2 * IDX_DIM, t), 0)
    q_heads = []
    for g in range(IDX_HEADS):
        pair = qidx[(g // 2) * 2 * IDX_DIM:(g // 2 + 1) * 2 * IDX_DIM]
        keep = (sub < IDX_DIM) if g % 2 == 0 else (sub >= IDX_DIM)
        q_heads.append(jnp.where(keep, pair, jnp.zeros_like(pair)))

    def score_block(kb, c):
        ks = pl.multiple_of(kb * t, t)
        kblk = kk_ref[0, pl.ds(ks, t), :]
        score = jnp.zeros((t, t), F32)
        for g in range(IDX_HEADS):
            d = _dot(kblk, q_heads[g])
            score = score + jnp.maximum(d, 0.0) * wis[g:g + 1]
        bits = pltpu.bitcast(score, jnp.int32)
        key = jnp.where(bits < 0, bits ^ jnp.int32(0x7FFFFFFF), bits)
        causal = (kb < qi) | (query_pos >= key_pos)
        key = jnp.where(causal, key, jnp.int32(INT_MIN))
        key_ref[kb] = key
        hi_ref[kb] = (key >> 16).astype(jnp.int16)
        lo_ref[kb] = ((key & 0xFFFF) + I16_MIN).astype(jnp.int16)
        return c

    lax.fori_loop(0, nblk, score_block, 0)

    hi_ref[nkb] = jnp.full((t, t), I16_MIN, jnp.int16)
    lo_ref[nkb] = jnp.full((t, t), I16_MIN, jnp.int16)
    one, zero = jnp.int16(1), jnp.int16(0)

    def count_ge(plane_ref, cand):
        cand_b = jnp.broadcast_to(cand.astype(jnp.int16), (t, t))

        def body(i, c):
            for u in range(SCAN_UNROLL):
                kb = SCAN_UNROLL * i + u
                x = jnp.where(plane_ref[jnp.where(kb < nblk, kb, nkb)] >= cand_b, one, zero)
                parts = [x[r * 16:(r + 1) * 16] for r in range(t // 16)]
                while len(parts) > 1:
                    parts = [a + b for a, b in zip(parts[0::2], parts[1::2])]
                c = c + parts[0]
            return c

        trips = (nblk + SCAN_UNROLL - 1) // SCAN_UNROLL
        c = lax.fori_loop(0, trips, body, jnp.zeros((16, t), jnp.int16))
        return jnp.sum(c.astype(jnp.int32), axis=0, keepdims=True)

    def greedy_bits(count_at_least):
        def bit_step(i, v):
            cand = v + jnp.left_shift(jnp.int32(1), 15 - i)
            return jnp.where(count_at_least(cand) >= topk, cand, v)
        return lax.fori_loop(0, 16, bit_step, jnp.full((1, t), I16_MIN, jnp.int32))

    hi = greedy_bits(lambda cand: count_ge(hi_ref, cand))
    above = count_ge(hi_ref, hi + 1)
    hi_b = jnp.broadcast_to(hi.astype(jnp.int16), (t, t))

    def restrict(kb, c):
        lo_ref[kb] = jnp.where(hi_ref[kb] == hi_b, lo_ref[kb], jnp.int16(I16_MIN))
        return c

    lax.fori_loop(0, nblk, restrict, 0)
    lo = greedy_bits(lambda cand: above + count_ge(lo_ref, cand))
    thr = jnp.left_shift(hi, 16) + (lo - I16_MIN)
    thr_b = jnp.broadcast_to(jnp.maximum(thr, jnp.int32(INT_MIN + 1)), (t, t))

    def count_keys(pred):
        def body(kb, c):
            x = jnp.where(pred(key_ref[kb], kb * t), 1, 0)
            return c + jnp.sum(x.reshape(t // 8, 8, t), axis=0)
        c = lax.fori_loop(0, nblk, body, jnp.zeros((8, t), jnp.int32))
        return jnp.sum(c, axis=0, keepdims=True)

    selected = count_keys(lambda key, base: key >= thr_b)

    @pl.when(jnp.max(selected) > topk)
    def _():
        need = topk - count_keys(lambda key, base: key > thr_b)

        def index_bit(i, cut):
            cand = cut + jnp.left_shift(jnp.int32(1), index_bits - 1 - i)
            cand_b = jnp.broadcast_to(cand, (t, t))
            below = count_keys(lambda key, base: (key == thr_b) & (key_pos + base < cand_b))
            return jnp.where(below < need, cand, cut)

        index_bits = max(1, (nkb * t - 1).bit_length())
        cut = lax.fori_loop(0, index_bits, index_bit, jnp.zeros((1, t), jnp.int32))
        cut_b = jnp.broadcast_to(cut, (t, t))

        def demote(kb, c):
            key = key_ref[kb]
            key_ref[kb] = jnp.where((key == thr_b) & (key_pos + kb * t > cut_b), key - 1, key)
            return c

        lax.fori_loop(0, nblk, demote, 0)

    def lhs_block(kb, h):
        return ckvn_ref[0, pl.ds(pl.multiple_of(kb * t, t), t), :]

    def selection(kb):
        return jnp.where(key_ref[kb] >= thr_b, 0.0, NEG)

    def far_addend(kb):
        mask = selection(kb)
        return lambda h: mask

    def near_addend(kind):
        def addend(kb):
            mask = selection(kb)
            return lambda h: bias_ref[h, kind] + mask
        return addend

    _attend_causal_blocks(
        qi, t, B_HEADS, lhs_block, lambda h: qlat_ref[0, 0, :, h * t:(h + 1) * t],
        lambda kb, h: ckvnt_ref[0, kb],
        far_addend, near_addend(jnp.where(qi >= 1, PREV_BLOCK, NO_BLOCK)), near_addend(DIAG_BLOCK),
        acc_ref, s0_ref, s1_ref, pa_ref, pb_ref)

    for h in range(B_HEADS):
        cols = slice(h * t, (h + 1) * t)
        o_lat = (acc_ref[:B_KV_RANK, cols] * (1.0 / acc_ref[B_KV_RANK:B_KV_RANK + 1, cols])).astype(BF16)
        o = _dot(wuv_ref[h], o_lat)
        o_ref[0, :, h * HEAD_DIM:(h + 1) * HEAD_DIM] = o.T.astype(o_ref.dtype)


def _dsa_attention(qlat_t, qidx_t, wis_t, kk, ckvn, ckvn_t, bias, wuv):
    b, s, _ = ckvn.shape
    nkb, t = ckvn_t.shape[1], ckvn_t.shape[3]
    topk = min(TOPK_MAX, s // 4)
    kern = functools.partial(_dsa_kernel, t=t, topk=topk)
    qcol = lambda bi, qi: (bi, 0, qi)
    allk = lambda bi, qi: (bi, 0, 0)
    return pl.pallas_call(
        kern,
        grid=(b, s // t),
        in_specs=[pl.BlockSpec((1, 1, B_KV_RANK, B_HEADS * t), lambda bi, qi: (bi, qi, 0, 0)),
                  pl.BlockSpec((1, IDX_HEADS * IDX_DIM, t), qcol),
                  pl.BlockSpec((1, IDX_HEADS, t), qcol),
                  pl.BlockSpec((1, s, 128), allk),
                  pl.BlockSpec((1, s, B_KV_RANK), allk),
                  pl.BlockSpec((1, nkb, B_KV_RANK + SUM_ROWS, t), lambda bi, qi: (bi, 0, 0, 0)),
                  pl.BlockSpec((B_HEADS, 3, t, t), lambda bi, qi: (0, 0, 0, 0)),
                  pl.BlockSpec((B_HEADS, HEAD_DIM, B_KV_RANK), lambda bi, qi: (0, 0, 0))],
        out_specs=pl.BlockSpec((1, t, B_WIDTH), lambda bi, qi: (bi, qi, 0)),
        out_shape=jax.ShapeDtypeStruct((b, s, B_WIDTH), BF16),
        scratch_shapes=[pltpu.VMEM((nkb, t, t), jnp.int32),
                        pltpu.VMEM((nkb + 1, t, t), jnp.int16), pltpu.VMEM((nkb + 1, t, t), jnp.int16),
                        pltpu.VMEM((B_KV_RANK + SUM_ROWS, B_HEADS * t), F32),
                        pltpu.VMEM((t, B_HEADS * t), F32), pltpu.VMEM((t, B_HEADS * t), F32),
                        pltpu.VMEM((t, B_HEADS * t), BF16), pltpu.VMEM((t, B_HEADS * t), BF16)],
        compiler_params=_params(("parallel", "arbitrary"), 56),
        name="dsa_attention",
    )(qlat_t, qidx_t, wis_t, kk, ckvn, ckvn_t, bias, wuv)


def _gelu(x):
    return x * (0.5 * (1.0 + jnp.tanh(math.sqrt(2.0 / math.pi) * (x + 0.044715 * (x * x * x)))))


def _gmlp_kernel(cu_ref, cv_ref, g_ref, b_ref, ws_ref, bs_ref, o_ref, *, ts):
    u = _gelu(cu_ref[...])
    v = _gelu(cv_ref[...])
    mu = jnp.mean(v, axis=-1, keepdims=True)
    vc = v - mu
    var = jnp.mean(vc * vc, axis=-1, keepdims=True)
    vn = (vc * lax.rsqrt(var + EPS) * g_ref[...] + b_ref[...]).astype(BF16)
    row = lax.broadcasted_iota(jnp.int32, (CHUNK, CHUNK), 0)
    col = lax.broadcasted_iota(jnp.int32, (CHUNK, CHUNK), 1)
    for g in range(C_GROUPS):
        w = jnp.where(row >= col, ws_ref[g], 0.0).astype(BF16)
        bcol = bs_ref[:, g:g + 1]
        cs = slice(g * 128, (g + 1) * 128)
        for c in range(ts // CHUNK):
            rs = slice(c * CHUNK, (c + 1) * CHUNK)
            y = _dot(w, vn[rs, cs]) + bcol
            o_ref[rs, cs] = (u[rs, cs] * y).astype(o_ref.dtype)


def _gmlp(rest, ln_g, ln_b, w_s, b_s, ts):
    m = rest.shape[0]
    kern = functools.partial(_gmlp_kernel, ts=ts)
    return pl.pallas_call(
        kern,
        grid=(m // ts,),
        in_specs=[pl.BlockSpec((ts, C_WIDTH), lambda i: (i, 0)),
                  pl.BlockSpec((ts, C_WIDTH), lambda i: (i, 1)),
                  pl.BlockSpec((1, C_WIDTH), lambda i: (0, 0)),
                  pl.BlockSpec((1, C_WIDTH), lambda i: (0, 0)),
                  pl.BlockSpec((C_GROUPS, CHUNK, CHUNK), lambda i: (0, 0, 0)),
                  pl.BlockSpec((CHUNK, C_GROUPS), lambda i: (0, 0))],
        out_specs=pl.BlockSpec((ts, C_WIDTH), lambda i: (i, 0)),
        out_shape=jax.ShapeDtypeStruct((m, C_WIDTH), BF16),
        compiler_params=_params(("parallel",), 32),
        name="gmlp",
    )(rest, rest, ln_g.reshape(1, -1), ln_b.reshape(1, -1), w_s, jnp.transpose(b_s))


def _out_proj_kernel(h_ref, oa_ref, ob_ref, oc_ref, wa_ref, wb_ref, wc_ref, g_ref, o_ref):
    y = _dot(oa_ref[...], wa_ref[...]) + _dot(ob_ref[...], wb_ref[...]) + _dot(oc_ref[...], wc_ref[...])
    o_ref[...] = h_ref[...] + _rms(y, g_ref[...])


def _out_proj(h, oa, ob, oc, wa, wb, wc, g, tm):
    m, d = h.shape
    rowblk = lambda i: (i, 0)
    full = lambda i: (0, 0)
    return pl.pallas_call(
        _out_proj_kernel,
        grid=(m // tm,),
        in_specs=[pl.BlockSpec((tm, d), rowblk),
                  pl.BlockSpec((tm, A_WIDTH), rowblk),
                  pl.BlockSpec((tm, B_WIDTH), rowblk),
                  pl.BlockSpec((tm, C_WIDTH), rowblk),
                  pl.BlockSpec((A_WIDTH, d), full),
                  pl.BlockSpec((B_WIDTH, d), full),
                  pl.BlockSpec((C_WIDTH, d), full),
                  pl.BlockSpec((1, d), full)],
        out_specs=pl.BlockSpec((tm, d), rowblk),
        out_shape=jax.ShapeDtypeStruct((m, d), F32),
        compiler_params=_params(("parallel",), 48),
        name="out_proj",
    )(h, oa, ob, oc, wa, wb, wc, g.reshape(1, d))


def _xattn_kernel(h_ref, gpre_ref, wq_ref, k_ref, v_ref, wo_ref, gpost_ref, o_ref):
    x = h_ref[0]
    xn = _rms(x, gpre_ref[...]).astype(BF16)
    q = _dot(xn, wq_ref[...])
    scale = X_DIM ** -0.5
    outs = []
    for hh in range(X_HEADS):
        cs = slice(hh * X_DIM, (hh + 1) * X_DIM)
        s = _dot_nt(q[:, cs].astype(BF16), k_ref[0, :, cs]) * scale
        p = jnp.exp(s - jnp.max(s, axis=-1, keepdims=True))
        p = p * (1.0 / jnp.sum(p, axis=-1, keepdims=True))
        outs.append(_dot(p.astype(BF16), v_ref[0, :, cs]).astype(BF16))
    o = jnp.concatenate(outs, axis=1)
    xa = _dot(o, wo_ref[...])
    o_ref[0] = x + _rms(xa, gpost_ref[...])


def _xattn(h, kv, gpre, wq, wo, gpost, tm):
    b, s, d = h.shape
    mlen = kv.shape[1]
    hw = X_HEADS * X_DIM
    return pl.pallas_call(
        _xattn_kernel,
        grid=(b, s // tm),
        in_specs=[pl.BlockSpec((1, tm, d), lambda bi, i: (bi, i, 0)),
                  pl.BlockSpec((1, d), lambda bi, i: (0, 0)),
                  pl.BlockSpec((d, hw), lambda bi, i: (0, 0)),
                  pl.BlockSpec((1, mlen, hw), lambda bi, i: (bi, 0, 0)),
                  pl.BlockSpec((1, mlen, hw), lambda bi, i: (bi, 0, 1)),
                  pl.BlockSpec((hw, d), lambda bi, i: (0, 0)),
                  pl.BlockSpec((1, d), lambda bi, i: (0, 0))],
        out_specs=pl.BlockSpec((1, tm, d), lambda bi, i: (bi, i, 0)),
        out_shape=jax.ShapeDtypeStruct((b, s, d), F32),
        compiler_params=_params(("parallel", "parallel"), 48),
        name="xattn",
    )(h, gpre.reshape(1, d), wq, kv, kv, wo, gpost.reshape(1, d))


def _ffn_kernel(x_ref, gpre_ref, wg_ref, wu_ref, wd_ref, gpost_ref, o_ref, xn_ref, acc_ref):
    j = pl.program_id(1)

    @pl.when(j == 0)
    def _():
        xn_ref[...] = _rms(x_ref[...], gpre_ref[...]).astype(BF16)
        acc_ref[...] = jnp.zeros(acc_ref.shape, F32)

    xn = xn_ref[...]
    gate = _dot(xn, wg_ref[...])
    up = _dot(xn, wu_ref[...])
    act = gate * (1.0 / (1.0 + jnp.exp(-gate))) * up
    acc_ref[...] += _dot(act.astype(BF16), wd_ref[...])

    @pl.when(j == pl.num_programs(1) - 1)
    def _():
        o_ref[...] = x_ref[...] + _rms(acc_ref[...], gpost_ref[...])


def _ffn(h, gpre, w_gu, w_down, gpost, tm, tf):
    m, d = h.shape
    f = w_down.shape[0]
    nf = f // tf
    return pl.pallas_call(
        _ffn_kernel,
        grid=(m // tm, nf),
        in_specs=[pl.BlockSpec((tm, d), lambda i, j: (i, 0)),
                  pl.BlockSpec((1, d), lambda i, j: (0, 0)),
                  pl.BlockSpec((d, tf), lambda i, j: (0, j)),
                  pl.BlockSpec((d, tf), lambda i, j: (0, nf + j)),
                  pl.BlockSpec((tf, d), lambda i, j: (j, 0)),
                  pl.BlockSpec((1, d), lambda i, j: (0, 0))],
        out_specs=pl.BlockSpec((tm, d), lambda i, j: (i, 0)),
        out_shape=jax.ShapeDtypeStruct((m, d), F32),
        scratch_shapes=[pltpu.VMEM((tm, d), BF16), pltpu.VMEM((tm, d), F32)],
        compiler_params=_params(("parallel", "arbitrary"), 56),
        name="ffn",
    )(h, gpre.reshape(1, d), w_gu, w_gu, w_down, gpost.reshape(1, d))


def _tile(n, pref):
    t = min(pref, n)
    assert n % t == 0
    return t


def kernel(x, mem, rel_bias, mix_pre_g, mix_post_g, w_in, w_out, a_lambda, a_sub_g, b_cq_g, b_ckv_g, b_w_uq, b_w_qidx, b_w_uk, b_w_uv, c_ln_g, c_ln_b, c_w_s, c_b_s, x_pre_g, x_post_g, mem_g, x_wq, x_wkv, x_wo, f_pre_g, f_post_g, f_w_gu, f_w_down):
    bsz, s_len, d = x.shape
    depth = w_in.shape[0]
    tokens = bsz * s_len
    mlen = mem.shape[1]
    t_attn = min(ATTN_BLOCK, s_len)
    bias_a = _bias_tiles(rel_bias[:, :A_HEADS], t_attn)
    bias_b = _bias_tiles(rel_bias[:, A_HEADS:], t_attn)
    sizes = (A_WIDTH, A_WIDTH, A_WIDTH, B_Q_RANK, B_KV_RANK, IDX_DIM, IDX_HEADS, C_WIDTH, C_WIDTH)
    offs = np.concatenate([[0], np.cumsum(sizes)])
    tm = _tile(tokens, 512)

    h = x.reshape(tokens, d)
    mem2 = mem.reshape(bsz * mlen, d)
    for l in range(depth):
        lam_init = 0.8 - 0.6 * math.exp(-0.3 * l)
        wl = w_in[l]
        cols = [wl[:, offs[i]:offs[i + 1]] for i in range(len(sizes))]
        wqa, wka, wva, wcq, wckv, wkidx, wwidx, wcu, wcv = cols
        w_k_rest = jnp.concatenate(
            [wka, wcu, wcv, wcq, wckv, wkidx, wkidx, wwidx,
             jnp.zeros((d, 128 - IDX_HEADS), wl.dtype)], axis=1).astype(BF16)
        wqt = jnp.transpose(wqa * (A_HALF ** -0.5 * LOG2E)).astype(BF16)
        wvt = jnp.transpose(wva).astype(BF16)

        qt, k, vt, rest = _in_proj(h.reshape(bsz, s_len, d), mix_pre_g[l], wqt, wvt, w_k_rest,
                                   t_attn, _tile(s_len, 512))

        lp = a_lambda[l].astype(F32)
        lam = jnp.exp(jnp.sum(lp[0] * lp[1])) - jnp.exp(jnp.sum(lp[2] * lp[3])) + lam_init
        oa = _diff_attention(qt, k, vt, lam.reshape(1), bias_a, a_sub_g[l], 1.0 - lam_init,
                             DIFF_HEADS_PER_STEP)

        wuq = b_w_uq[l].reshape(B_Q_RANK, B_WIDTH).astype(BF16)
        wuk = jnp.transpose(b_w_uk[l], (1, 0, 2)).astype(BF16)
        wqit = jnp.transpose(b_w_qidx[l].reshape(B_Q_RANK, IDX_HEADS * IDX_DIM)).astype(BF16)
        wuv = jnp.transpose(b_w_uv[l], (1, 2, 0)).astype(BF16)
        prep = _dsa_prep(rest, b_cq_g[l], b_ckv_g[l], wuq, wuk, wqit, t_attn, _tile(s_len, 512))
        ob = _dsa_attention(*prep, bias_b, wuv)

        rest = rest.reshape(tokens, rest.shape[-1])
        oc = _gmlp(rest, c_ln_g[l], c_ln_b[l], c_w_s[l], c_b_s[l], tm)

        wo_l = w_out[l].astype(BF16)
        h = _out_proj(h, oa.reshape(tokens, A_WIDTH), ob.reshape(tokens, B_WIDTH), oc,
                      wo_l[:A_WIDTH], wo_l[A_WIDTH:A_WIDTH + B_WIDTH], wo_l[A_WIDTH + B_WIDTH:],
                      mix_post_g[l], tm)

        hw = X_HEADS * X_DIM
        kv = _norm_matmul(mem2, mem_g[l], x_wkv[l].reshape(d, 2 * hw).astype(BF16), BF16,
                          _tile(bsz * mlen, 512), 512)
        h = _xattn(h.reshape(bsz, s_len, d), kv.reshape(bsz, mlen, 2 * hw), x_pre_g[l],
                   x_wq[l].reshape(d, hw).astype(BF16), x_wo[l].reshape(hw, d).astype(BF16),
                   x_post_g[l], _tile(s_len, 512)).reshape(tokens, d)

        fh = f_w_down.shape[1]
        h = _ffn(h, f_pre_g[l], f_w_gu[l].reshape(d, 2 * fh).astype(BF16),
                 f_w_down[l].astype(BF16), f_post_g[l], tm, 512)
    return h.reshape(bsz, s_len, d)
```

```python
import functools
import math

import numpy as np
import jax
import jax.numpy as jnp
from jax import lax
from jax.experimental import pallas as pl
from jax.experimental.pallas import tpu as pltpu

F32 = jnp.float32
BF16 = jnp.bfloat16
EPS = 1e-6
NEG = -1e30
INT_MIN = -(2 ** 31)
I16_MIN = -(2 ** 15)

HEAD_DIM = 128
A_HEADS = 6
A_HALF = 64
B_HEADS = 6
B_Q_RANK = 512
B_KV_RANK = 256
IDX_HEADS = 16
IDX_DIM = 64
TOPK_MAX = 256
C_GROUPS = 4
C_WIDTH = 512
CHUNK = 128
X_HEADS = 4
X_DIM = 128
REL_BUCKETS = 32
REL_MAX_DIST = 128
A_WIDTH = A_HEADS * HEAD_DIM
B_WIDTH = B_HEADS * HEAD_DIM

ATTN_BLOCK = 256
DIFF_HEADS_PER_STEP = 6
SCAN_UNROLL = 2
MIB = 1024 * 1024


def _params(semantics, vmem_mib):
    return pltpu.CompilerParams(dimension_semantics=semantics,
                                vmem_limit_bytes=vmem_mib * MIB)


def _rms(x, g):
    return x * lax.rsqrt(jnp.mean(x * x, axis=-1, keepdims=True) + EPS) * g


def _dot(a, b):
    return jnp.dot(a, b, preferred_element_type=F32)


def _dot_nt(a, b):
    return lax.dot_general(a, b, (((1,), (1,)), ((), ())), preferred_element_type=F32)


def _norm_matmul_kernel(x_ref, g_ref, w_ref, o_ref, xn_ref):
    @pl.when(pl.program_id(1) == 0)
    def _():
        xn_ref[...] = _rms(x_ref[...], g_ref[...]).astype(BF16)

    o_ref[...] = _dot(xn_ref[...], w_ref[...]).astype(o_ref.dtype)


def _norm_matmul(x, g, w, out_dtype, tm, tn):
    m, d = x.shape
    n = w.shape[1]
    return pl.pallas_call(
        _norm_matmul_kernel,
        grid=(m // tm, n // tn),
        in_specs=[pl.BlockSpec((tm, d), lambda i, j: (i, 0)),
                  pl.BlockSpec((1, d), lambda i, j: (0, 0)),
                  pl.BlockSpec((d, tn), lambda i, j: (0, j))],
        out_specs=pl.BlockSpec((tm, tn), lambda i, j: (i, j)),
        out_shape=jax.ShapeDtypeStruct((m, n), out_dtype),
        scratch_shapes=[pltpu.VMEM((tm, d), BF16)],
        compiler_params=_params(("parallel", "arbitrary"), 48),
        name="norm_matmul",
    )(x, g.reshape(1, d), w)


SUM_ROWS = 16
IN_PROJ_CHUNK = 1024


def _sum_rows(lead, t, dtype):
    first = lax.broadcasted_iota(jnp.int32, lead + (SUM_ROWS, t), len(lead)) == 0
    return jnp.where(first, 1.0, 0.0).astype(dtype)


def _in_proj_kernel(x_ref, g_ref, wqt_ref, wvt_ref, w_ref, qt_ref, k_ref, vt_ref, rest_ref, *, t):
    xn = _rms(x_ref[0], g_ref[...]).astype(BF16)
    tm = xn.shape[0]
    n_k = k_ref.shape[2]
    k_ref[0] = _dot(xn, w_ref[:, :n_k]).astype(k_ref.dtype)
    for c0 in range(0, rest_ref.shape[2], IN_PROJ_CHUNK):
        c1 = min(c0 + IN_PROJ_CHUNK, rest_ref.shape[2])
        rest_ref[0, :, c0:c1] = _dot(xn, w_ref[:, n_k + c0:n_k + c1])
    qt_ref[0] = _dot_nt(wqt_ref[...], xn).astype(qt_ref.dtype)
    vt = _dot_nt(wvt_ref[...], xn)
    heads, dv = vt_ref.shape[1], vt_ref.shape[3] - SUM_ROWS
    for h in range(heads):
        for blk in range(tm // t):
            vt_ref[0, h, blk, :dv, :] = vt[h * dv:(h + 1) * dv, blk * t:(blk + 1) * t].astype(vt_ref.dtype)
    vt_ref[0, :, :, dv:, :] = _sum_rows((heads, tm // t), t, vt_ref.dtype)


def _in_proj(x, g, wqt, wvt, w, t, tm):
    b, s, d = x.shape
    n_rest = w.shape[1] - A_WIDTH
    resident = lambda shape: pl.BlockSpec(shape, lambda bi, i: (0,) * len(shape), pipeline_mode=pl.Buffered(1))
    kern = functools.partial(_in_proj_kernel, t=t)
    return pl.pallas_call(
        kern,
        grid=(b, s // tm),
        in_specs=[pl.BlockSpec((1, tm, d), lambda bi, i: (bi, i, 0)),
                  pl.BlockSpec((1, d), lambda bi, i: (0, 0)),
                  resident((A_WIDTH, d)), resident((A_WIDTH, d)), resident((d, w.shape[1]))],
        out_specs=[pl.BlockSpec((1, A_WIDTH, tm), lambda bi, i: (bi, 0, i)),
                   pl.BlockSpec((1, tm, A_WIDTH), lambda bi, i: (bi, i, 0)),
                   pl.BlockSpec((1, A_HEADS, tm // t, HEAD_DIM + SUM_ROWS, t), lambda bi, i: (bi, 0, i, 0, 0)),
                   pl.BlockSpec((1, tm, n_rest), lambda bi, i: (bi, i, 0))],
        out_shape=[jax.ShapeDtypeStruct((b, A_WIDTH, s), BF16),
                   jax.ShapeDtypeStruct((b, s, A_WIDTH), BF16),
                   jax.ShapeDtypeStruct((b, A_HEADS, s // t, HEAD_DIM + SUM_ROWS, t), BF16),
                   jax.ShapeDtypeStruct((b, s, n_rest), F32)],
        compiler_params=_params(("parallel", "parallel"), 56),
        name="in_proj",
    )(x, g.reshape(1, d), wqt, wvt, w)


LOG2E = math.log2(math.e)


def _attend_causal_blocks(qi, t, n_col, lhs_block, rhs_col, values_at,
                          far_addend, prev_addend, diag_addend,
                          acc_ref, s0_ref, s1_ref, pa_ref, pb_ref):
    nq = n_col * t
    n_far_pairs = qi // 2
    pad = 2 * n_far_pairs - (qi - 1)

    def block_of(j):
        return jnp.clip(j - pad, 0, qi)

    def cols(c):
        return slice(c * t, (c + 1) * t)

    def scores_into(ref, j, c):
        ref[:, cols(c)] = _dot(lhs_block(block_of(j), c), rhs_col(c))

    def accumulate(alpha, j, p_ref, c):
        acc_ref[:, cols(c)] = (alpha[:, cols(c)] * acc_ref[:, cols(c)]
                               + _dot(values_at(block_of(j), c), p_ref[:, cols(c)]))

    def probs_into(p_ref, s_ref, m, addend, c):
        s = s_ref[:, cols(c)]
        if addend is not None:
            s = s + addend(c)
        m_new = jnp.maximum(m[:, cols(c)], jnp.max(s, axis=0, keepdims=True))
        p_ref[:, cols(c)] = jnp.exp2(s - m_new).astype(BF16)
        return m_new, jnp.exp2(m[:, cols(c)] - m_new)

    def step(j, carry, addend_a, addend_b):
        m, alpha_late = carry
        add_a, add_b = addend_a(block_of(j)), addend_b(block_of(j + 1))
        m_a, alpha_a = [], []
        for c in range(n_col):
            accumulate(alpha_late, j - 1, pb_ref, c)
            scores_into(s1_ref, j + 1, c)
            m_c, alpha_c = probs_into(pa_ref, s0_ref, m, add_a, c)
            m_a.append(m_c)
            alpha_a.append(alpha_c)
        m, alpha = jnp.concatenate(m_a, axis=1), jnp.concatenate(alpha_a, axis=1)
        m_b, alpha_b = [], []
        for c in range(n_col):
            accumulate(alpha, j, pa_ref, c)
            scores_into(s0_ref, j + 2, c)
            m_c, alpha_c = probs_into(pb_ref, s1_ref, m, add_b, c)
            m_b.append(m_c)
            alpha_b.append(alpha_c)
        m, alpha_late = jnp.concatenate(m_b, axis=1), jnp.concatenate(alpha_b, axis=1)
        return m, jnp.where(j < pad, 0.0, alpha_late)

    acc_ref[...] = jnp.zeros(acc_ref.shape, F32)
    pb_ref[...] = jnp.zeros(pb_ref.shape, BF16)
    for c in range(n_col):
        scores_into(s0_ref, 0, c)
    carry = (jnp.full((1, nq), NEG, F32), jnp.ones((1, nq), F32))
    carry = lax.fori_loop(0, n_far_pairs, lambda i, c: step(2 * i, c, far_addend, far_addend), carry)
    j_near = 2 * n_far_pairs
    _, alpha_late = step(j_near, carry, prev_addend, diag_addend)
    for c in range(n_col):
        accumulate(alpha_late, j_near + 1, pb_ref, c)


def _bucket_np(dist):
    n = np.maximum(dist, 0)
    max_exact = REL_BUCKETS // 2
    nf = np.maximum(n, 1).astype(np.float64)
    large = max_exact + (np.log(nf / max_exact) / math.log(REL_MAX_DIST / max_exact)
                         * (REL_BUCKETS - max_exact)).astype(np.int32)
    large = np.minimum(large, REL_BUCKETS - 1)
    return np.where(n < max_exact, n, large)


NO_BLOCK, PREV_BLOCK, DIAG_BLOCK = 0, 1, 2


def _bias_tiles(tab, t):
    assert t >= REL_MAX_DIST
    length = 2 * t
    bucket = jnp.asarray(_bucket_np(np.arange(length)))
    by_dist = ((jnp.take(tab, bucket, axis=0) - tab[REL_BUCKETS - 1]) * LOG2E).T

    def skew(v):
        flat = jnp.tile(v, (1, t))[:, :t * (length - 1)]
        return flat.reshape(-1, t, length - 1)[:, :, :t]

    diag = skew(by_dist)
    prev = skew(jnp.roll(by_dist, -t, axis=1))
    key = np.arange(t)[:, None]
    query = np.arange(t)[None, :]
    diag = jnp.where(jnp.asarray(query >= key)[None], diag, NEG)
    return jnp.stack([jnp.full_like(prev, NEG), prev, diag], axis=1).astype(F32)


def _diff_attn_kernel(lam_ref, qt_ref, k_ref, vt_ref, bias_ref, g_ref, o_ref, acc_ref, s0_ref, s1_ref, pa_ref, pb_ref,
                      *, t, heads, out_scale):
    qi = pl.program_id(2)
    sub = lax.broadcasted_iota(jnp.int32, (HEAD_DIM, t), 0)
    rhs = []
    for g in range(heads):
        qt = qt_ref[0, g * HEAD_DIM:(g + 1) * HEAD_DIM, :]
        zero = jnp.zeros_like(qt)
        rhs += [jnp.where(sub < A_HALF, qt, zero), jnp.where(sub >= A_HALF, qt, zero)]

    def lhs_block(kb, c):
        g = c // 2
        return k_ref[0, pl.ds(pl.multiple_of(kb * t, t), t), g * HEAD_DIM:(g + 1) * HEAD_DIM]

    prev_kind = jnp.where(qi >= 1, PREV_BLOCK, NO_BLOCK)
    _attend_causal_blocks(
        qi, t, 2 * heads, lhs_block, lambda c: rhs[c], lambda kb, c: vt_ref[0, c // 2, kb],
        lambda kb: None,
        lambda kb: (lambda c: bias_ref[c // 2, prev_kind]),
        lambda kb: (lambda c: bias_ref[c // 2, DIAG_BLOCK]),
        acc_ref, s0_ref, s1_ref, pa_ref, pb_ref)

    for g in range(heads):
        acc = acc_ref[:, 2 * g * t:(2 * g + 2) * t]
        o = acc[:HEAD_DIM] * (1.0 / acc[HEAD_DIM:HEAD_DIM + 1])
        o = o[:, :t] - lam_ref[0] * o[:, t:]
        y = o * lax.rsqrt(jnp.mean(o * o, axis=0, keepdims=True) + EPS) * g_ref[...] * out_scale
        o_ref[0, :, g * HEAD_DIM:(g + 1) * HEAD_DIM] = y.T.astype(o_ref.dtype)


def _diff_attention(qt, k, vt, lam, bias, sub_g, out_scale, heads):
    b, s, _ = k.shape
    nkb, t = vt.shape[2], vt.shape[4]
    hw = heads * HEAD_DIM
    nq = 2 * heads * t
    kern = functools.partial(_diff_attn_kernel, t=t, heads=heads, out_scale=out_scale)
    return pl.pallas_call(
        kern,
        grid=(b, A_HEADS // heads, s // t),
        in_specs=[pl.BlockSpec(memory_space=pltpu.SMEM),
                  pl.BlockSpec((1, hw, t), lambda bi, h, qi: (bi, h, qi)),
                  pl.BlockSpec((1, s, hw), lambda bi, h, qi: (bi, 0, h)),
                  pl.BlockSpec((1, heads, nkb, HEAD_DIM + SUM_ROWS, t), lambda bi, h, qi: (bi, h, 0, 0, 0)),
                  pl.BlockSpec((heads, 3, t, t), lambda bi, h, qi: (h, 0, 0, 0)),
                  pl.BlockSpec((HEAD_DIM, 1), lambda bi, h, qi: (0, 0))],
        out_specs=pl.BlockSpec((1, t, hw), lambda bi, h, qi: (bi, qi, h)),
        out_shape=jax.ShapeDtypeStruct((b, s, A_WIDTH), BF16),
        scratch_shapes=[pltpu.VMEM((HEAD_DIM + SUM_ROWS, nq), F32),
                        pltpu.VMEM((t, nq), F32), pltpu.VMEM((t, nq), F32),
                        pltpu.VMEM((t, nq), BF16), pltpu.VMEM((t, nq), BF16)],
        compiler_params=_params(("parallel", "parallel", "arbitrary"), 56),
        name="diff_attention",
    )(lam, qt, k, vt, bias, sub_g.reshape(HEAD_DIM, 1))


def _dsa_prep_kernel(cq_ref, ckv_ref, kk_ref, wi_ref, gq_ref, gkv_ref, wuq_ref, wuk_ref, wqit_ref,
                     qlat_ref, qidx_ref, wis_ref, kkb_ref, ckvn_ref, ckvnt_ref, *, t):
    cqn = _rms(cq_ref[0], gq_ref[...]).astype(BF16)
    tm = cqn.shape[0]
    q = _dot(cqn, wuq_ref[...])
    for h in range(B_HEADS):
        qh = q[:, h * HEAD_DIM:(h + 1) * HEAD_DIM].astype(BF16)
        qlat = (_dot_nt(wuk_ref[h], qh) * (HEAD_DIM ** -0.5 * LOG2E)).astype(BF16)
        for blk in range(tm // t):
            qlat_ref[0, blk, :, h * t:(h + 1) * t] = qlat[:, blk * t:(blk + 1) * t]
    qidx_ref[0] = _dot_nt(wqit_ref[...], cqn).astype(BF16)
    wis = wi_ref[0] * (IDX_HEADS ** -0.5 * IDX_DIM ** -0.5)
    wis_ref[0] = wis.T[:IDX_HEADS]
    kkb_ref[0] = kk_ref[0].astype(BF16)
    ckvn = _rms(ckv_ref[0], gkv_ref[...])
    ckvn_ref[0] = ckvn.astype(BF16)
    ckvn_t = ckvn.T.astype(BF16)
    for blk in range(tm // t):
        ckvnt_ref[0, blk, :B_KV_RANK, :] = ckvn_t[:, blk * t:(blk + 1) * t]
    ckvnt_ref[0, :, B_KV_RANK:, :] = _sum_rows((tm // t,), t, BF16)


def _dsa_prep(rest, gq, gkv, wuq, wuk, wqit, t, tm):
    b, s, _ = rest.shape
    col = lambda c: (lambda bi, i: (bi, i, c))
    const = lambda n: (lambda bi, i: (0,) * n)
    kern = functools.partial(_dsa_prep_kernel, t=t)
    return pl.pallas_call(
        kern,
        grid=(b, s // tm),
        in_specs=[pl.BlockSpec((1, tm, B_Q_RANK), col(2)),
                  pl.BlockSpec((1, tm, B_KV_RANK), col(6)),
                  pl.BlockSpec((1, tm, 128), col(14)),
                  pl.BlockSpec((1, tm, 128), col(15)),
                  pl.BlockSpec((1, B_Q_RANK), const(2)),
                  pl.BlockSpec((1, B_KV_RANK), const(2)),
                  pl.BlockSpec((B_Q_RANK, B_WIDTH), const(2)),
                  pl.BlockSpec((B_HEADS, B_KV_RANK, HEAD_DIM), const(3)),
                  pl.BlockSpec((IDX_HEADS * IDX_DIM, B_Q_RANK), const(2))],
        out_specs=[pl.BlockSpec((1, tm // t, B_KV_RANK, B_HEADS * t), lambda bi, i: (bi, i, 0, 0)),
                   pl.BlockSpec((1, IDX_HEADS * IDX_DIM, tm), lambda bi, i: (bi, 0, i)),
                   pl.BlockSpec((1, IDX_HEADS, tm), lambda bi, i: (bi, 0, i)),
                   pl.BlockSpec((1, tm, 128), lambda bi, i: (bi, i, 0)),
                   pl.BlockSpec((1, tm, B_KV_RANK), lambda bi, i: (bi, i, 0)),
                   pl.BlockSpec((1, tm // t, B_KV_RANK + SUM_ROWS, t), lambda bi, i: (bi, i, 0, 0))],
        out_shape=[jax.ShapeDtypeStruct((b, s // t, B_KV_RANK, B_HEADS * t), BF16),
                   jax.ShapeDtypeStruct((b, IDX_HEADS * IDX_DIM, s), BF16),
                   jax.ShapeDtypeStruct((b, IDX_HEADS, s), F32),
                   jax.ShapeDtypeStruct((b, s, 128), BF16),
                   jax.ShapeDtypeStruct((b, s, B_KV_RANK), BF16),
                   jax.ShapeDtypeStruct((b, s // t, B_KV_RANK + SUM_ROWS, t), BF16)],
        compiler_params=_params(("parallel", "parallel"), 40),
        name="dsa_prep",
    )(rest, rest, rest, rest, gq.reshape(1, -1), gkv.reshape(1, -1), wuq, wuk, wqit)


def _dsa_kernel(qlat_ref, qidx_ref, wis_ref, kk_ref, ckvn_ref, ckvnt_ref, bias_ref, wuv_ref, o_ref,
                key_ref, hi_ref, lo_ref, acc_ref, s0_ref, s1_ref, pa_ref, pb_ref, *, t, topk):
    qi = pl.program_id(1)
    nblk = qi + 1
    nkb = key_ref.shape[0]
    key_pos = lax.broadcasted_iota(jnp.int32, (t, t), 0)
    query_pos = lax.broadcasted_iota(jnp.int32, (t, t), 1)

    qidx = qidx_ref[0]
    wis = wis_ref[0]
    sub = lax.broadcasted_iota(jnp.int32, (2 * IDX_DIM, t), 0)
    q_heads = []
    for g in range(IDX_HEADS):
        pair = qidx[(g // 2) * 2 * IDX_DIM:(g // 2 + 1) * 2 * IDX_DIM]
        keep = (sub < IDX_DIM) if g % 2 == 0 else (sub >= IDX_DIM)
        q_heads.append(jnp.where(keep, pair, jnp.zeros_like(pair)))

    def score_block(kb):
        ks = pl.multiple_of(kb * t, t)
        kblk = kk_ref[0, pl.ds(ks, t), :]
        score = jnp.zeros((t, t), F32)
        for g in range(IDX_HEADS):
            d = _dot(kblk, q_heads[g])
            score = score + jnp.maximum(d, 0.0) * wis[g:g + 1]
        bits = pltpu.bitcast(score, jnp.int32)
        key = jnp.where(bits < 0, bits ^ jnp.int32(0x7FFFFFFF), bits)
        causal = (kb < qi) | (query_pos >= key_pos)
        key = jnp.where(causal, key, jnp.int32(INT_MIN))
        key_ref[kb] = key
        hi_ref[kb] = (key >> 16).astype(jnp.int16)
        lo_ref[kb] = ((key & 0xFFFF) + I16_MIN).astype(jnp.int16)

    def score_pair(i, c):
        score_block(2 * i)
        score_block(jnp.minimum(2 * i + 1, qi))
        return c

    lax.fori_loop(0, (nblk + 1) // 2, score_pair, 0)

    hi_ref[nkb] = jnp.full((t, t), I16_MIN, jnp.int16)
    lo_ref[nkb] = jnp.full((t, t), I16_MIN, jnp.int16)
    one, zero = jnp.int16(1), jnp.int16(0)

    def count_ge(plane_ref, cand):
        cand_b = jnp.broadcast_to(cand.astype(jnp.int16), (t, t))

        def body(i, c):
            for u in range(SCAN_UNROLL):
                kb = SCAN_UNROLL * i + u
                x = jnp.where(plane_ref[jnp.where(kb < nblk, kb, nkb)] >= cand_b, one, zero)
                parts = [x[r * 16:(r + 1) * 16] for r in range(t // 16)]
                while len(parts) > 1:
                    parts = [a + b for a, b in zip(parts[0::2], parts[1::2])]
                c = c + parts[0]
            return c

        trips = (nblk + SCAN_UNROLL - 1) // SCAN_UNROLL
        c = lax.fori_loop(0, trips, body, jnp.zeros((16, t), jnp.int16))
        return jnp.sum(c.astype(jnp.int32), axis=0, keepdims=True)

    def greedy_bits(count_at_least):
        def bit_step(i, v):
            cand = v + jnp.left_shift(jnp.int32(1), 15 - i)
            return jnp.where(count_at_least(cand) >= topk, cand, v)
        return lax.fori_loop(0, 16, bit_step, jnp.full((1, t), I16_MIN, jnp.int32))

    hi = greedy_bits(lambda cand: count_ge(hi_ref, cand))
    above = count_ge(hi_ref, hi + 1)
    hi_b = jnp.broadcast_to(hi.astype(jnp.int16), (t, t))

    def restrict(kb, c):
        lo_ref[kb] = jnp.where(hi_ref[kb] == hi_b, lo_ref[kb], jnp.int16(I16_MIN))
        return c

    lax.fori_loop(0, nblk, restrict, 0)
    lo = greedy_bits(lambda cand: above + count_ge(lo_ref, cand))
    thr = jnp.left_shift(hi, 16) + (lo - I16_MIN)
    thr_b = jnp.broadcast_to(jnp.maximum(thr, jnp.int32(INT_MIN + 1)), (t, t))

    def count_keys(pred):
        def body(kb, c):
            x = jnp.where(pred(key_ref[kb], kb * t), 1, 0)
            return c + jnp.sum(x.reshape(t // 8, 8, t), axis=0)
        c = lax.fori_loop(0, nblk, body, jnp.zeros((8, t), jnp.int32))
        return jnp.sum(c, axis=0, keepdims=True)

    selected = count_keys(lambda key, base: key >= thr_b)

    @pl.when(jnp.max(selected) > topk)
    def _():
        need = topk - count_keys(lambda key, base: key > thr_b)

        def index_bit(i, cut):
            cand = cut + jnp.left_shift(jnp.int32(1), index_bits - 1 - i)
            cand_b = jnp.broadcast_to(cand, (t, t))
            below = count_keys(lambda key, base: (key == thr_b) & (key_pos + base < cand_b))
            return jnp.where(below < need, cand, cut)

        index_bits = max(1, (nkb * t - 1).bit_length())
        cut = lax.fori_loop(0, index_bits, index_bit, jnp.zeros((1, t), jnp.int32))
        cut_b = jnp.broadcast_to(cut, (t, t))

        def demote(kb, c):
            key = key_ref[kb]
            key_ref[kb] = jnp.where((key == thr_b) & (key_pos + kb * t > cut_b), key - 1, key)
            return c

        lax.fori_loop(0, nblk, demote, 0)

    def lhs_block(kb, h):
        return ckvn_ref[0, pl.ds(pl.multiple_of(kb * t, t), t), :]

    def selection(kb):
        return jnp.where(key_ref[kb] >= thr_b, 0.0, NEG)

    def far_addend(kb):
        mask = selection(kb)
        return lambda h: mask

    def near_addend(kind):
        def addend(kb):
            mask = selection(kb)
            return lambda h: bias_ref[h, kind] + mask
        return addend

    _attend_causal_blocks(
        qi, t, B_HEADS, lhs_block, lambda h: qlat_ref[0, 0, :, h * t:(h + 1) * t],
        lambda kb, h: ckvnt_ref[0, kb],
        far_addend, near_addend(jnp.where(qi >= 1, PREV_BLOCK, NO_BLOCK)), near_addend(DIAG_BLOCK),
        acc_ref, s0_ref, s1_ref, pa_ref, pb_ref)

    for h in range(B_HEADS):
        cols = slice(h * t, (h + 1) * t)
        o_lat = (acc_ref[:B_KV_RANK, cols] * (1.0 / acc_ref[B_KV_RANK:B_KV_RANK + 1, cols])).astype(BF16)
        o = _dot(wuv_ref[h], o_lat)
        o_ref[0, :, h * HEAD_DIM:(h + 1) * HEAD_DIM] = o.T.astype(o_ref.dtype)


def _dsa_attention(qlat_t, qidx_t, wis_t, kk, ckvn, ckvn_t, bias, wuv):
    b, s, _ = ckvn.shape
    nkb, t = ckvn_t.shape[1], ckvn_t.shape[3]
    topk = min(TOPK_MAX, s // 4)
    kern = functools.partial(_dsa_kernel, t=t, topk=topk)
    qcol = lambda bi, qi: (bi, 0, qi)
    allk = lambda bi, qi: (bi, 0, 0)
    return pl.pallas_call(
        kern,
        grid=(b, s // t),
        in_specs=[pl.BlockSpec((1, 1, B_KV_RANK, B_HEADS * t), lambda bi, qi: (bi, qi, 0, 0)),
                  pl.BlockSpec((1, IDX_HEADS * IDX_DIM, t), qcol),
                  pl.BlockSpec((1, IDX_HEADS, t), qcol),
                  pl.BlockSpec((1, s, 128), allk),
                  pl.BlockSpec((1, s, B_KV_RANK), allk),
                  pl.BlockSpec((1, nkb, B_KV_RANK + SUM_ROWS, t), lambda bi, qi: (bi, 0, 0, 0)),
                  pl.BlockSpec((B_HEADS, 3, t, t), lambda bi, qi: (0, 0, 0, 0)),
                  pl.BlockSpec((B_HEADS, HEAD_DIM, B_KV_RANK), lambda bi, qi: (0, 0, 0))],
        out_specs=pl.BlockSpec((1, t, B_WIDTH), lambda bi, qi: (bi, qi, 0)),
        out_shape=jax.ShapeDtypeStruct((b, s, B_WIDTH), BF16),
        scratch_shapes=[pltpu.VMEM((nkb, t, t), jnp.int32),
                        pltpu.VMEM((nkb + 1, t, t), jnp.int16), pltpu.VMEM((nkb + 1, t, t), jnp.int16),
                        pltpu.VMEM((B_KV_RANK + SUM_ROWS, B_HEADS * t), F32),
                        pltpu.VMEM((t, B_HEADS * t), F32), pltpu.VMEM((t, B_HEADS * t), F32),
                        pltpu.VMEM((t, B_HEADS * t), BF16), pltpu.VMEM((t, B_HEADS * t), BF16)],
        compiler_params=_params(("parallel", "arbitrary"), 56),
        name="dsa_attention",
    )(qlat_t, qidx_t, wis_t, kk, ckvn, ckvn_t, bias, wuv)


def _gelu(x):
    return x * (0.5 * (1.0 + jnp.tanh(math.sqrt(2.0 / math.pi) * (x + 0.044715 * (x * x * x)))))


def _gmlp_kernel(cu_ref, cv_ref, g_ref, b_ref, ws_ref, bs_ref, o_ref, *, ts):
    u = _gelu(cu_ref[...])
    v = _gelu(cv_ref[...])
    mu = jnp.mean(v, axis=-1, keepdims=True)
    vc = v - mu
    var = jnp.mean(vc * vc, axis=-1, keepdims=True)
    vn = (vc * lax.rsqrt(var + EPS) * g_ref[...] + b_ref[...]).astype(BF16)
    row = lax.broadcasted_iota(jnp.int32, (CHUNK, CHUNK), 0)
    col = lax.broadcasted_iota(jnp.int32, (CHUNK, CHUNK), 1)
    for g in range(C_GROUPS):
        w = jnp.where(row >= col, ws_ref[g], 0.0).astype(BF16)
        bcol = bs_ref[:, g:g + 1]
        cs = slice(g * 128, (g + 1) * 128)
        for c in range(ts // CHUNK):
            rs = slice(c * CHUNK, (c + 1) * CHUNK)
            y = _dot(w, vn[rs, cs]) + bcol
            o_ref[rs, cs] = (u[rs, cs] * y).astype(o_ref.dtype)


def _gmlp(rest, ln_g, ln_b, w_s, b_s, ts):
    m = rest.shape[0]
    kern = functools.partial(_gmlp_kernel, ts=ts)
    return pl.pallas_call(
        kern,
        grid=(m // ts,),
        in_specs=[pl.BlockSpec((ts, C_WIDTH), lambda i: (i, 0)),
                  pl.BlockSpec((ts, C_WIDTH), lambda i: (i, 1)),
                  pl.BlockSpec((1, C_WIDTH), lambda i: (0, 0)),
                  pl.BlockSpec((1, C_WIDTH), lambda i: (0, 0)),
                  pl.BlockSpec((C_GROUPS, CHUNK, CHUNK), lambda i: (0, 0, 0)),
                  pl.BlockSpec((CHUNK, C_GROUPS), lambda i: (0, 0))],
        out_specs=pl.BlockSpec((ts, C_WIDTH), lambda i: (i, 0)),
        out_shape=jax.ShapeDtypeStruct((m, C_WIDTH), BF16),
        compiler_params=_params(("parallel",), 32),
        name="gmlp",
    )(rest, rest, ln_g.reshape(1, -1), ln_b.reshape(1, -1), w_s, jnp.transpose(b_s))


def _out_proj_kernel(h_ref, oa_ref, ob_ref, oc_ref, wa_ref, wb_ref, wc_ref, g_ref, o_ref):
    y = _dot(oa_ref[...], wa_ref[...]) + _dot(ob_ref[...], wb_ref[...]) + _dot(oc_ref[...], wc_ref[...])
    o_ref[...] = h_ref[...] + _rms(y, g_ref[...])


def _out_proj(h, oa, ob, oc, wa, wb, wc, g, tm):
    m, d = h.shape
    rowblk = lambda i: (i, 0)
    full = lambda i: (0, 0)
    return pl.pallas_call(
        _out_proj_kernel,
        grid=(m // tm,),
        in_specs=[pl.BlockSpec((tm, d), rowblk),
                  pl.BlockSpec((tm, A_WIDTH), rowblk),
                  pl.BlockSpec((tm, B_WIDTH), rowblk),
                  pl.BlockSpec((tm, C_WIDTH), rowblk),
                  pl.BlockSpec((A_WIDTH, d), full),
                  pl.BlockSpec((B_WIDTH, d), full),
                  pl.BlockSpec((C_WIDTH, d), full),
                  pl.BlockSpec((1, d), full)],
        out_specs=pl.BlockSpec((tm, d), rowblk),
        out_shape=jax.ShapeDtypeStruct((m, d), F32),
        compiler_params=_params(("parallel",), 48),
        name="out_proj",
    )(h, oa, ob, oc, wa, wb, wc, g.reshape(1, d))


def _xattn_kernel(h_ref, gpre_ref, wq_ref, k_ref, v_ref, wo_ref, gpost_ref, o_ref):
    x = h_ref[0]
    xn = _rms(x, gpre_ref[...]).astype(BF16)
    q = _dot(xn, wq_ref[...])
    scale = X_DIM ** -0.5
    outs = []
    for hh in range(X_HEADS):
        cs = slice(hh * X_DIM, (hh + 1) * X_DIM)
        s = _dot_nt(q[:, cs].astype(BF16), k_ref[0, :, cs]) * scale
        p = jnp.exp(s - jnp.max(s, axis=-1, keepdims=True))
        p = p * (1.0 / jnp.sum(p, axis=-1, keepdims=True))
        outs.append(_dot(p.astype(BF16), v_ref[0, :, cs]).astype(BF16))
    o = jnp.concatenate(outs, axis=1)
    xa = _dot(o, wo_ref[...])
    o_ref[0] = x + _rms(xa, gpost_ref[...])


def _xattn(h, kv, gpre, wq, wo, gpost, tm):
    b, s, d = h.shape
    mlen = kv.shape[1]
    hw = X_HEADS * X_DIM
    return pl.pallas_call(
        _xattn_kernel,
        grid=(b, s // tm),
        in_specs=[pl.BlockSpec((1, tm, d), lambda bi, i: (bi, i, 0)),
                  pl.BlockSpec((1, d), lambda bi, i: (0, 0)),
                  pl.BlockSpec((d, hw), lambda bi, i: (0, 0)),
                  pl.BlockSpec((1, mlen, hw), lambda bi, i: (bi, 0, 0)),
                  pl.BlockSpec((1, mlen, hw), lambda bi, i: (bi, 0, 1)),
                  pl.BlockSpec((hw, d), lambda bi, i: (0, 0)),
                  pl.BlockSpec((1, d), lambda bi, i: (0, 0))],
        out_specs=pl.BlockSpec((1, tm, d), lambda bi, i: (bi, i, 0)),
        out_shape=jax.ShapeDtypeStruct((b, s, d), F32),
        compiler_params=_params(("parallel", "parallel"), 48),
        name="xattn",
    )(h, gpre.reshape(1, d), wq, kv, kv, wo, gpost.reshape(1, d))


def _ffn_kernel(x_ref, gpre_ref, wg_ref, wu_ref, wd_ref, gpost_ref, o_ref, xn_ref, acc_ref):
    j = pl.program_id(1)

    @pl.when(j == 0)
    def _():
        xn_ref[...] = _rms(x_ref[...], gpre_ref[...]).astype(BF16)
        acc_ref[...] = jnp.zeros(acc_ref.shape, F32)

    xn = xn_ref[...]
    gate = _dot(xn, wg_ref[...])
    up = _dot(xn, wu_ref[...])
    act = gate * (1.0 / (1.0 + jnp.exp(-gate))) * up
    acc_ref[...] += _dot(act.astype(BF16), wd_ref[...])

    @pl.when(j == pl.num_programs(1) - 1)
    def _():
        o_ref[...] = x_ref[...] + _rms(acc_ref[...], gpost_ref[...])


def _ffn(h, gpre, w_gu, w_down, gpost, tm, tf):
    m, d = h.shape
    f = w_down.shape[0]
    nf = f // tf
    return pl.pallas_call(
        _ffn_kernel,
        grid=(m // tm, nf),
        in_specs=[pl.BlockSpec((tm, d), lambda i, j: (i, 0)),
                  pl.BlockSpec((1, d), lambda i, j: (0, 0)),
                  pl.BlockSpec((d, tf), lambda i, j: (0, j)),
                  pl.BlockSpec((d, tf), lambda i, j: (0, nf + j)),
                  pl.BlockSpec((tf, d), lambda i, j: (j, 0)),
                  pl.BlockSpec((1, d), lambda i, j: (0, 0))],
        out_specs=pl.BlockSpec((tm, d), lambda i, j: (i, 0)),
        out_shape=jax.ShapeDtypeStruct((m, d), F32),
        scratch_shapes=[pltpu.VMEM((tm, d), BF16), pltpu.VMEM((tm, d), F32)],
        compiler_params=_params(("parallel", "arbitrary"), 56),
        name="ffn",
    )(h, gpre.reshape(1, d), w_gu, w_gu, w_down, gpost.reshape(1, d))


def _tile(n, pref):
    t = min(pref, n)
    assert n % t == 0
    return t


def kernel(x, mem, rel_bias, mix_pre_g, mix_post_g, w_in, w_out, a_lambda, a_sub_g, b_cq_g, b_ckv_g, b_w_uq, b_w_qidx, b_w_uk, b_w_uv, c_ln_g, c_ln_b, c_w_s, c_b_s, x_pre_g, x_post_g, mem_g, x_wq, x_wkv, x_wo, f_pre_g, f_post_g, f_w_gu, f_w_down):
    bsz, s_len, d = x.shape
    depth = w_in.shape[0]
    tokens = bsz * s_len
    mlen = mem.shape[1]
    t_attn = min(ATTN_BLOCK, s_len)
    bias_a = _bias_tiles(rel_bias[:, :A_HEADS], t_attn)
    bias_b = _bias_tiles(rel_bias[:, A_HEADS:], t_attn)
    sizes = (A_WIDTH, A_WIDTH, A_WIDTH, B_Q_RANK, B_KV_RANK, IDX_DIM, IDX_HEADS, C_WIDTH, C_WIDTH)
    offs = np.concatenate([[0], np.cumsum(sizes)])
    tm = _tile(tokens, 512)

    h = x.reshape(tokens, d)
    mem2 = mem.reshape(bsz * mlen, d)
    for l in range(depth):
        lam_init = 0.8 - 0.6 * math.exp(-0.3 * l)
        wl = w_in[l]
        cols = [wl[:, offs[i]:offs[i + 1]] for i in range(len(sizes))]
        wqa, wka, wva, wcq, wckv, wkidx, wwidx, wcu, wcv = cols
        w_k_rest = jnp.concatenate(
            [wka, wcu, wcv, wcq, wckv, wkidx, wkidx, wwidx,
             jnp.zeros((d, 128 - IDX_HEADS), wl.dtype)], axis=1).astype(BF16)
        wqt = jnp.transpose(wqa * (A_HALF ** -0.5 * LOG2E)).astype(BF16)
        wvt = jnp.transpose(wva).astype(BF16)

        qt, k, vt, rest = _in_proj(h.reshape(bsz, s_len, d), mix_pre_g[l], wqt, wvt, w_k_rest,
                                   t_attn, _tile(s_len, 512))

        lp = a_lambda[l].astype(F32)
        lam = jnp.exp(jnp.sum(lp[0] * lp[1])) - jnp.exp(jnp.sum(lp[2] * lp[3])) + lam_init
        oa = _diff_attention(qt, k, vt, lam.reshape(1), bias_a, a_sub_g[l], 1.0 - lam_init,
                             DIFF_HEADS_PER_STEP)

        wuq = b_w_uq[l].reshape(B_Q_RANK, B_WIDTH).astype(BF16)
        wuk = jnp.transpose(b_w_uk[l], (1, 0, 2)).astype(BF16)
        wqit = jnp.transpose(b_w_qidx[l].reshape(B_Q_RANK, IDX_HEADS * IDX_DIM)).astype(BF16)
        wuv = jnp.transpose(b_w_uv[l], (1, 2, 0)).astype(BF16)
        prep = _dsa_prep(rest, b_cq_g[l], b_ckv_g[l], wuq, wuk, wqit, t_attn, _tile(s_len, 512))
        ob = _dsa_attention(*prep, bias_b, wuv)

        rest = rest.reshape(tokens, rest.shape[-1])
        oc = _gmlp(rest, c_ln_g[l], c_ln_b[l], c_w_s[l], c_b_s[l], tm)

        wo_l = w_out[l].astype(BF16)
        h = _out_proj(h, oa.reshape(tokens, A_WIDTH), ob.reshape(tokens, B_WIDTH), oc,
                      wo_l[:A_WIDTH], wo_l[A_WIDTH:A_WIDTH + B_WIDTH], wo_l[A_WIDTH + B_WIDTH:],
                      mix_post_g[l], tm)

        hw = X_HEADS * X_DIM
        kv = _norm_matmul(mem2, mem_g[l], x_wkv[l].reshape(d, 2 * hw).astype(BF16), BF16,
                          _tile(bsz * mlen, 512), 512)
        h = _xattn(h.reshape(bsz, s_len, d), kv.reshape(bsz, mlen, 2 * hw), x_pre_g[l],
                   x_wq[l].reshape(d, hw).astype(BF16), x_wo[l].reshape(hw, d).astype(BF16),
                   x_post_g[l], _tile(s_len, 512)).reshape(tokens, d)

        fh = f_w_down.shape[1]
        h = _ffn(h, f_pre_g[l], f_w_gu[l].reshape(d, 2 * fh).astype(BF16),
                 f_w_down[l].astype(BF16), f_post_g[l], tm, 512)
    return h.reshape(bsz, s_len, d)
```

```python
import functools
import math

import numpy as np
import jax
import jax.numpy as jnp
from jax import lax
from jax.experimental import pallas as pl
from jax.experimental.pallas import tpu as pltpu

F32 = jnp.float32
BF16 = jnp.bfloat16
EPS = 1e-6
NEG = -1e30
INT_MIN = -(2 ** 31)
I16_MIN = -(2 ** 15)

HEAD_DIM = 128
A_HEADS = 6
A_HALF = 64
B_HEADS = 6
B_Q_RANK = 512
B_KV_RANK = 256
IDX_HEADS = 16
IDX_DIM = 64
TOPK_MAX = 256
C_GROUPS = 4
C_WIDTH = 512
CHUNK = 128
X_HEADS = 4
X_DIM = 128
REL_BUCKETS = 32
REL_MAX_DIST = 128
A_WIDTH = A_HEADS * HEAD_DIM
B_WIDTH = B_HEADS * HEAD_DIM

ATTN_BLOCK = 256
DIFF_HEADS_PER_STEP = 6
SCAN_UNROLL = 2
MIB = 1024 * 1024


def _params(semantics, vmem_mib):
    return pltpu.CompilerParams(dimension_semantics=semantics,
                                vmem_limit_bytes=vmem_mib * MIB)


def _rms(x, g):
    return x * lax.rsqrt(jnp.mean(x * x, axis=-1, keepdims=True) + EPS) * g


def _dot(a, b):
    return jnp.dot(a, b, preferred_element_type=F32)


def _dot_nt(a, b):
    return lax.dot_general(a, b, (((1,), (1,)), ((), ())), preferred_element_type=F32)


def _norm_matmul_kernel(x_ref, g_ref, w_ref, o_ref, xn_ref):
    @pl.when(pl.program_id(1) == 0)
    def _():
        xn_ref[...] = _rms(x_ref[...], g_ref[...]).astype(BF16)

    o_ref[...] = _dot(xn_ref[...], w_ref[...]).astype(o_ref.dtype)


def _norm_matmul(x, g, w, out_dtype, tm, tn):
    m, d = x.shape
    n = w.shape[1]
    return pl.pallas_call(
        _norm_matmul_kernel,
        grid=(m // tm, n // tn),
        in_specs=[pl.BlockSpec((tm, d), lambda i, j: (i, 0)),
                  pl.BlockSpec((1, d), lambda i, j: (0, 0)),
                  pl.BlockSpec((d, tn), lambda i, j: (0, j))],
        out_specs=pl.BlockSpec((tm, tn), lambda i, j: (i, j)),
        out_shape=jax.ShapeDtypeStruct((m, n), out_dtype),
        scratch_shapes=[pltpu.VMEM((tm, d), BF16)],
        compiler_params=_params(("parallel", "arbitrary"), 48),
        name="norm_matmul",
    )(x, g.reshape(1, d), w)


SUM_ROWS = 16
IN_PROJ_CHUNK = 1024


def _sum_rows(lead, t, dtype):
    first = lax.broadcasted_iota(jnp.int32, lead + (SUM_ROWS, t), len(lead)) == 0
    return jnp.where(first, 1.0, 0.0).astype(dtype)


def _in_proj_kernel(x_ref, g_ref, wqt_ref, wvt_ref, w_ref, qt_ref, k_ref, vt_ref, rest_ref, *, t):
    xn = _rms(x_ref[0], g_ref[...]).astype(BF16)
    tm = xn.shape[0]
    n_k = k_ref.shape[2]
    k_ref[0] = _dot(xn, w_ref[:, :n_k]).astype(k_ref.dtype)
    for c0 in range(0, rest_ref.shape[2], IN_PROJ_CHUNK):
        c1 = min(c0 + IN_PROJ_CHUNK, rest_ref.shape[2])
        rest_ref[0, :, c0:c1] = _dot(xn, w_ref[:, n_k + c0:n_k + c1])
    qt_ref[0] = _dot_nt(wqt_ref[...], xn).astype(qt_ref.dtype)
    vt = _dot_nt(wvt_ref[...], xn)
    heads, dv = vt_ref.shape[1], vt_ref.shape[3] - SUM_ROWS
    for h in range(heads):
        for blk in range(tm // t):
            vt_ref[0, h, blk, :dv, :] = vt[h * dv:(h + 1) * dv, blk * t:(blk + 1) * t].astype(vt_ref.dtype)
    vt_ref[0, :, :, dv:, :] = _sum_rows((heads, tm // t), t, vt_ref.dtype)


def _in_proj(x, g, wqt, wvt, w, t, tm):
    b, s, d = x.shape
    n_rest = w.shape[1] - A_WIDTH
    resident = lambda shape: pl.BlockSpec(shape, lambda bi, i: (0,) * len(shape), pipeline_mode=pl.Buffered(1))
    kern = functools.partial(_in_proj_kernel, t=t)
    return pl.pallas_call(
        kern,
        grid=(b, s // tm),
        in_specs=[pl.BlockSpec((1, tm, d), lambda bi, i: (bi, i, 0)),
                  pl.BlockSpec((1, d), lambda bi, i: (0, 0)),
                  resident((A_WIDTH, d)), resident((A_WIDTH, d)), resident((d, w.shape[1]))],
        out_specs=[pl.BlockSpec((1, A_WIDTH, tm), lambda bi, i: (bi, 0, i)),
                   pl.BlockSpec((1, tm, A_WIDTH), lambda bi, i: (bi, i, 0)),
                   pl.BlockSpec((1, A_HEADS, tm // t, HEAD_DIM + SUM_ROWS, t), lambda bi, i: (bi, 0, i, 0, 0)),
                   pl.BlockSpec((1, tm, n_rest), lambda bi, i: (bi, i, 0))],
        out_shape=[jax.ShapeDtypeStruct((b, A_WIDTH, s), BF16),
                   jax.ShapeDtypeStruct((b, s, A_WIDTH), BF16),
                   jax.ShapeDtypeStruct((b, A_HEADS, s // t, HEAD_DIM + SUM_ROWS, t), BF16),
                   jax.ShapeDtypeStruct((b, s, n_rest), F32)],
        compiler_params=_params(("parallel", "parallel"), 56),
        name="in_proj",
    )(x, g.reshape(1, d), wqt, wvt, w)


LOG2E = math.log2(math.e)


def _attend_causal_blocks(qi, t, n_col, lhs_block, rhs_col, values_at,
                          far_addend, prev_addend, diag_addend,
                          acc_ref, s0_ref, s1_ref, pa_ref, pb_ref):
    nq = n_col * t
    n_far_pairs = qi // 2
    pad = 2 * n_far_pairs - (qi - 1)

    def block_of(j):
        return jnp.clip(j - pad, 0, qi)

    def cols(c):
        return slice(c * t, (c + 1) * t)

    def scores_into(ref, j, c):
        ref[:, cols(c)] = _dot(lhs_block(block_of(j), c), rhs_col(c))

    def accumulate(alpha, j, p_ref, c):
        acc_ref[:, cols(c)] = (alpha[:, cols(c)] * acc_ref[:, cols(c)]
                               + _dot(values_at(block_of(j), c), p_ref[:, cols(c)]))

    def probs_into(p_ref, s_ref, m, addend, c):
        s = s_ref[:, cols(c)]
        if addend is not None:
            s = s + addend(c)
        m_new = jnp.maximum(m[:, cols(c)], jnp.max(s, axis=0, keepdims=True))
        p_ref[:, cols(c)] = jnp.exp2(s - m_new).astype(BF16)
        return m_new, jnp.exp2(m[:, cols(c)] - m_new)

    def step(j, carry, addend_a, addend_b):
        m, alpha_late = carry
        add_a, add_b = addend_a(block_of(j)), addend_b(block_of(j + 1))
        m_a, alpha_a = [], []
        for c in range(n_col):
            accumulate(alpha_late, j - 1, pb_ref, c)
            scores_into(s1_ref, j + 1, c)
            m_c, alpha_c = probs_into(pa_ref, s0_ref, m, add_a, c)
            m_a.append(m_c)
            alpha_a.append(alpha_c)
        m, alpha = jnp.concatenate(m_a, axis=1), jnp.concatenate(alpha_a, axis=1)
        m_b, alpha_b = [], []
        for c in range(n_col):
            accumulate(alpha, j, pa_ref, c)
            scores_into(s0_ref, j + 2, c)
            m_c, alpha_c = probs_into(pb_ref, s1_ref, m, add_b, c)
            m_b.append(m_c)
            alpha_b.append(alpha_c)
        m, alpha_late = jnp.concatenate(m_b, axis=1), jnp.concatenate(alpha_b, axis=1)
        return m, jnp.where(j < pad, 0.0, alpha_late)

    acc_ref[...] = jnp.zeros(acc_ref.shape, F32)
    pb_ref[...] = jnp.zeros(pb_ref.shape, BF16)
    for c in range(n_col):
        scores_into(s0_ref, 0, c)
    carry = (jnp.full((1, nq), NEG, F32), jnp.ones((1, nq), F32))
    carry = lax.fori_loop(0, n_far_pairs, lambda i, c: step(2 * i, c, far_addend, far_addend), carry)
    j_near = 2 * n_far_pairs
    _, alpha_late = step(j_near, carry, prev_addend, diag_addend)
    for c in range(n_col):
        accumulate(alpha_late, j_near + 1, pb_ref, c)


def _bucket_np(dist):
    n = np.maximum(dist, 0)
    max_exact = REL_BUCKETS // 2
    nf = np.maximum(n, 1).astype(np.float64)
    large = max_exact + (np.log(nf / max_exact) / math.log(REL_MAX_DIST / max_exact)
                         * (REL_BUCKETS - max_exact)).astype(np.int32)
    large = np.minimum(large, REL_BUCKETS - 1)
    return np.where(n < max_exact, n, large)


NO_BLOCK, PREV_BLOCK, DIAG_BLOCK = 0, 1, 2


def _bias_tiles(tab, t):
    assert t >= REL_MAX_DIST
    length = 2 * t
    bucket = jnp.asarray(_bucket_np(np.arange(length)))
    by_dist = ((jnp.take(tab, bucket, axis=0) - tab[REL_BUCKETS - 1]) * LOG2E).T

    def skew(v):
        flat = jnp.tile(v, (1, t))[:, :t * (length - 1)]
        return flat.reshape(-1, t, length - 1)[:, :, :t]

    diag = skew(by_dist)
    prev = skew(jnp.roll(by_dist, -t, axis=1))
    key = np.arange(t)[:, None]
    query = np.arange(t)[None, :]
    diag = jnp.where(jnp.asarray(query >= key)[None], diag, NEG)
    return jnp.stack([jnp.full_like(prev, NEG), prev, diag], axis=1).astype(F32)


def _diff_attn_kernel(lam_ref, qt_ref, k_ref, vt_ref, bias_ref, g_ref, o_ref, acc_ref, s0_ref, s1_ref, pa_ref, pb_ref,
                      *, t, heads, out_scale):
    qi = pl.program_id(2)
    sub = lax.broadcasted_iota(jnp.int32, (HEAD_DIM, t), 0)
    rhs = []
    for g in range(heads):
        qt = qt_ref[0, g * HEAD_DIM:(g + 1) * HEAD_DIM, :]
        zero = jnp.zeros_like(qt)
        rhs += [jnp.where(sub < A_HALF, qt, zero), jnp.where(sub >= A_HALF, qt, zero)]

    def lhs_block(kb, c):
        g = c // 2
        return k_ref[0, pl.ds(pl.multiple_of(kb * t, t), t), g * HEAD_DIM:(g + 1) * HEAD_DIM]

    prev_kind = jnp.where(qi >= 1, PREV_BLOCK, NO_BLOCK)
    _attend_causal_blocks(
        qi, t, 2 * heads, lhs_block, lambda c: rhs[c], lambda kb, c: vt_ref[0, c // 2, kb],
        lambda kb: None,
        lambda kb: (lambda c: bias_ref[c // 2, prev_kind]),
        lambda kb: (lambda c: bias_ref[c // 2, DIAG_BLOCK]),
        acc_ref, s0_ref, s1_ref, pa_ref, pb_ref)

    for g in range(heads):
        acc = acc_ref[:, 2 * g * t:(2 * g + 2) * t]
        o = acc[:HEAD_DIM] * (1.0 / acc[HEAD_DIM:HEAD_DIM + 1])
        o = o[:, :t] - lam_ref[0] * o[:, t:]
        y = o * lax.rsqrt(jnp.mean(o * o, axis=0, keepdims=True) + EPS) * g_ref[...] * out_scale
        o_ref[0, :, g * HEAD_DIM:(g + 1) * HEAD_DIM] = y.T.astype(o_ref.dtype)


def _diff_attention(qt, k, vt, lam, bias, sub_g, out_scale, heads):
    b, s, _ = k.shape
    nkb, t = vt.shape[2], vt.shape[4]
    hw = heads * HEAD_DIM
    nq = 2 * heads * t
    kern = functools.partial(_diff_attn_kernel, t=t, heads=heads, out_scale=out_scale)
    return pl.pallas_call(
        kern,
        grid=(b, A_HEADS // heads, s // t),
        in_specs=[pl.BlockSpec(memory_space=pltpu.SMEM),
                  pl.BlockSpec((1, hw, t), lambda bi, h, qi: (bi, h, qi)),
                  pl.BlockSpec((1, s, hw), lambda bi, h, qi: (bi, 0, h)),
                  pl.BlockSpec((1, heads, nkb, HEAD_DIM + SUM_ROWS, t), lambda bi, h, qi: (bi, h, 0, 0, 0)),
                  pl.BlockSpec((heads, 3, t, t), lambda bi, h, qi: (h, 0, 0, 0)),
                  pl.BlockSpec((HEAD_DIM, 1), lambda bi, h, qi: (0, 0))],
        out_specs=pl.BlockSpec((1, t, hw), lambda bi, h, qi: (bi, qi, h)),
        out_shape=jax.ShapeDtypeStruct((b, s, A_WIDTH), BF16),
        scratch_shapes=[pltpu.VMEM((HEAD_DIM + SUM_ROWS, nq), F32),
                        pltpu.VMEM((t, nq), F32), pltpu.VMEM((t, nq), F32),
                        pltpu.VMEM((t, nq), BF16), pltpu.VMEM((t, nq), BF16)],
        compiler_params=_params(("parallel", "parallel", "arbitrary"), 56),
        name="diff_attention",
    )(lam, qt, k, vt, bias, sub_g.reshape(HEAD_DIM, 1))


def _dsa_prep_kernel(cq_ref, ckv_ref, kk_ref, wi_ref, gq_ref, gkv_ref, wuq_ref, wuk_ref, wqit_ref,
                     qlat_ref, qidx_ref, wis_ref, kkb_ref, ckvn_ref, ckvnt_ref, *, t):
    cqn = _rms(cq_ref[0], gq_ref[...]).astype(BF16)
    tm = cqn.shape[0]
    q = _dot(cqn, wuq_ref[...])
    for h in range(B_HEADS):
        qh = q[:, h * HEAD_DIM:(h + 1) * HEAD_DIM].astype(BF16)
        qlat = (_dot_nt(wuk_ref[h], qh) * (HEAD_DIM ** -0.5 * LOG2E)).astype(BF16)
        for blk in range(tm // t):
            qlat_ref[0, blk, :, h * t:(h + 1) * t] = qlat[:, blk * t:(blk + 1) * t]
    qidx_ref[0] = _dot_nt(wqit_ref[...], cqn).astype(BF16)
    wis = wi_ref[0] * (IDX_HEADS ** -0.5 * IDX_DIM ** -0.5)
    wis_ref[0] = wis.T[:IDX_HEADS]
    kkb_ref[0] = kk_ref[0].astype(BF16)
    ckvn = _rms(ckv_ref[0], gkv_ref[...])
    ckvn_ref[0] = ckvn.astype(BF16)
    ckvn_t = ckvn.T.astype(BF16)
    for blk in range(tm // t):
        ckvnt_ref[0, blk, :B_KV_RANK, :] = ckvn_t[:, blk * t:(blk + 1) * t]
    ckvnt_ref[0, :, B_KV_RANK:, :] = _sum_rows((tm // t,), t, BF16)


def _dsa_prep(rest, gq, gkv, wuq, wuk, wqit, t, tm):
    b, s, _ = rest.shape
    col = lambda c: (lambda bi, i: (bi, i, c))
    const = lambda n: (lambda bi, i: (0,) * n)
    kern = functools.partial(_dsa_prep_kernel, t=t)
    return pl.pallas_call(
        kern,
        grid=(b, s // tm),
        in_specs=[pl.BlockSpec((1, tm, B_Q_RANK), col(2)),
                  pl.BlockSpec((1, tm, B_KV_RANK), col(6)),
                  pl.BlockSpec((1, tm, 128), col(14)),
                  pl.BlockSpec((1, tm, 128), col(15)),
                  pl.BlockSpec((1, B_Q_RANK), const(2)),
                  pl.BlockSpec((1, B_KV_RANK), const(2)),
                  pl.BlockSpec((B_Q_RANK, B_WIDTH), const(2)),
                  pl.BlockSpec((B_HEADS, B_KV_RANK, HEAD_DIM), const(3)),
                  pl.BlockSpec((IDX_HEADS * IDX_DIM, B_Q_RANK), const(2))],
        out_specs=[pl.BlockSpec((1, tm // t, B_KV_RANK, B_HEADS * t), lambda bi, i: (bi, i, 0, 0)),
                   pl.BlockSpec((1, IDX_HEADS * IDX_DIM, tm), lambda bi, i: (bi, 0, i)),
                   pl.BlockSpec((1, IDX_HEADS, tm), lambda bi, i: (bi, 0, i)),
                   pl.BlockSpec((1, tm, 128), lambda bi, i: (bi, i, 0)),
                   pl.BlockSpec((1, tm, B_KV_RANK), lambda bi, i: (bi, i, 0)),
                   pl.BlockSpec((1, tm // t, B_KV_RANK + SUM_ROWS, t), lambda bi, i: (bi, i, 0, 0))],
        out_shape=[jax.ShapeDtypeStruct((b, s // t, B_KV_RANK, B_HEADS * t), BF16),
                   jax.ShapeDtypeStruct((b, IDX_HEADS * IDX_DIM, s), BF16),
                   jax.ShapeDtypeStruct((b, IDX_HEADS, s), F32),
                   jax.ShapeDtypeStruct((b, s, 128), BF16),
                   jax.ShapeDtypeStruct((b, s, B_KV_RANK), BF16),
                   jax.ShapeDtypeStruct((b, s // t, B_KV_RANK + SUM_ROWS, t), BF16)],
        compiler_params=_params(("parallel", "parallel"), 40),
        name="dsa_prep",
    )(rest, rest, rest, rest, gq.reshape(1, -1), gkv.reshape(1, -1), wuq, wuk, wqit)


def _dsa_kernel(qlat_ref, qidx_ref, wis_ref, kk_ref, ckvn_ref, ckvnt_ref, bias_ref, wuv_ref, o_ref,
                key_ref, hi_ref, lo_ref, acc_ref, s0_ref, s1_ref, pa_ref, pb_ref, *, t, topk):
    qi = pl.program_id(1)
    nblk = qi + 1
    nkb = key_ref.shape[0]
    key_pos = lax.broadcasted_iota(jnp.int32, (t, t), 0)
    query_pos = lax.broadcasted_iota(jnp.int32, (t, t), 1)

    qidx = qidx_ref[0]
    wis = wis_ref[0]
    sub = lax.broadcasted_iota(jnp.int32, (2 * IDX_DIM, t), 0)
    q_heads = []
    for g in range(IDX_HEADS):
        pair = qidx[(g // 2) * 2 * IDX_DIM:(g // 2 + 1) * 2 * IDX_DIM]
        keep = (sub < IDX_DIM) if g % 2 == 0 else (sub >= IDX_DIM)
        q_heads.append(jnp.where(keep, pair, jnp.zeros_like(pair)))

    def score_block(kb):
        ks = pl.multiple_of(kb * t, t)
        kblk = kk_ref[0, pl.ds(ks, t), :]
        score = jnp.zeros((t, t), F32)
        for g in range(IDX_HEADS):
            d = _dot(kblk, q_heads[g])
            score = score + jnp.maximum(d, 0.0) * wis[g:g + 1]
        bits = pltpu.bitcast(score, jnp.int32)
        key = jnp.where(bits < 0, bits ^ jnp.int32(0x7FFFFFFF), bits)
        causal = (kb < qi) | (query_pos >= key_pos)
        key = jnp.where(causal, key, jnp.int32(INT_MIN))
        key_ref[kb] = key
        hi_ref[kb] = (key >> 16).astype(jnp.int16)
        lo_ref[kb] = ((key & 0xFFFF) + I16_MIN).astype(jnp.int16)

    def score_pair(i, c):
        score_block(2 * i)
        score_block(jnp.minimum(2 * i + 1, qi))
        return c

    lax.fori_loop(0, (nblk + 1) // 2, score_pair, 0)

    hi_ref[nkb] = jnp.full((t, t), I16_MIN, jnp.int16)
    lo_ref[nkb] = jnp.full((t, t), I16_MIN, jnp.int16)
    one, zero = jnp.int16(1), jnp.int16(0)

    def count_ge(plane_ref, cand):
        cand_b = jnp.broadcast_to(cand.astype(jnp.int16), (t, t))

        def body(i, c):
            for u in range(SCAN_UNROLL):
                kb = SCAN_UNROLL * i + u
                x = jnp.where(plane_ref[jnp.where(kb < nblk, kb, nkb)] >= cand_b, one, zero)
                parts = [x[r * 16:(r + 1) * 16] for r in range(t // 16)]
                while len(parts) > 1:
                    parts = [a + b for a, b in zip(parts[0::2], parts[1::2])]
                c = c + parts[0]
            return c

        trips = (nblk + SCAN_UNROLL - 1) // SCAN_UNROLL
        c = lax.fori_loop(0, trips, body, jnp.zeros((16, t), jnp.int16))
        return jnp.sum(c.astype(jnp.int32), axis=0, keepdims=True)

    def greedy_bits(count_at_least):
        def bit_step(i, v):
            cand = v + jnp.left_shift(jnp.int32(1), 15 - i)
            return jnp.where(count_at_least(cand) >= topk, cand, v)
        return lax.fori_loop(0, 16, bit_step, jnp.full((1, t), I16_MIN, jnp.int32))

    hi = greedy_bits(lambda cand: count_ge(hi_ref, cand))
    above = count_ge(hi_ref, hi + 1)
    hi_b = jnp.broadcast_to(hi.astype(jnp.int16), (t, t))

    def restrict(kb, c):
        lo_ref[kb] = jnp.where(hi_ref[kb] == hi_b, lo_ref[kb], jnp.int16(I16_MIN))
        return c

    lax.fori_loop(0, nblk, restrict, 0)
    lo = greedy_bits(lambda cand: above + count_ge(lo_ref, cand))
    thr = jnp.left_shift(hi, 16) + (lo - I16_MIN)
    thr_b = jnp.broadcast_to(jnp.maximum(thr, jnp.int32(INT_MIN + 1)), (t, t))

    def count_keys(pred):
        def body(kb, c):
            x = jnp.where(pred(key_ref[kb], kb * t), 1, 0)
            return c + jnp.sum(x.reshape(t // 8, 8, t), axis=0)
        c = lax.fori_loop(0, nblk, body, jnp.zeros((8, t), jnp.int32))
        return jnp.sum(c, axis=0, keepdims=True)

    selected = count_keys(lambda key, base: key >= thr_b)

    @pl.when(jnp.max(selected) > topk)
    def _():
        need = topk - count_keys(lambda key, base: key > thr_b)

        def index_bit(i, cut):
            cand = cut + jnp.left_shift(jnp.int32(1), index_bits - 1 - i)
            cand_b = jnp.broadcast_to(cand, (t, t))
            below = count_keys(lambda key, base: (key == thr_b) & (key_pos + base < cand_b))
            return jnp.where(below < need, cand, cut)

        index_bits = max(1, (nkb * t - 1).bit_length())
        cut = lax.fori_loop(0, index_bits, index_bit, jnp.zeros((1, t), jnp.int32))
        cut_b = jnp.broadcast_to(cut, (t, t))

        def demote(kb, c):
            key = key_ref[kb]
            key_ref[kb] = jnp.where((key == thr_b) & (key_pos + kb * t > cut_b), key - 1, key)
            return c

        lax.fori_loop(0, nblk, demote, 0)

    def lhs_block(kb, h):
        return ckvn_ref[0, pl.ds(pl.multiple_of(kb * t, t), t), :]

    def selection(kb):
        return jnp.where(key_ref[kb] >= thr_b, 0.0, NEG)

    def far_addend(kb):
        mask = selection(kb)
        return lambda h: mask

    def near_addend(kind):
        def addend(kb):
            mask = selection(kb)
            return lambda h: bias_ref[h, kind] + mask
        return addend

    _attend_causal_blocks(
        qi, t, B_HEADS, lhs_block, lambda h: qlat_ref[0, 0, :, h * t:(h + 1) * t],
        lambda kb, h: ckvnt_ref[0, kb],
        far_addend, near_addend(jnp.where(qi >= 1, PREV_BLOCK, NO_BLOCK)), near_addend(DIAG_BLOCK),
        acc_ref, s0_ref, s1_ref, pa_ref, pb_ref)

    for h in range(B_HEADS):
        cols = slice(h * t, (h + 1) * t)
        o_lat = (acc_ref[:B_KV_RANK, cols] * (1.0 / acc_ref[B_KV_RANK:B_KV_RANK + 1, cols])).astype(BF16)
        o = _dot(wuv_ref[h], o_lat)
        o_ref[0, :, h * HEAD_DIM:(h + 1) * HEAD_DIM] = o.T.astype(o_ref.dtype)


def _dsa_attention(qlat_t, qidx_t, wis_t, kk, ckvn, ckvn_t, bias, wuv):
    b, s, _ = ckvn.shape
    nkb, t = ckvn_t.shape[1], ckvn_t.shape[3]
    topk = min(TOPK_MAX, s // 4)
    kern = functools.partial(_dsa_kernel, t=t, topk=topk)
    qcol = lambda bi, qi: (bi, 0, qi)
    allk = lambda bi, qi: (bi, 0, 0)
    return pl.pallas_call(
        kern,
        grid=(b, s // t),
        in_specs=[pl.BlockSpec((1, 1, B_KV_RANK, B_HEADS * t), lambda bi, qi: (bi, qi, 0, 0)),
                  pl.BlockSpec((1, IDX_HEADS * IDX_DIM, t), qcol),
                  pl.BlockSpec((1, IDX_HEADS, t), qcol),
                  pl.BlockSpec((1, s, 128), allk),
                  pl.BlockSpec((1, s, B_KV_RANK), allk),
                  pl.BlockSpec((1, nkb, B_KV_RANK + SUM_ROWS, t), lambda bi, qi: (bi, 0, 0, 0)),
                  pl.BlockSpec((B_HEADS, 3, t, t), lambda bi, qi: (0, 0, 0, 0)),
                  pl.BlockSpec((B_HEADS, HEAD_DIM, B_KV_RANK), lambda bi, qi: (0, 0, 0))],
        out_specs=pl.BlockSpec((1, t, B_WIDTH), lambda bi, qi: (bi, qi, 0)),
        out_shape=jax.ShapeDtypeStruct((b, s, B_WIDTH), BF16),
        scratch_shapes=[pltpu.VMEM((nkb, t, t), jnp.int32),
                        pltpu.VMEM((nkb + 1, t, t), jnp.int16), pltpu.VMEM((nkb + 1, t, t), jnp.int16),
                        pltpu.VMEM((B_KV_RANK + SUM_ROWS, B_HEADS * t), F32),
                        pltpu.VMEM((t, B_HEADS * t), F32), pltpu.VMEM((t, B_HEADS * t), F32),
                        pltpu.VMEM((t, B_HEADS * t), BF16), pltpu.VMEM((t, B_HEADS * t), BF16)],
        compiler_params=_params(("parallel", "arbitrary"), 56),
        name="dsa_attention",
    )(qlat_t, qidx_t, wis_t, kk, ckvn, ckvn_t, bias, wuv)


def _gelu(x):
    return x * (0.5 * (1.0 + jnp.tanh(math.sqrt(2.0 / math.pi) * (x + 0.044715 * (x * x * x)))))


def _gmlp_kernel(cu_ref, cv_ref, g_ref, b_ref, ws_ref, bs_ref, o_ref, *, ts):
    u = _gelu(cu_ref[...])
    v = _gelu(cv_ref[...])
    mu = jnp.mean(v, axis=-1, keepdims=True)
    vc = v - mu
    var = jnp.mean(vc * vc, axis=-1, keepdims=True)
    vn = (vc * lax.rsqrt(var + EPS) * g_ref[...] + b_ref[...]).astype(BF16)
    row = lax.broadcasted_iota(jnp.int32, (CHUNK, CHUNK), 0)
    col = lax.broadcasted_iota(jnp.int32, (CHUNK, CHUNK), 1)
    for g in range(C_GROUPS):
        w = jnp.where(row >= col, ws_ref[g], 0.0).astype(BF16)
        bcol = bs_ref[:, g:g + 1]
        cs = slice(g * 128, (g + 1) * 128)
        for c in range(ts // CHUNK):
            rs = slice(c * CHUNK, (c + 1) * CHUNK)
            y = _dot(w, vn[rs, cs]) + bcol
            o_ref[rs, cs] = (u[rs, cs] * y).astype(o_ref.dtype)


def _gmlp(rest, ln_g, ln_b, w_s, b_s, ts):
    m = rest.shape[0]
    kern = functools.partial(_gmlp_kernel, ts=ts)
    return pl.pallas_call(
        kern,
        grid=(m // ts,),
        in_specs=[pl.BlockSpec((ts, C_WIDTH), lambda i: (i, 0)),
                  pl.BlockSpec((ts, C_WIDTH), lambda i: (i, 1)),
                  pl.BlockSpec((1, C_WIDTH), lambda i: (0, 0)),
                  pl.BlockSpec((1, C_WIDTH), lambda i: (0, 0)),
                  pl.BlockSpec((C_GROUPS, CHUNK, CHUNK), lambda i: (0, 0, 0)),
                  pl.BlockSpec((CHUNK, C_GROUPS), lambda i: (0, 0))],
        out_specs=pl.BlockSpec((ts, C_WIDTH), lambda i: (i, 0)),
        out_shape=jax.ShapeDtypeStruct((m, C_WIDTH), BF16),
        compiler_params=_params(("parallel",), 32),
        name="gmlp",
    )(rest, rest, ln_g.reshape(1, -1), ln_b.reshape(1, -1), w_s, jnp.transpose(b_s))


def _out_proj_kernel(h_ref, oa_ref, ob_ref, oc_ref, wa_ref, wb_ref, wc_ref, g_ref, o_ref):
    y = _dot(oa_ref[...], wa_ref[...]) + _dot(ob_ref[...], wb_ref[...]) + _dot(oc_ref[...], wc_ref[...])
    o_ref[...] = h_ref[...] + _rms(y, g_ref[...])


def _out_proj(h, oa, ob, oc, wa, wb, wc, g, tm):
    m, d = h.shape
    rowblk = lambda i: (i, 0)
    full = lambda i: (0, 0)
    return pl.pallas_call(
        _out_proj_kernel,
        grid=(m // tm,),
        in_specs=[pl.BlockSpec((tm, d), rowblk),
                  pl.BlockSpec((tm, A_WIDTH), rowblk),
                  pl.BlockSpec((tm, B_WIDTH), rowblk),
                  pl.BlockSpec((tm, C_WIDTH), rowblk),
                  pl.BlockSpec((A_WIDTH, d), full),
                  pl.BlockSpec((B_WIDTH, d), full),
                  pl.BlockSpec((C_WIDTH, d), full),
                  pl.BlockSpec((1, d), full)],
        out_specs=pl.BlockSpec((tm, d), rowblk),
        out_shape=jax.ShapeDtypeStruct((m, d), F32),
        compiler_params=_params(("parallel",), 48),
        name="out_proj",
    )(h, oa, ob, oc, wa, wb, wc, g.reshape(1, d))


def _xattn_kernel(h_ref, gpre_ref, wq_ref, k_ref, v_ref, wo_ref, gpost_ref, o_ref):
    x = h_ref[0]
    xn = _rms(x, gpre_ref[...]).astype(BF16)
    q = _dot(xn, wq_ref[...])
    scale = X_DIM ** -0.5
    outs = []
    for hh in range(X_HEADS):
        cs = slice(hh * X_DIM, (hh + 1) * X_DIM)
        s = _dot_nt(q[:, cs].astype(BF16), k_ref[0, :, cs]) * scale
        p = jnp.exp(s - jnp.max(s, axis=-1, keepdims=True))
        p = p * (1.0 / jnp.sum(p, axis=-1, keepdims=True))
        outs.append(_dot(p.astype(BF16), v_ref[0, :, cs]).astype(BF16))
    o = jnp.concatenate(outs, axis=1)
    xa = _dot(o, wo_ref[...])
    o_ref[0] = x + _rms(xa, gpost_ref[...])


def _xattn(h, kv, gpre, wq, wo, gpost, tm):
    b, s, d = h.shape
    mlen = kv.shape[1]
    hw = X_HEADS * X_DIM
    return pl.pallas_call(
        _xattn_kernel,
        grid=(b, s // tm),
        in_specs=[pl.BlockSpec((1, tm, d), lambda bi, i: (bi, i, 0)),
                  pl.BlockSpec((1, d), lambda bi, i: (0, 0)),
                  pl.BlockSpec((d, hw), lambda bi, i: (0, 0)),
                  pl.BlockSpec((1, mlen, hw), lambda bi, i: (bi, 0, 0)),
                  pl.BlockSpec((1, mlen, hw), lambda bi, i: (bi, 0, 1)),
                  pl.BlockSpec((hw, d), lambda bi, i: (0, 0)),
                  pl.BlockSpec((1, d), lambda bi, i: (0, 0))],
        out_specs=pl.BlockSpec((1, tm, d), lambda bi, i: (bi, i, 0)),
        out_shape=jax.ShapeDtypeStruct((b, s, d), F32),
        compiler_params=_params(("parallel", "parallel"), 48),
        name="xattn",
    )(h, gpre.reshape(1, d), wq, kv, kv, wo, gpost.reshape(1, d))


def _ffn_kernel(x_ref, xprev_ref, gpre_ref, wg_ref, wu_ref, wd_ref, gpost_ref, o_ref, xn_ref, acc_ref):
    i, j = pl.program_id(0), pl.program_id(1)
    n_tiles = pl.num_programs(0) - 1
    cur = lax.rem(i, 2)

    @pl.when((i == 0) & (j == 0))
    def _():
        acc_ref[1] = jnp.zeros(acc_ref.shape[1:], F32)

    def finish_previous():
        o_ref[...] = xprev_ref[...] + _rms(acc_ref[1 - cur], gpost_ref[...])

    def hidden_step(xn):
        gate = _dot(xn, wg_ref[...])
        up = _dot(xn, wu_ref[...])
        act = gate * (1.0 / (1.0 + jnp.exp(-gate))) * up
        return _dot(act.astype(BF16), wd_ref[...])

    def first_step():
        finish_previous()
        xn = _rms(x_ref[...], gpre_ref[...]).astype(BF16)
        xn_ref[...] = xn
        acc_ref[cur] = hidden_step(xn)

    def later_step():
        acc_ref[cur] += hidden_step(xn_ref[...])

    def flush_step():
        lax.cond(j == 0, finish_previous, lambda: None)

    lax.cond(i < n_tiles, lambda: lax.cond(j == 0, first_step, later_step), flush_step)


def _ffn(h, gpre, w_gu, w_down, gpost, tm, tf):
    m, d = h.shape
    f = w_down.shape[0]
    nf = f // tf
    n_tiles = m // tm
    row = lambda i: jnp.minimum(i, n_tiles - 1)
    hid = lambda i, j: jnp.where(i < n_tiles, j, nf - 1)
    done = lambda i, j: jnp.clip(i - (j == 0).astype(jnp.int32), 0, n_tiles - 1)
    return pl.pallas_call(
        _ffn_kernel,
        grid=(n_tiles + 1, nf),
        in_specs=[pl.BlockSpec((tm, d), lambda i, j: (row(i), 0)),
                  pl.BlockSpec((tm, d), lambda i, j: (jnp.maximum(i - 1, 0), 0)),
                  pl.BlockSpec((1, d), lambda i, j: (0, 0)),
                  pl.BlockSpec((d, tf), lambda i, j: (0, hid(i, j))),
                  pl.BlockSpec((d, tf), lambda i, j: (0, nf + hid(i, j))),
                  pl.BlockSpec((tf, d), lambda i, j: (hid(i, j), 0)),
                  pl.BlockSpec((1, d), lambda i, j: (0, 0))],
        out_specs=pl.BlockSpec((tm, d), lambda i, j: (done(i, j), 0)),
        out_shape=jax.ShapeDtypeStruct((m, d), F32),
        scratch_shapes=[pltpu.VMEM((tm, d), BF16), pltpu.VMEM((2, tm, d), F32)],
        compiler_params=_params(("arbitrary", "arbitrary"), 58),
        name="ffn",
    )(h, h, gpre.reshape(1, d), w_gu, w_gu, w_down, gpost.reshape(1, d))


def _tile(n, pref):
    t = min(pref, n)
    assert n % t == 0
    return t


def kernel(x, mem, rel_bias, mix_pre_g, mix_post_g, w_in, w_out, a_lambda, a_sub_g, b_cq_g, b_ckv_g, b_w_uq, b_w_qidx, b_w_uk, b_w_uv, c_ln_g, c_ln_b, c_w_s, c_b_s, x_pre_g, x_post_g, mem_g, x_wq, x_wkv, x_wo, f_pre_g, f_post_g, f_w_gu, f_w_down):
    bsz, s_len, d = x.shape
    depth = w_in.shape[0]
    tokens = bsz * s_len
    mlen = mem.shape[1]
    t_attn = min(ATTN_BLOCK, s_len)
    bias_a = _bias_tiles(rel_bias[:, :A_HEADS], t_attn)
    bias_b = _bias_tiles(rel_bias[:, A_HEADS:], t_attn)
    sizes = (A_WIDTH, A_WIDTH, A_WIDTH, B_Q_RANK, B_KV_RANK, IDX_DIM, IDX_HEADS, C_WIDTH, C_WIDTH)
    offs = np.concatenate([[0], np.cumsum(sizes)])
    tm = _tile(tokens, 512)

    h = x.reshape(tokens, d)
    mem2 = mem.reshape(bsz * mlen, d)
    for l in range(depth):
        lam_init = 0.8 - 0.6 * math.exp(-0.3 * l)
        wl = w_in[l]
        cols = [wl[:, offs[i]:offs[i + 1]] for i in range(len(sizes))]
        wqa, wka, wva, wcq, wckv, wkidx, wwidx, wcu, wcv = cols
        w_k_rest = jnp.concatenate(
            [wka, wcu, wcv, wcq, wckv, wkidx, wkidx, wwidx,
             jnp.zeros((d, 128 - IDX_HEADS), wl.dtype)], axis=1).astype(BF16)
        wqt = jnp.transpose(wqa * (A_HALF ** -0.5 * LOG2E)).astype(BF16)
        wvt = jnp.transpose(wva).astype(BF16)

        qt, k, vt, rest = _in_proj(h.reshape(bsz, s_len, d), mix_pre_g[l], wqt, wvt, w_k_rest,
                                   t_attn, _tile(s_len, 512))

        lp = a_lambda[l].astype(F32)
        lam = jnp.exp(jnp.sum(lp[0] * lp[1])) - jnp.exp(jnp.sum(lp[2] * lp[3])) + lam_init
        oa = _diff_attention(qt, k, vt, lam.reshape(1), bias_a, a_sub_g[l], 1.0 - lam_init,
                             DIFF_HEADS_PER_STEP)

        wuq = b_w_uq[l].reshape(B_Q_RANK, B_WIDTH).astype(BF16)
        wuk = jnp.transpose(b_w_uk[l], (1, 0, 2)).astype(BF16)
        wqit = jnp.transpose(b_w_qidx[l].reshape(B_Q_RANK, IDX_HEADS * IDX_DIM)).astype(BF16)
        wuv = jnp.transpose(b_w_uv[l], (1, 2, 0)).astype(BF16)
        prep = _dsa_prep(rest, b_cq_g[l], b_ckv_g[l], wuq, wuk, wqit, t_attn, _tile(s_len, 512))
        ob = _dsa_attention(*prep, bias_b, wuv)

        rest = rest.reshape(tokens, rest.shape[-1])
        oc = _gmlp(rest, c_ln_g[l], c_ln_b[l], c_w_s[l], c_b_s[l], tm)

        wo_l = w_out[l].astype(BF16)
        h = _out_proj(h, oa.reshape(tokens, A_WIDTH), ob.reshape(tokens, B_WIDTH), oc,
                      wo_l[:A_WIDTH], wo_l[A_WIDTH:A_WIDTH + B_WIDTH], wo_l[A_WIDTH + B_WIDTH:],
                      mix_post_g[l], tm)

        hw = X_HEADS * X_DIM
        kv = _norm_matmul(mem2, mem_g[l], x_wkv[l].reshape(d, 2 * hw).astype(BF16), BF16,
                          _tile(bsz * mlen, 512), 512)
        h = _xattn(h.reshape(bsz, s_len, d), kv.reshape(bsz, mlen, 2 * hw), x_pre_g[l],
                   x_wq[l].reshape(d, hw).astype(BF16), x_wo[l].reshape(hw, d).astype(BF16),
                   x_post_g[l], _tile(s_len, 512)).reshape(tokens, d)

        fh = f_w_down.shape[1]
        h = _ffn(h, f_pre_g[l], f_w_gu[l].reshape(d, 2 * fh).astype(BF16),
                 f_w_down[l].astype(BF16), f_post_g[l], tm, 512)
    return h.reshape(bsz, s_len, d)
```

```python
import functools
import math

import numpy as np
import jax
import jax.numpy as jnp
from jax import lax
from jax.experimental import pallas as pl
from jax.experimental.pallas import tpu as pltpu

F32 = jnp.float32
BF16 = jnp.bfloat16
EPS = 1e-6
NEG = -1e30
INT_MIN = -(2 ** 31)
I16_MIN = -(2 ** 15)

HEAD_DIM = 128
A_HEADS = 6
A_HALF = 64
B_HEADS = 6
B_Q_RANK = 512
B_KV_RANK = 256
IDX_HEADS = 16
IDX_DIM = 64
TOPK_MAX = 256
C_GROUPS = 4
C_WIDTH = 512
CHUNK = 128
X_HEADS = 4
X_DIM = 128
REL_BUCKETS = 32
REL_MAX_DIST = 128
A_WIDTH = A_HEADS * HEAD_DIM
B_WIDTH = B_HEADS * HEAD_DIM

ATTN_BLOCK = 256
DIFF_HEADS_PER_STEP = 6
SCAN_UNROLL = 2
MIB = 1024 * 1024


def _params(semantics, vmem_mib):
    return pltpu.CompilerParams(dimension_semantics=semantics,
                                vmem_limit_bytes=vmem_mib * MIB)


def _rms(x, g):
    return x * lax.rsqrt(jnp.mean(x * x, axis=-1, keepdims=True) + EPS) * g


def _dot(a, b):
    return jnp.dot(a, b, preferred_element_type=F32)


def _dot_nt(a, b):
    return lax.dot_general(a, b, (((1,), (1,)), ((), ())), preferred_element_type=F32)


def _norm_matmul_kernel(x_ref, g_ref, w_ref, o_ref, xn_ref):
    @pl.when(pl.program_id(1) == 0)
    def _():
        xn_ref[...] = _rms(x_ref[...], g_ref[...]).astype(BF16)

    o_ref[...] = _dot(xn_ref[...], w_ref[...]).astype(o_ref.dtype)


def _norm_matmul(x, g, w, out_dtype, tm, tn):
    m, d = x.shape
    n = w.shape[1]
    return pl.pallas_call(
        _norm_matmul_kernel,
        grid=(m // tm, n // tn),
        in_specs=[pl.BlockSpec((tm, d), lambda i, j: (i, 0)),
                  pl.BlockSpec((1, d), lambda i, j: (0, 0)),
                  pl.BlockSpec((d, tn), lambda i, j: (0, j))],
        out_specs=pl.BlockSpec((tm, tn), lambda i, j: (i, j)),
        out_shape=jax.ShapeDtypeStruct((m, n), out_dtype),
        scratch_shapes=[pltpu.VMEM((tm, d), BF16)],
        compiler_params=_params(("parallel", "arbitrary"), 48),
        name="norm_matmul",
    )(x, g.reshape(1, d), w)


SUM_ROWS = 16
IN_PROJ_CHUNK = 1024


def _sum_rows(lead, t, dtype):
    first = lax.broadcasted_iota(jnp.int32, lead + (SUM_ROWS, t), len(lead)) == 0
    return jnp.where(first, 1.0, 0.0).astype(dtype)


def _in_proj_kernel(x_ref, g_ref, wqt_ref, wvt_ref, w_ref, qt_ref, k_ref, vt_ref, rest_ref, *, t):
    xn = _rms(x_ref[0], g_ref[...]).astype(BF16)
    tm = xn.shape[0]
    n_k = k_ref.shape[2]
    k_ref[0] = _dot(xn, w_ref[:, :n_k]).astype(k_ref.dtype)
    for c0 in range(0, rest_ref.shape[2], IN_PROJ_CHUNK):
        c1 = min(c0 + IN_PROJ_CHUNK, rest_ref.shape[2])
        rest_ref[0, :, c0:c1] = _dot(xn, w_ref[:, n_k + c0:n_k + c1])
    qt_ref[0] = _dot_nt(wqt_ref[...], xn).astype(qt_ref.dtype)
    vt = _dot_nt(wvt_ref[...], xn)
    heads, dv = vt_ref.shape[1], vt_ref.shape[3] - SUM_ROWS
    for h in range(heads):
        for blk in range(tm // t):
            vt_ref[0, h, blk, :dv, :] = vt[h * dv:(h + 1) * dv, blk * t:(blk + 1) * t].astype(vt_ref.dtype)
    vt_ref[0, :, :, dv:, :] = _sum_rows((heads, tm // t), t, vt_ref.dtype)


def _in_proj(x, g, wqt, wvt, w, t, tm):
    b, s, d = x.shape
    n_rest = w.shape[1] - A_WIDTH
    resident = lambda shape: pl.BlockSpec(shape, lambda bi, i: (0,) * len(shape), pipeline_mode=pl.Buffered(1))
    kern = functools.partial(_in_proj_kernel, t=t)
    return pl.pallas_call(
        kern,
        grid=(b, s // tm),
        in_specs=[pl.BlockSpec((1, tm, d), lambda bi, i: (bi, i, 0)),
                  pl.BlockSpec((1, d), lambda bi, i: (0, 0)),
                  resident((A_WIDTH, d)), resident((A_WIDTH, d)), resident((d, w.shape[1]))],
        out_specs=[pl.BlockSpec((1, A_WIDTH, tm), lambda bi, i: (bi, 0, i)),
                   pl.BlockSpec((1, tm, A_WIDTH), lambda bi, i: (bi, i, 0)),
                   pl.BlockSpec((1, A_HEADS, tm // t, HEAD_DIM + SUM_ROWS, t), lambda bi, i: (bi, 0, i, 0, 0)),
                   pl.BlockSpec((1, tm, n_rest), lambda bi, i: (bi, i, 0))],
        out_shape=[jax.ShapeDtypeStruct((b, A_WIDTH, s), BF16),
                   jax.ShapeDtypeStruct((b, s, A_WIDTH), BF16),
                   jax.ShapeDtypeStruct((b, A_HEADS, s // t, HEAD_DIM + SUM_ROWS, t), BF16),
                   jax.ShapeDtypeStruct((b, s, n_rest), F32)],
        compiler_params=_params(("parallel", "parallel"), 56),
        name="in_proj",
    )(x, g.reshape(1, d), wqt, wvt, w)


LOG2E = math.log2(math.e)


def _attend_causal_blocks(qi, t, n_col, lhs_block, rhs_col, values_at,
                          far_addend, prev_addend, diag_addend,
                          acc_ref, s0_ref, s1_ref, pa_ref, pb_ref):
    nq = n_col * t
    n_far_pairs = qi // 2
    pad = 2 * n_far_pairs - (qi - 1)

    def block_of(j):
        return jnp.clip(j - pad, 0, qi)

    def cols(c):
        return slice(c * t, (c + 1) * t)

    def scores_into(ref, j, c):
        ref[:, cols(c)] = _dot(lhs_block(block_of(j), c), rhs_col(c))

    def accumulate(alpha, j, p_ref, c):
        acc_ref[:, cols(c)] = (alpha[:, cols(c)] * acc_ref[:, cols(c)]
                               + _dot(values_at(block_of(j), c), p_ref[:, cols(c)]))

    def probs_into(p_ref, s_ref, m, addend, c):
        s = s_ref[:, cols(c)]
        if addend is not None:
            s = s + addend(c)
        m_new = jnp.maximum(m[:, cols(c)], jnp.max(s, axis=0, keepdims=True))
        p_ref[:, cols(c)] = jnp.exp2(s - m_new).astype(BF16)
        return m_new, jnp.exp2(m[:, cols(c)] - m_new)

    def step(j, carry, addend_a, addend_b):
        m, alpha_late = carry
        add_a, add_b = addend_a(block_of(j)), addend_b(block_of(j + 1))
        m_a, alpha_a = [], []
        for c in range(n_col):
            accumulate(alpha_late, j - 1, pb_ref, c)
            scores_into(s1_ref, j + 1, c)
            m_c, alpha_c = probs_into(pa_ref, s0_ref, m, add_a, c)
            m_a.append(m_c)
            alpha_a.append(alpha_c)
        m, alpha = jnp.concatenate(m_a, axis=1), jnp.concatenate(alpha_a, axis=1)
        m_b, alpha_b = [], []
        for c in range(n_col):
            accumulate(alpha, j, pa_ref, c)
            scores_into(s0_ref, j + 2, c)
            m_c, alpha_c = probs_into(pb_ref, s1_ref, m, add_b, c)
            m_b.append(m_c)
            alpha_b.append(alpha_c)
        m, alpha_late = jnp.concatenate(m_b, axis=1), jnp.concatenate(alpha_b, axis=1)
        return m, jnp.where(j < pad, 0.0, alpha_late)

    acc_ref[...] = jnp.zeros(acc_ref.shape, F32)
    pb_ref[...] = jnp.zeros(pb_ref.shape, BF16)
    for c in range(n_col):
        scores_into(s0_ref, 0, c)
    carry = (jnp.full((1, nq), NEG, F32), jnp.ones((1, nq), F32))
    carry = lax.fori_loop(0, n_far_pairs, lambda i, c: step(2 * i, c, far_addend, far_addend), carry)
    j_near = 2 * n_far_pairs
    _, alpha_late = step(j_near, carry, prev_addend, diag_addend)
    for c in range(n_col):
        accumulate(alpha_late, j_near + 1, pb_ref, c)


def _bucket_np(dist):
    n = np.maximum(dist, 0)
    max_exact = REL_BUCKETS // 2
    nf = np.maximum(n, 1).astype(np.float64)
    large = max_exact + (np.log(nf / max_exact) / math.log(REL_MAX_DIST / max_exact)
                         * (REL_BUCKETS - max_exact)).astype(np.int32)
    large = np.minimum(large, REL_BUCKETS - 1)
    return np.where(n < max_exact, n, large)


NO_BLOCK, PREV_BLOCK, DIAG_BLOCK = 0, 1, 2


def _bias_tiles(tab, t):
    assert t >= REL_MAX_DIST
    length = 2 * t
    bucket = jnp.asarray(_bucket_np(np.arange(length)))
    by_dist = ((jnp.take(tab, bucket, axis=0) - tab[REL_BUCKETS - 1]) * LOG2E).T

    def skew(v):
        flat = jnp.tile(v, (1, t))[:, :t * (length - 1)]
        return flat.reshape(-1, t, length - 1)[:, :, :t]

    diag = skew(by_dist)
    prev = skew(jnp.roll(by_dist, -t, axis=1))
    key = np.arange(t)[:, None]
    query = np.arange(t)[None, :]
    diag = jnp.where(jnp.asarray(query >= key)[None], diag, NEG)
    return jnp.stack([jnp.full_like(prev, NEG), prev, diag], axis=1).astype(F32)


def _diff_attn_kernel(lam_ref, qt_ref, k_ref, vt_ref, bias_ref, g_ref, o_ref, acc_ref, s0_ref, s1_ref, pa_ref, pb_ref,
                      *, t, heads, out_scale):
    qi = pl.program_id(2)
    sub = lax.broadcasted_iota(jnp.int32, (HEAD_DIM, t), 0)
    rhs = []
    for g in range(heads):
        qt = qt_ref[0, g * HEAD_DIM:(g + 1) * HEAD_DIM, :]
        zero = jnp.zeros_like(qt)
        rhs += [jnp.where(sub < A_HALF, qt, zero), jnp.where(sub >= A_HALF, qt, zero)]

    def lhs_block(kb, c):
        g = c // 2
        return k_ref[0, pl.ds(pl.multiple_of(kb * t, t), t), g * HEAD_DIM:(g + 1) * HEAD_DIM]

    prev_kind = jnp.where(qi >= 1, PREV_BLOCK, NO_BLOCK)
    _attend_causal_blocks(
        qi, t, 2 * heads, lhs_block, lambda c: rhs[c], lambda kb, c: vt_ref[0, c // 2, kb],
        lambda kb: None,
        lambda kb: (lambda c: bias_ref[c // 2, prev_kind]),
        lambda kb: (lambda c: bias_ref[c // 2, DIAG_BLOCK]),
        acc_ref, s0_ref, s1_ref, pa_ref, pb_ref)

    for g in range(heads):
        acc = acc_ref[:, 2 * g * t:(2 * g + 2) * t]
        o = acc[:HEAD_DIM] * (1.0 / acc[HEAD_DIM:HEAD_DIM + 1])
        o = o[:, :t] - lam_ref[0] * o[:, t:]
        y = o * lax.rsqrt(jnp.mean(o * o, axis=0, keepdims=True) + EPS) * g_ref[...] * out_scale
        o_ref[0, :, g * HEAD_DIM:(g + 1) * HEAD_DIM] = y.T.astype(o_ref.dtype)


def _diff_attention(qt, k, vt, lam, bias, sub_g, out_scale, heads):
    b, s, _ = k.shape
    nkb, t = vt.shape[2], vt.shape[4]
    hw = heads * HEAD_DIM
    nq = 2 * heads * t
    kern = functools.partial(_diff_attn_kernel, t=t, heads=heads, out_scale=out_scale)
    return pl.pallas_call(
        kern,
        grid=(b, A_HEADS // heads, s // t),
        in_specs=[pl.BlockSpec(memory_space=pltpu.SMEM),
                  pl.BlockSpec((1, hw, t), lambda bi, h, qi: (bi, h, qi)),
                  pl.BlockSpec((1, s, hw), lambda bi, h, qi: (bi, 0, h)),
                  pl.BlockSpec((1, heads, nkb, HEAD_DIM + SUM_ROWS, t), lambda bi, h, qi: (bi, h, 0, 0, 0)),
                  pl.BlockSpec((heads, 3, t, t), lambda bi, h, qi: (h, 0, 0, 0)),
                  pl.BlockSpec((HEAD_DIM, 1), lambda bi, h, qi: (0, 0))],
        out_specs=pl.BlockSpec((1, t, hw), lambda bi, h, qi: (bi, qi, h)),
        out_shape=jax.ShapeDtypeStruct((b, s, A_WIDTH), BF16),
        scratch_shapes=[pltpu.VMEM((HEAD_DIM + SUM_ROWS, nq), F32),
                        pltpu.VMEM((t, nq), F32), pltpu.VMEM((t, nq), F32),
                        pltpu.VMEM((t, nq), BF16), pltpu.VMEM((t, nq), BF16)],
        compiler_params=_params(("parallel", "parallel", "arbitrary"), 56),
        name="diff_attention",
    )(lam, qt, k, vt, bias, sub_g.reshape(HEAD_DIM, 1))


def _dsa_prep_kernel(cq_ref, ckv_ref, kk_ref, wi_ref, gq_ref, gkv_ref, wuq_ref, wuk_ref, wqit_ref,
                     qlat_ref, qidx_ref, wis_ref, kkb_ref, ckvn_ref, ckvnt_ref, *, t):
    cqn = _rms(cq_ref[0], gq_ref[...]).astype(BF16)
    tm = cqn.shape[0]
    q = _dot(cqn, wuq_ref[...])
    for h in range(B_HEADS):
        qh = q[:, h * HEAD_DIM:(h + 1) * HEAD_DIM].astype(BF16)
        qlat = (_dot_nt(wuk_ref[h], qh) * (HEAD_DIM ** -0.5 * LOG2E)).astype(BF16)
        for blk in range(tm // t):
            qlat_ref[0, blk, :, h * t:(h + 1) * t] = qlat[:, blk * t:(blk + 1) * t]
    qidx_ref[0] = _dot_nt(wqit_ref[...], cqn).astype(BF16)
    wis = wi_ref[0] * (IDX_HEADS ** -0.5 * IDX_DIM ** -0.5)
    wis_ref[0] = wis.T[:IDX_HEADS]
    kkb_ref[0] = kk_ref[0].astype(BF16)
    ckvn = _rms(ckv_ref[0], gkv_ref[...])
    ckvn_ref[0] = ckvn.astype(BF16)
    ckvn_t = ckvn.T.astype(BF16)
    for blk in range(tm // t):
        ckvnt_ref[0, blk, :B_KV_RANK, :] = ckvn_t[:, blk * t:(blk + 1) * t]
    ckvnt_ref[0, :, B_KV_RANK:, :] = _sum_rows((tm // t,), t, BF16)


def _dsa_prep(rest, gq, gkv, wuq, wuk, wqit, t, tm):
    b, s, _ = rest.shape
    col = lambda c: (lambda bi, i: (bi, i, c))
    const = lambda n: (lambda bi, i: (0,) * n)
    kern = functools.partial(_dsa_prep_kernel, t=t)
    return pl.pallas_call(
        kern,
        grid=(b, s // tm),
        in_specs=[pl.BlockSpec((1, tm, B_Q_RANK), col(2)),
                  pl.BlockSpec((1, tm, B_KV_RANK), col(6)),
                  pl.BlockSpec((1, tm, 128), col(14)),
                  pl.BlockSpec((1, tm, 128), col(15)),
                  pl.BlockSpec((1, B_Q_RANK), const(2)),
                  pl.BlockSpec((1, B_KV_RANK), const(2)),
                  pl.BlockSpec((B_Q_RANK, B_WIDTH), const(2)),
                  pl.BlockSpec((B_HEADS, B_KV_RANK, HEAD_DIM), const(3)),
                  pl.BlockSpec((IDX_HEADS * IDX_DIM, B_Q_RANK), const(2))],
        out_specs=[pl.BlockSpec((1, tm // t, B_KV_RANK, B_HEADS * t), lambda bi, i: (bi, i, 0, 0)),
                   pl.BlockSpec((1, IDX_HEADS * IDX_DIM, tm), lambda bi, i: (bi, 0, i)),
                   pl.BlockSpec((1, IDX_HEADS, tm), lambda bi, i: (bi, 0, i)),
                   pl.BlockSpec((1, tm, 128), lambda bi, i: (bi, i, 0)),
                   pl.BlockSpec((1, tm, B_KV_RANK), lambda bi, i: (bi, i, 0)),
                   pl.BlockSpec((1, tm // t, B_KV_RANK + SUM_ROWS, t), lambda bi, i: (bi, i, 0, 0))],
        out_shape=[jax.ShapeDtypeStruct((b, s // t, B_KV_RANK, B_HEADS * t), BF16),
                   jax.ShapeDtypeStruct((b, IDX_HEADS * IDX_DIM, s), BF16),
                   jax.ShapeDtypeStruct((b, IDX_HEADS, s), F32),
                   jax.ShapeDtypeStruct((b, s, 128), BF16),
                   jax.ShapeDtypeStruct((b, s, B_KV_RANK), BF16),
                   jax.ShapeDtypeStruct((b, s // t, B_KV_RANK + SUM_ROWS, t), BF16)],
        compiler_params=_params(("parallel", "parallel"), 40),
        name="dsa_prep",
    )(rest, rest, rest, rest, gq.reshape(1, -1), gkv.reshape(1, -1), wuq, wuk, wqit)


def _dsa_kernel(qlat_ref, qidx_ref, wis_ref, kk_ref, ckvn_ref, ckvnt_ref, bias_ref, wuv_ref, o_ref,
                key_ref, hi_ref, lo_ref, acc_ref, s0_ref, s1_ref, pa_ref, pb_ref, *, t, topk):
    qi = pl.program_id(1)
    nblk = qi + 1
    nkb = key_ref.shape[0]
    key_pos = lax.broadcasted_iota(jnp.int32, (t, t), 0)
    query_pos = lax.broadcasted_iota(jnp.int32, (t, t), 1)

    qidx = qidx_ref[0]
    wis = wis_ref[0]
    sub = lax.broadcasted_iota(jnp.int32, (2 * IDX_DIM, t), 0)
    q_heads = []
    for g in range(IDX_HEADS):
        pair = qidx[(g // 2) * 2 * IDX_DIM:(g // 2 + 1) * 2 * IDX_DIM]
        keep = (sub < IDX_DIM) if g % 2 == 0 else (sub >= IDX_DIM)
        q_heads.append(jnp.where(keep, pair, jnp.zeros_like(pair)))

    def score_block(kb):
        ks = pl.multiple_of(kb * t, t)
        kblk = kk_ref[0, pl.ds(ks, t), :]
        score = jnp.zeros((t, t), F32)
        for g in range(IDX_HEADS):
            d = _dot(kblk, q_heads[g])
            score = score + jnp.maximum(d, 0.0) * wis[g:g + 1]
        bits = pltpu.bitcast(score, jnp.int32)
        key = jnp.where(bits < 0, bits ^ jnp.int32(0x7FFFFFFF), bits)
        causal = (kb < qi) | (query_pos >= key_pos)
        key = jnp.where(causal, key, jnp.int32(INT_MIN))
        key_ref[kb] = key
        hi_ref[kb] = (key >> 16).astype(jnp.int16)
        lo_ref[kb] = ((key & 0xFFFF) + I16_MIN).astype(jnp.int16)

    def score_pair(i, c):
        score_block(2 * i)
        score_block(jnp.minimum(2 * i + 1, qi))
        return c

    lax.fori_loop(0, (nblk + 1) // 2, score_pair, 0)

    hi_ref[nkb] = jnp.full((t, t), I16_MIN, jnp.int16)
    lo_ref[nkb] = jnp.full((t, t), I16_MIN, jnp.int16)
    one, zero = jnp.int16(1), jnp.int16(0)

    def count_ge(plane_ref, cand):
        cand_b = jnp.broadcast_to(cand.astype(jnp.int16), (t, t))

        def body(i, c):
            for u in range(SCAN_UNROLL):
                kb = SCAN_UNROLL * i + u
                x = jnp.where(plane_ref[jnp.where(kb < nblk, kb, nkb)] >= cand_b, one, zero)
                parts = [x[r * 16:(r + 1) * 16] for r in range(t // 16)]
                while len(parts) > 1:
                    parts = [a + b for a, b in zip(parts[0::2], parts[1::2])]
                c = c + parts[0]
            return c

        trips = (nblk + SCAN_UNROLL - 1) // SCAN_UNROLL
        c = lax.fori_loop(0, trips, body, jnp.zeros((16, t), jnp.int16))
        return jnp.sum(c.astype(jnp.int32), axis=0, keepdims=True)

    def greedy_bits(count_at_least):
        def bit_step(i, v):
            cand = v + jnp.left_shift(jnp.int32(1), 15 - i)
            return jnp.where(count_at_least(cand) >= topk, cand, v)
        return lax.fori_loop(0, 16, bit_step, jnp.full((1, t), I16_MIN, jnp.int32))

    hi = greedy_bits(lambda cand: count_ge(hi_ref, cand))
    above = count_ge(hi_ref, hi + 1)
    hi_b = jnp.broadcast_to(hi.astype(jnp.int16), (t, t))

    def restrict(kb, c):
        lo_ref[kb] = jnp.where(hi_ref[kb] == hi_b, lo_ref[kb], jnp.int16(I16_MIN))
        return c

    lax.fori_loop(0, nblk, restrict, 0)
    lo = greedy_bits(lambda cand: above + count_ge(lo_ref, cand))
    thr = jnp.left_shift(hi, 16) + (lo - I16_MIN)
    thr_b = jnp.broadcast_to(jnp.maximum(thr, jnp.int32(INT_MIN + 1)), (t, t))

    def count_keys(pred):
        def body(kb, c):
            x = jnp.where(pred(key_ref[kb], kb * t), 1, 0)
            return c + jnp.sum(x.reshape(t // 8, 8, t), axis=0)
        c = lax.fori_loop(0, nblk, body, jnp.zeros((8, t), jnp.int32))
        return jnp.sum(c, axis=0, keepdims=True)

    selected = count_keys(lambda key, base: key >= thr_b)

    @pl.when(jnp.max(selected) > topk)
    def _():
        need = topk - count_keys(lambda key, base: key > thr_b)

        def index_bit(i, cut):
            cand = cut + jnp.left_shift(jnp.int32(1), index_bits - 1 - i)
            cand_b = jnp.broadcast_to(cand, (t, t))
            below = count_keys(lambda key, base: (key == thr_b) & (key_pos + base < cand_b))
            return jnp.where(below < need, cand, cut)

        index_bits = max(1, (nkb * t - 1).bit_length())
        cut = lax.fori_loop(0, index_bits, index_bit, jnp.zeros((1, t), jnp.int32))
        cut_b = jnp.broadcast_to(cut, (t, t))

        def demote(kb, c):
            key = key_ref[kb]
            key_ref[kb] = jnp.where((key == thr_b) & (key_pos + kb * t > cut_b), key - 1, key)
            return c

        lax.fori_loop(0, nblk, demote, 0)

    def lhs_block(kb, h):
        return ckvn_ref[0, pl.ds(pl.multiple_of(kb * t, t), t), :]

    def selection(kb):
        return jnp.where(key_ref[kb] >= thr_b, 0.0, NEG)

    def far_addend(kb):
        mask = selection(kb)
        return lambda h: mask

    def near_addend(kind):
        def addend(kb):
            mask = selection(kb)
            return lambda h: bias_ref[h, kind] + mask
        return addend

    _attend_causal_blocks(
        qi, t, B_HEADS, lhs_block, lambda h: qlat_ref[0, 0, :, h * t:(h + 1) * t],
        lambda kb, h: ckvnt_ref[0, kb],
        far_addend, near_addend(jnp.where(qi >= 1, PREV_BLOCK, NO_BLOCK)), near_addend(DIAG_BLOCK),
        acc_ref, s0_ref, s1_ref, pa_ref, pb_ref)

    for h in range(B_HEADS):
        cols = slice(h * t, (h + 1) * t)
        o_lat = (acc_ref[:B_KV_RANK, cols] * (1.0 / acc_ref[B_KV_RANK:B_KV_RANK + 1, cols])).astype(BF16)
        o = _dot(wuv_ref[h], o_lat)
        o_ref[0, :, h * HEAD_DIM:(h + 1) * HEAD_DIM] = o.T.astype(o_ref.dtype)


def _dsa_attention(qlat_t, qidx_t, wis_t, kk, ckvn, ckvn_t, bias, wuv):
    b, s, _ = ckvn.shape
    nkb, t = ckvn_t.shape[1], ckvn_t.shape[3]
    topk = min(TOPK_MAX, s // 4)
    kern = functools.partial(_dsa_kernel, t=t, topk=topk)
    qcol = lambda bi, qi: (bi, 0, qi)
    allk = lambda bi, qi: (bi, 0, 0)
    return pl.pallas_call(
        kern,
        grid=(b, s // t),
        in_specs=[pl.BlockSpec((1, 1, B_KV_RANK, B_HEADS * t), lambda bi, qi: (bi, qi, 0, 0)),
                  pl.BlockSpec((1, IDX_HEADS * IDX_DIM, t), qcol),
                  pl.BlockSpec((1, IDX_HEADS, t), qcol),
                  pl.BlockSpec((1, s, 128), allk),
                  pl.BlockSpec((1, s, B_KV_RANK), allk),
                  pl.BlockSpec((1, nkb, B_KV_RANK + SUM_ROWS, t), lambda bi, qi: (bi, 0, 0, 0)),
                  pl.BlockSpec((B_HEADS, 3, t, t), lambda bi, qi: (0, 0, 0, 0)),
                  pl.BlockSpec((B_HEADS, HEAD_DIM, B_KV_RANK), lambda bi, qi: (0, 0, 0))],
        out_specs=pl.BlockSpec((1, t, B_WIDTH), lambda bi, qi: (bi, qi, 0)),
        out_shape=jax.ShapeDtypeStruct((b, s, B_WIDTH), BF16),
        scratch_shapes=[pltpu.VMEM((nkb, t, t), jnp.int32),
                        pltpu.VMEM((nkb + 1, t, t), jnp.int16), pltpu.VMEM((nkb + 1, t, t), jnp.int16),
                        pltpu.VMEM((B_KV_RANK + SUM_ROWS, B_HEADS * t), F32),
                        pltpu.VMEM((t, B_HEADS * t), F32), pltpu.VMEM((t, B_HEADS * t), F32),
                        pltpu.VMEM((t, B_HEADS * t), BF16), pltpu.VMEM((t, B_HEADS * t), BF16)],
        compiler_params=_params(("parallel", "arbitrary"), 56),
        name="dsa_attention",
    )(qlat_t, qidx_t, wis_t, kk, ckvn, ckvn_t, bias, wuv)


def _gelu(x):
    return x * (0.5 * (1.0 + jnp.tanh(math.sqrt(2.0 / math.pi) * (x + 0.044715 * (x * x * x)))))


def _gmlp_kernel(cu_ref, cv_ref, g_ref, b_ref, ws_ref, bs_ref, o_ref, *, ts):
    u = _gelu(cu_ref[...])
    v = _gelu(cv_ref[...])
    mu = jnp.mean(v, axis=-1, keepdims=True)
    vc = v - mu
    var = jnp.mean(vc * vc, axis=-1, keepdims=True)
    vn = (vc * lax.rsqrt(var + EPS) * g_ref[...] + b_ref[...]).astype(BF16)
    row = lax.broadcasted_iota(jnp.int32, (CHUNK, CHUNK), 0)
    col = lax.broadcasted_iota(jnp.int32, (CHUNK, CHUNK), 1)
    for g in range(C_GROUPS):
        w = jnp.where(row >= col, ws_ref[g], 0.0).astype(BF16)
        bcol = bs_ref[:, g:g + 1]
        cs = slice(g * 128, (g + 1) * 128)
        for c in range(ts // CHUNK):
            rs = slice(c * CHUNK, (c + 1) * CHUNK)
            y = _dot(w, vn[rs, cs]) + bcol
            o_ref[rs, cs] = (u[rs, cs] * y).astype(o_ref.dtype)


def _gmlp(rest, ln_g, ln_b, w_s, b_s, ts):
    m = rest.shape[0]
    kern = functools.partial(_gmlp_kernel, ts=ts)
    return pl.pallas_call(
        kern,
        grid=(m // ts,),
        in_specs=[pl.BlockSpec((ts, C_WIDTH), lambda i: (i, 0)),
                  pl.BlockSpec((ts, C_WIDTH), lambda i: (i, 1)),
                  pl.BlockSpec((1, C_WIDTH), lambda i: (0, 0)),
                  pl.BlockSpec((1, C_WIDTH), lambda i: (0, 0)),
                  pl.BlockSpec((C_GROUPS, CHUNK, CHUNK), lambda i: (0, 0, 0)),
                  pl.BlockSpec((CHUNK, C_GROUPS), lambda i: (0, 0))],
        out_specs=pl.BlockSpec((ts, C_WIDTH), lambda i: (i, 0)),
        out_shape=jax.ShapeDtypeStruct((m, C_WIDTH), BF16),
        compiler_params=_params(("parallel",), 32),
        name="gmlp",
    )(rest, rest, ln_g.reshape(1, -1), ln_b.reshape(1, -1), w_s, jnp.transpose(b_s))


def _out_proj_kernel(h_ref, oa_ref, ob_ref, oc_ref, wa_ref, wb_ref, wc_ref, g_ref, o_ref):
    y = _dot(oa_ref[...], wa_ref[...]) + _dot(ob_ref[...], wb_ref[...]) + _dot(oc_ref[...], wc_ref[...])
    o_ref[...] = h_ref[...] + _rms(y, g_ref[...])


def _out_proj(h, oa, ob, oc, wa, wb, wc, g, tm):
    m, d = h.shape
    rowblk = lambda i: (i, 0)
    full = lambda i: (0, 0)
    return pl.pallas_call(
        _out_proj_kernel,
        grid=(m // tm,),
        in_specs=[pl.BlockSpec((tm, d), rowblk),
                  pl.BlockSpec((tm, A_WIDTH), rowblk),
                  pl.BlockSpec((tm, B_WIDTH), rowblk),
                  pl.BlockSpec((tm, C_WIDTH), rowblk),
                  pl.BlockSpec((A_WIDTH, d), full),
                  pl.BlockSpec((B_WIDTH, d), full),
                  pl.BlockSpec((C_WIDTH, d), full),
                  pl.BlockSpec((1, d), full)],
        out_specs=pl.BlockSpec((tm, d), rowblk),
        out_shape=jax.ShapeDtypeStruct((m, d), F32),
        compiler_params=_params(("parallel",), 48),
        name="out_proj",
    )(h, oa, ob, oc, wa, wb, wc, g.reshape(1, d))


def _xattn_kernel(h_ref, gpre_ref, wq_ref, k_ref, v_ref, wo_ref, gpost_ref, o_ref):
    x = h_ref[0]
    xn = _rms(x, gpre_ref[...]).astype(BF16)
    q = _dot(xn, wq_ref[...])
    scale = X_DIM ** -0.5
    outs = []
    for hh in range(X_HEADS):
        cs = slice(hh * X_DIM, (hh + 1) * X_DIM)
        s = _dot_nt(q[:, cs].astype(BF16), k_ref[0, :, cs]) * scale
        p = jnp.exp(s - jnp.max(s, axis=-1, keepdims=True))
        p = p * (1.0 / jnp.sum(p, axis=-1, keepdims=True))
        outs.append(_dot(p.astype(BF16), v_ref[0, :, cs]).astype(BF16))
    o = jnp.concatenate(outs, axis=1)
    xa = _dot(o, wo_ref[...])
    o_ref[0] = x + _rms(xa, gpost_ref[...])


def _xattn(h, kv, gpre, wq, wo, gpost, tm):
    b, s, d = h.shape
    mlen = kv.shape[1]
    hw = X_HEADS * X_DIM
    return pl.pallas_call(
        _xattn_kernel,
        grid=(b, s // tm),
        in_specs=[pl.BlockSpec((1, tm, d), lambda bi, i: (bi, i, 0)),
                  pl.BlockSpec((1, d), lambda bi, i: (0, 0)),
                  pl.BlockSpec((d, hw), lambda bi, i: (0, 0)),
                  pl.BlockSpec((1, mlen, hw), lambda bi, i: (bi, 0, 0)),
                  pl.BlockSpec((1, mlen, hw), lambda bi, i: (bi, 0, 1)),
                  pl.BlockSpec((hw, d), lambda bi, i: (0, 0)),
                  pl.BlockSpec((1, d), lambda bi, i: (0, 0))],
        out_specs=pl.BlockSpec((1, tm, d), lambda bi, i: (bi, i, 0)),
        out_shape=jax.ShapeDtypeStruct((b, s, d), F32),
        compiler_params=_params(("parallel", "parallel"), 48),
        name="xattn",
    )(h, gpre.reshape(1, d), wq, kv, kv, wo, gpost.reshape(1, d))


def _ffn_kernel(x_ref, xprev_ref, gpre_ref, wg_ref, wu_ref, wd_ref, gpost_ref, o_ref, xn_ref, acc_ref, done_ref):
    i, j = pl.program_id(0), pl.program_id(1)
    n_tiles = pl.num_programs(0) - 1
    last = pl.num_programs(1) - 1

    @pl.when((i == 0) & (j == 0))
    def _():
        done_ref[...] = jnp.zeros(done_ref.shape, F32)

    def finish_previous():
        o_ref[...] = xprev_ref[...] + _rms(done_ref[...], gpost_ref[...])

    def hidden_step(xn):
        gate = _dot(xn, wg_ref[...])
        up = _dot(xn, wu_ref[...])
        act = gate * (1.0 / (1.0 + jnp.exp(-gate))) * up
        return _dot(act.astype(BF16), wd_ref[...])

    def first_step():
        finish_previous()
        xn = _rms(x_ref[...], gpre_ref[...]).astype(BF16)
        xn_ref[...] = xn
        acc_ref[...] = hidden_step(xn)

    def middle_step():
        acc_ref[...] += hidden_step(xn_ref[...])

    def last_step():
        done_ref[...] = acc_ref[...] + hidden_step(xn_ref[...])

    def flush_step():
        lax.cond(j == 0, finish_previous, lambda: None)

    def work_step():
        lax.cond(j == 0, first_step, lambda: lax.cond(j == last, last_step, middle_step))

    lax.cond(i < n_tiles, work_step, flush_step)


def _ffn(h, gpre, w_gu, w_down, gpost, tm, tf):
    m, d = h.shape
    f = w_down.shape[0]
    nf = f // tf
    n_tiles = m // tm
    row = lambda i: jnp.minimum(i, n_tiles - 1)
    hid = lambda i, j: jnp.where(i < n_tiles, j, nf - 1)
    done = lambda i, j: jnp.clip(i - (j == 0).astype(jnp.int32), 0, n_tiles - 1)
    return pl.pallas_call(
        _ffn_kernel,
        grid=(n_tiles + 1, nf),
        in_specs=[pl.BlockSpec((tm, d), lambda i, j: (row(i), 0)),
                  pl.BlockSpec((tm, d), lambda i, j: (jnp.maximum(i - 1, 0), 0)),
                  pl.BlockSpec((1, d), lambda i, j: (0, 0)),
                  pl.BlockSpec((d, tf), lambda i, j: (0, hid(i, j))),
                  pl.BlockSpec((d, tf), lambda i, j: (0, nf + hid(i, j))),
                  pl.BlockSpec((tf, d), lambda i, j: (hid(i, j), 0)),
                  pl.BlockSpec((1, d), lambda i, j: (0, 0))],
        out_specs=pl.BlockSpec((tm, d), lambda i, j: (done(i, j), 0)),
        out_shape=jax.ShapeDtypeStruct((m, d), F32),
        scratch_shapes=[pltpu.VMEM((tm, d), BF16), pltpu.VMEM((tm, d), F32), pltpu.VMEM((tm, d), F32)],
        compiler_params=_params(("arbitrary", "arbitrary"), 58),
        name="ffn",
    )(h, h, gpre.reshape(1, d), w_gu, w_gu, w_down, gpost.reshape(1, d))


def _tile(n, pref):
    t = min(pref, n)
    assert n % t == 0
    return t


def kernel(x, mem, rel_bias, mix_pre_g, mix_post_g, w_in, w_out, a_lambda, a_sub_g, b_cq_g, b_ckv_g, b_w_uq, b_w_qidx, b_w_uk, b_w_uv, c_ln_g, c_ln_b, c_w_s, c_b_s, x_pre_g, x_post_g, mem_g, x_wq, x_wkv, x_wo, f_pre_g, f_post_g, f_w_gu, f_w_down):
    bsz, s_len, d = x.shape
    depth = w_in.shape[0]
    tokens = bsz * s_len
    mlen = mem.shape[1]
    t_attn = min(ATTN_BLOCK, s_len)
    bias_a = _bias_tiles(rel_bias[:, :A_HEADS], t_attn)
    bias_b = _bias_tiles(rel_bias[:, A_HEADS:], t_attn)
    sizes = (A_WIDTH, A_WIDTH, A_WIDTH, B_Q_RANK, B_KV_RANK, IDX_DIM, IDX_HEADS, C_WIDTH, C_WIDTH)
    offs = np.concatenate([[0], np.cumsum(sizes)])
    tm = _tile(tokens, 512)

    h = x.reshape(tokens, d)
    mem2 = mem.reshape(bsz * mlen, d)
    for l in range(depth):
        lam_init = 0.8 - 0.6 * math.exp(-0.3 * l)
        wl = w_in[l]
        cols = [wl[:, offs[i]:offs[i + 1]] for i in range(len(sizes))]
        wqa, wka, wva, wcq, wckv, wkidx, wwidx, wcu, wcv = cols
        w_k_rest = jnp.concatenate(
            [wka, wcu, wcv, wcq, wckv, wkidx, wkidx, wwidx,
             jnp.zeros((d, 128 - IDX_HEADS), wl.dtype)], axis=1).astype(BF16)
        wqt = jnp.transpose(wqa * (A_HALF ** -0.5 * LOG2E)).astype(BF16)
        wvt = jnp.transpose(wva).astype(BF16)

        qt, k, vt, rest = _in_proj(h.reshape(bsz, s_len, d), mix_pre_g[l], wqt, wvt, w_k_rest,
                                   t_attn, _tile(s_len, 512))

        lp = a_lambda[l].astype(F32)
        lam = jnp.exp(jnp.sum(lp[0] * lp[1])) - jnp.exp(jnp.sum(lp[2] * lp[3])) + lam_init
        oa = _diff_attention(qt, k, vt, lam.reshape(1), bias_a, a_sub_g[l], 1.0 - lam_init,
                             DIFF_HEADS_PER_STEP)

        wuq = b_w_uq[l].reshape(B_Q_RANK, B_WIDTH).astype(BF16)
        wuk = jnp.transpose(b_w_uk[l], (1, 0, 2)).astype(BF16)
        wqit = jnp.transpose(b_w_qidx[l].reshape(B_Q_RANK, IDX_HEADS * IDX_DIM)).astype(BF16)
        wuv = jnp.transpose(b_w_uv[l], (1, 2, 0)).astype(BF16)
        prep = _dsa_prep(rest, b_cq_g[l], b_ckv_g[l], wuq, wuk, wqit, t_attn, _tile(s_len, 512))
        ob = _dsa_attention(*prep, bias_b, wuv)

        rest = rest.reshape(tokens, rest.shape[-1])
        oc = _gmlp(rest, c_ln_g[l], c_ln_b[l], c_w_s[l], c_b_s[l], tm)

        wo_l = w_out[l].astype(BF16)
        h = _out_proj(h, oa.reshape(tokens, A_WIDTH), ob.reshape(tokens, B_WIDTH), oc,
                      wo_l[:A_WIDTH], wo_l[A_WIDTH:A_WIDTH + B_WIDTH], wo_l[A_WIDTH + B_WIDTH:],
                      mix_post_g[l], tm)

        hw = X_HEADS * X_DIM
        kv = _norm_matmul(mem2, mem_g[l], x_wkv[l].reshape(d, 2 * hw).astype(BF16), BF16,
                          _tile(bsz * mlen, 512), 512)
        h = _xattn(h.reshape(bsz, s_len, d), kv.reshape(bsz, mlen, 2 * hw), x_pre_g[l],
                   x_wq[l].reshape(d, hw).astype(BF16), x_wo[l].reshape(hw, d).astype(BF16),
                   x_post_g[l], _tile(s_len, 512)).reshape(tokens, d)

        fh = f_w_down.shape[1]
        h = _ffn(h, f_pre_g[l], f_w_gu[l].reshape(d, 2 * fh).astype(BF16),
                 f_w_down[l].astype(BF16), f_post_g[l], tm, 512)
    return h.reshape(bsz, s_len, d)
```
